```python
import math
import jax
import jax.numpy as jnp
from jax import lax
import numpy as np

D_MODEL = 2048
BATCH = 2
SEQ = 4096
DEPTH = 4
DEC_BATCH = 32
DEC_SEQ = 4
PAST_LEN = 16384
PAGE_SIZE = 128

N_BRANCH = 4
BRANCH_WIDTH = D_MODEL // 2
GLA_HEADS = 4
GLA_QK = BRANCH_WIDTH // 2
GLA_V = BRANCH_WIDTH
GLA_DK = GLA_QK // GLA_HEADS
GLA_DV = GLA_V // GLA_HEADS
GLA_GATE_RANK = 16
GLA_TAU = 16.0
GLA_CHUNK = 64
SWA_HEAD_DIM = 64
SWA_Q_HEADS = BRANCH_WIDTH // SWA_HEAD_DIM
SWA_KV_HEADS = 2
WINDOW = 128
SWA_BLOCK = 128
RWKV_HEAD = 64
RWKV_HEADS = BRANCH_WIDTH // RWKV_HEAD
RWKV_DECAY_RANK = 64
RWKV_A_RANK = 64
RWKV_GATE_RANK = 128
RWKV_SIZES = (BRANCH_WIDTH, BRANCH_WIDTH, BRANCH_WIDTH, RWKV_DECAY_RANK, RWKV_A_RANK, RWKV_GATE_RANK)
RWKV_COLS = 3 * BRANCH_WIDTH + RWKV_DECAY_RANK + RWKV_A_RANK + RWKV_GATE_RANK
MEM_TOKENS = 256
MEM_HEADS = 4
MEM_HEAD_DIM = BRANCH_WIDTH // MEM_HEADS
D_FF = -(-8 * D_MODEL // (3 * 256)) * 256
DEEPNORM_ALPHA = (2 * DEPTH) ** 0.25
DEEPNORM_BETA = (8 * DEPTH) ** -0.25
NEG = -1e30
IN_SIZES = (GLA_QK, GLA_QK, GLA_V, GLA_V, GLA_GATE_RANK,
            SWA_Q_HEADS * SWA_HEAD_DIM, SWA_KV_HEADS * SWA_HEAD_DIM, SWA_KV_HEADS * SWA_HEAD_DIM,
            RWKV_COLS, MEM_HEADS * MEM_HEAD_DIM, N_BRANCH * D_MODEL)

kernel_name = 'hybrid_gla_swa_rwkv7_decoder_step'


def _split(u, sizes):
    pts, acc = [], 0
    for s in sizes[:-1]:
        acc += s
        pts.append(acc)
    return jnp.split(u, pts, axis=-1)


def _heads(t, h):
    return t.reshape(t.shape[:-1] + (h, t.shape[-1] // h))


def layer_norm(x, g, b, eps=1e-5):
    xf = x.astype(jnp.float32)
    mu = xf.mean(-1, keepdims=True)
    var = jnp.mean(jnp.square(xf - mu), -1, keepdims=True)
    return ((xf - mu) * lax.rsqrt(var + eps) * g + b).astype(x.dtype)


def head_norm(o, g, b, eps):
    of = o.astype(jnp.float32)
    mu = of.mean(-1, keepdims=True)
    var = jnp.mean(jnp.square(of - mu), -1, keepdims=True)
    n = (of - mu) * lax.rsqrt(var + eps)
    return n.reshape(o.shape[:-2] + (-1,)) * g + b


def alibi_slopes():
    return 2.0 ** (-8.0 * jnp.arange(1, SWA_Q_HEADS + 1, dtype=jnp.float32) / SWA_Q_HEADS)


def sink_softmax(s, sink):
    m = jnp.maximum(s.max(-1, keepdims=True), sink)
    p = jnp.exp(s - m)
    return p / (p.sum(-1, keepdims=True) + jnp.exp(sink - m))


def gla_chunked(q, k, v, log_a, s0):
    B, L, H, _ = q.shape
    C = math.gcd(GLA_CHUNK, L)
    n = L // C

    def to_chunks(t):
        return t.reshape(B, n, C, H, t.shape[-1]).transpose(1, 0, 3, 2, 4)

    causal = jnp.tril(jnp.ones((C, C), dtype=bool))

    def step(S, inp):
        qi, ki, vi, ai = inp
        b = jnp.cumsum(ai, axis=-2)
        q_dec = qi * jnp.exp(b)
        att = jnp.einsum('bhtk,bhsk->bhts', q_dec, ki * jnp.exp(-b))
        att = jnp.where(causal, att, 0.0)
        o = jnp.einsum('bhts,bhsv->bhtv', att, vi) + jnp.einsum('bhtk,bhkv->bhtv', q_dec, S)
        b_last = b[:, :, -1:, :]
        S = (jnp.exp(b_last[:, :, 0, :])[..., None] * S
             + jnp.einsum('bhsk,bhsv->bhkv', ki * jnp.exp(b_last - b), vi))
        return S, o

    S, o = lax.scan(step, s0, (to_chunks(q), to_chunks(k), to_chunks(v), to_chunks(log_a)))
    return o.transpose(1, 0, 3, 2, 4).reshape(B, L, H, v.shape[-1]), S


def swa_banded(q, k, v, sinks):
    B, L, HQ, D = q.shape
    KV = k.shape[2]
    G = HQ // KV
    T = SWA_BLOCK
    nb = L // T
    qb = q.reshape(B, nb, T, KV, G, D)

    def with_prev(t):
        tb = t.reshape(B, nb, T, KV, D)
        prev = jnp.concatenate([jnp.zeros_like(tb[:, :1]), tb[:, :-1]], axis=1)
        return jnp.concatenate([prev, tb], axis=2)

    k2, v2 = with_prev(k), with_prev(v)
    s = jnp.einsum('bntkgd,bnskd->bnkgts', qb, k2).astype(jnp.float32) * SWA_HEAD_DIM ** -0.5
    dist = T + jnp.arange(T)[:, None] - jnp.arange(2 * T)[None, :]
    valid = (dist >= 0) & (dist <= WINDOW)
    valid = valid[None] & ((jnp.arange(nb) > 0)[:, None, None] | (jnp.arange(2 * T) >= T)[None, None, :])
    slopes = alibi_slopes().reshape(KV, G, 1, 1)
    s = jnp.where(valid[None, :, None, None], s - slopes * dist, NEG)
    p = sink_softmax(s, sinks.astype(jnp.float32).reshape(KV, G, 1, 1))
    o = jnp.einsum('bnkgts,bnskd->bntkgd', p.astype(v.dtype), v2)
    return o.reshape(B, L, HQ, D)


def swa_with_buffer(q, k, v, kbuf, vbuf, sinks):
    B, L, HQ, D = q.shape
    KV = k.shape[2]
    G = HQ // KV
    W = kbuf.shape[1]
    kc = jnp.concatenate([kbuf.astype(k.dtype), k], axis=1)
    vc = jnp.concatenate([vbuf.astype(v.dtype), v], axis=1)
    s = jnp.einsum('btkgd,bskd->bkgts', q.reshape(B, L, KV, G, D), kc).astype(jnp.float32) * SWA_HEAD_DIM ** -0.5
    dist = W + jnp.arange(L)[:, None] - jnp.arange(W + L)[None, :]
    valid = (dist >= 0) & (dist <= WINDOW)
    slopes = alibi_slopes().reshape(KV, G, 1, 1)
    s = jnp.where(valid, s - slopes * dist, NEG)
    p = sink_softmax(s, sinks.astype(jnp.float32).reshape(KV, G, 1, 1))
    o = jnp.einsum('bkgts,bskd->btkgd', p.astype(v.dtype), vc).reshape(B, L, HQ, D)
    return o, kc[:, L:], vc[:, L:]


def rwkv7_scan(r, w, k, v, kk, a, s0):
    def step(S, inp):
        rt, wt, kt, vt, kkt, at = inp
        S = (S * wt[:, :, None, :]
             - jnp.einsum('bhvk,bhk->bhv', S, kkt)[..., None] * (kkt * at)[:, :, None, :]
             + vt[..., None] * kt[:, :, None, :])
        return S, jnp.einsum('bhvk,bhk->bhv', S, rt)

    xs = tuple(t.transpose(1, 0, 2, 3) for t in (r, w, k, v, kk, a))
    S, y = lax.scan(step, s0, xs)
    return y.transpose(1, 0, 2, 3), S


def rwkv7_branch(ru, prev, s0, p):
    B, L, _ = ru.shape
    f32 = jnp.float32
    shifted = jnp.concatenate([prev.astype(ru.dtype), ru[:, :-1]], axis=1)
    xm = (ru + (shifted - ru) * p['rwkv_mu']).astype(f32)
    r, k, v, wd, ad, gd = _split(xm, RWKV_SIZES)
    log_w = -jax.nn.softplus(-(p['rwkv_w0'] + jnp.tanh(wd) @ p['rwkv_w2'])) - 0.5
    decay = jnp.exp(-jnp.exp(log_w))
    a = jax.nn.sigmoid(p['rwkv_a0'] + ad @ p['rwkv_a2'])
    g = jax.nn.sigmoid(gd) @ p['rwkv_g2']
    kk = _heads(k * p['rwkv_k_k'], RWKV_HEADS)
    kk = kk / jnp.maximum(jnp.sqrt(jnp.sum(kk * kk, axis=-1, keepdims=True)), 1e-12)
    k = k * (1.0 + (a - 1.0) * p['rwkv_k_a'])
    rh, kh, vh, ah, wh = (_heads(t, RWKV_HEADS) for t in (r, k, v, a, decay))
    y, S = rwkv7_scan(rh, wh, kh, vh, kk, ah, s0.astype(f32))
    y = head_norm(y, p['rwkv_ln_g'], p['rwkv_ln_b'], 64e-5)
    bonus = (jnp.sum(rh * kh * p['rwkv_r_k'], axis=-1, keepdims=True) * vh).reshape(B, L, BRANCH_WIDTH)
    out = (y + bonus) * g
    return out.astype(ru.dtype), S.astype(s0.dtype), ru[:, -1:]


def mem_attend(q, mk, mv):
    s = jnp.einsum('bthd,bmhd->bhtm', q, mk.astype(q.dtype)).astype(jnp.float32) * MEM_HEAD_DIM ** -0.5
    pr = jax.nn.softmax(s, axis=-1)
    return jnp.einsum('bhtm,bmhd->bthd', pr.astype(q.dtype), mv.astype(q.dtype))


def trunk_layer(h, p, mem_k, mem_v, gla_s0, rwkv_s0, rwkv_prev, swa_kbuf, swa_vbuf):
    B, L, _ = h.shape
    f32 = jnp.float32
    u = h @ p['w_in']
    gq, gk, gv, gr, ga, sq, sk, sv, ru, mq, gpre = _split(u, IN_SIZES)

    log_a = jax.nn.log_sigmoid((ga @ p['gla_a_up'] + p['gla_a_b']).astype(f32)) / GLA_TAU
    o_a, gla_s = gla_chunked(_heads(gq, GLA_HEADS).astype(f32) * GLA_DK ** -0.5,
                             _heads(gk, GLA_HEADS).astype(f32),
                             _heads(gv, GLA_HEADS).astype(f32),
                             _heads(log_a, GLA_HEADS), gla_s0.astype(f32))
    o_a = (head_norm(o_a, p['gla_norm_g'], p['gla_norm_b'], 1e-5) * jax.nn.silu(gr.astype(f32))).astype(h.dtype)

    q_b = _heads(sq, SWA_Q_HEADS)
    k_b = _heads(sk, SWA_KV_HEADS)
    v_b = _heads(sv, SWA_KV_HEADS)
    if swa_kbuf is None:
        o_b = swa_banded(q_b, k_b, v_b, p['swa_sinks'])
        new_kbuf, new_vbuf = k_b[:, L - WINDOW:], v_b[:, L - WINDOW:]
    else:
        o_b, new_kbuf, new_vbuf = swa_with_buffer(q_b, k_b, v_b, swa_kbuf, swa_vbuf, p['swa_sinks'])
    o_b = o_b.reshape(B, L, BRANCH_WIDTH).astype(h.dtype)

    o_c, rwkv_s, rwkv_last = rwkv7_branch(ru, rwkv_prev, rwkv_s0, p)

    o_m = mem_attend(_heads(mq, MEM_HEADS), mem_k, mem_v).reshape(B, L, BRANCH_WIDTH).astype(h.dtype)

    branches = jnp.stack([o_a, o_b, o_c.astype(h.dtype), o_m], axis=2)
    gate = jax.nn.sigmoid(gpre.reshape(B, L, N_BRANCH, D_MODEL) + p['gate_b'])
    merged = jnp.einsum('btnd,btnd->btd', gate, jnp.einsum('btnw,nwd->btnd', branches, p['w_branch']))
    x = layer_norm(DEEPNORM_ALPHA * h + merged @ p['w_out'], p['ln1_g'], p['ln1_b'])

    gg, uu = jnp.split(x @ p['w_gu'], 2, axis=-1)
    x = layer_norm(DEEPNORM_ALPHA * x + (jax.nn.silu(gg) * uu) @ p['w_down'], p['ln2_g'], p['ln2_b'])
    return x, new_kbuf, new_vbuf, gla_s, rwkv_s, rwkv_last


def setup_inputs(seed: int = 0) -> dict:
    key = jax.random.key(seed)
    ks = iter(jax.random.split(key, 64))
    f32 = jnp.float32

    def nrm(shape, scale=1.0):
        return jax.random.normal(next(ks), shape, f32) * scale

    def near(shape, center, noise=0.05):
        return center + nrm(shape, noise)

    beta = DEEPNORM_BETA
    win_rows = min(WINDOW, PAST_LEN)
    n_in = sum(IN_SIZES)
    return {
        'x_prompt': nrm((BATCH, SEQ, D_MODEL)),
        'x_sample': nrm((DEC_BATCH, DEC_SEQ, D_MODEL)),
        'mem_prompt': nrm((BATCH, MEM_TOKENS, D_MODEL)),
        'cache_swa_k': nrm((DEPTH, DEC_BATCH, win_rows, SWA_KV_HEADS, SWA_HEAD_DIM)),
        'cache_swa_v': nrm((DEPTH, DEC_BATCH, win_rows, SWA_KV_HEADS, SWA_HEAD_DIM)),
        'cache_mem_k': nrm((DEPTH, DEC_BATCH, MEM_TOKENS, MEM_HEADS, MEM_HEAD_DIM)),
        'cache_mem_v': nrm((DEPTH, DEC_BATCH, MEM_TOKENS, MEM_HEADS, MEM_HEAD_DIM)),
        'state_gla': nrm((DEPTH, DEC_BATCH, GLA_HEADS, GLA_DK, GLA_DV)),
        'state_rwkv': nrm((DEPTH, DEC_BATCH, RWKV_HEADS, RWKV_HEAD, RWKV_HEAD), 0.3),
        'state_rwkv_shift': nrm((DEPTH, DEC_BATCH, 1, RWKV_COLS)),
        'w_in': nrm((DEPTH, D_MODEL, n_in), D_MODEL ** -0.5),
        'gate_b': nrm((DEPTH, N_BRANCH, D_MODEL), 0.1),
        'gla_a_up': nrm((DEPTH, GLA_GATE_RANK, GLA_QK), GLA_GATE_RANK ** -0.5),
        'gla_a_b': nrm((DEPTH, GLA_QK), 0.1),
        'gla_norm_g': near((DEPTH, GLA_V), 1.0),
        'gla_norm_b': nrm((DEPTH, GLA_V), 0.02),
        'swa_sinks': nrm((DEPTH, SWA_Q_HEADS), 0.5),
        'rwkv_mu': jax.random.uniform(next(ks), (DEPTH, RWKV_COLS), f32),
        'rwkv_w0': nrm((DEPTH, BRANCH_WIDTH), 0.5),
        'rwkv_w2': nrm((DEPTH, RWKV_DECAY_RANK, BRANCH_WIDTH), RWKV_DECAY_RANK ** -0.5),
        'rwkv_a0': nrm((DEPTH, BRANCH_WIDTH), 0.1),
        'rwkv_a2': nrm((DEPTH, RWKV_A_RANK, BRANCH_WIDTH), RWKV_A_RANK ** -0.5),
        'rwkv_g2': nrm((DEPTH, RWKV_GATE_RANK, BRANCH_WIDTH), RWKV_GATE_RANK ** -0.5),
        'rwkv_k_k': near((DEPTH, BRANCH_WIDTH), 0.85),
        'rwkv_k_a': near((DEPTH, BRANCH_WIDTH), 1.0),
        'rwkv_r_k': nrm((DEPTH, RWKV_HEADS, RWKV_HEAD), 0.1),
        'rwkv_ln_g': near((DEPTH, BRANCH_WIDTH), 1.0),
        'rwkv_ln_b': nrm((DEPTH, BRANCH_WIDTH), 0.02),
        'w_mem_kv': nrm((DEPTH, D_MODEL, 2 * MEM_HEADS * MEM_HEAD_DIM), D_MODEL ** -0.5),
        'w_branch': nrm((DEPTH, N_BRANCH, BRANCH_WIDTH, D_MODEL), BRANCH_WIDTH ** -0.5 * beta),
        'w_out': nrm((DEPTH, D_MODEL, D_MODEL), D_MODEL ** -0.5 * beta),
        'ln1_g': near((DEPTH, D_MODEL), 1.0),
        'ln1_b': nrm((DEPTH, D_MODEL), 0.02),
        'w_gu': nrm((DEPTH, D_MODEL, 2 * D_FF), D_MODEL ** -0.5),
        'w_down': nrm((DEPTH, D_FF, D_MODEL), D_FF ** -0.5 * beta),
        'ln2_g': near((DEPTH, D_MODEL), 1.0),
        'ln2_b': nrm((DEPTH, D_MODEL), 0.02),
    }


def reference(x_prompt, x_sample, mem_prompt, cache_swa_k, cache_swa_v, cache_mem_k, cache_mem_v,
              state_gla, state_rwkv, state_rwkv_shift, w_in, gate_b, gla_a_up, gla_a_b, gla_norm_g,
              gla_norm_b, swa_sinks, rwkv_mu, rwkv_w0, rwkv_w2, rwkv_a0, rwkv_a2, rwkv_g2, rwkv_k_k,
              rwkv_k_a, rwkv_r_k, rwkv_ln_g, rwkv_ln_b, w_mem_kv, w_branch, w_out, ln1_g, ln1_b,
              w_gu, w_down, ln2_g, ln2_b):
    stacked = {'w_in': w_in, 'gate_b': gate_b, 'gla_a_up': gla_a_up, 'gla_a_b': gla_a_b,
               'gla_norm_g': gla_norm_g, 'gla_norm_b': gla_norm_b, 'swa_sinks': swa_sinks,
               'rwkv_mu': rwkv_mu, 'rwkv_w0': rwkv_w0, 'rwkv_w2': rwkv_w2, 'rwkv_a0': rwkv_a0,
               'rwkv_a2': rwkv_a2, 'rwkv_g2': rwkv_g2, 'rwkv_k_k': rwkv_k_k, 'rwkv_k_a': rwkv_k_a,
               'rwkv_r_k': rwkv_r_k, 'rwkv_ln_g': rwkv_ln_g, 'rwkv_ln_b': rwkv_ln_b,
               'w_mem_kv': w_mem_kv, 'w_branch': w_branch, 'w_out': w_out, 'ln1_g': ln1_g,
               'ln1_b': ln1_b, 'w_gu': w_gu, 'w_down': w_down, 'ln2_g': ln2_g, 'ln2_b': ln2_b}
    bp = x_prompt.shape[0]
    yp, ys = x_prompt, x_sample
    p_swk, p_swv, p_mk, p_mv, p_gla, p_rw, p_rs = [], [], [], [], [], [], []
    s_swk, s_swv, s_gla, s_rw, s_rs = [], [], [], [], []
    for l in range(DEPTH):
        p = {name: arr[l] for name, arr in stacked.items()}
        mk, mv = jnp.split(mem_prompt @ p['w_mem_kv'], 2, axis=-1)
        mk, mv = _heads(mk, MEM_HEADS), _heads(mv, MEM_HEADS)
        gla0 = jnp.zeros((bp, GLA_HEADS, GLA_DK, GLA_DV), jnp.float32)
        rwkv0 = jnp.zeros((bp, RWKV_HEADS, RWKV_HEAD, RWKV_HEAD), jnp.float32)
        prev0 = jnp.zeros((bp, 1, RWKV_COLS), x_prompt.dtype)
        yp, kb, vb, gs, rs, rl = trunk_layer(yp, p, mk, mv, gla0, rwkv0, prev0, None, None)
        p_swk.append(kb); p_swv.append(vb); p_mk.append(mk); p_mv.append(mv)
        p_gla.append(gs); p_rw.append(rs); p_rs.append(rl)
        ys, kb, vb, gs, rs, rl = trunk_layer(ys, p, cache_mem_k[l], cache_mem_v[l], state_gla[l],
                                            state_rwkv[l], state_rwkv_shift[l],
                                            cache_swa_k[l], cache_swa_v[l])
        s_swk.append(kb); s_swv.append(vb); s_gla.append(gs); s_rw.append(rs); s_rs.append(rl)
    return (yp, ys,
            jnp.stack(p_swk), jnp.stack(p_swv), jnp.stack(p_mk), jnp.stack(p_mv),
            jnp.stack(p_gla), jnp.stack(p_rw), jnp.stack(p_rs),
            jnp.stack(s_swk), jnp.stack(s_swv), jnp.stack(s_gla), jnp.stack(s_rw), jnp.stack(s_rs))
```

```python
import functools

import jax
import jax.numpy as jnp
from jax import lax
from jax.experimental import pallas as pl
from jax.experimental.pallas import tpu as pltpu

f32 = jnp.float32
bf16 = jnp.bfloat16

D_MODEL = 2048
DEPTH = 4
BRANCH_WIDTH = 1024
N_BRANCH = 4
GLA_HEADS = 4
GLA_QK = 512
GLA_V = 1024
GLA_DK = 128
GLA_DV = 256
GLA_GATE_RANK = 16
GLA_TAU = 16.0
GLA_CHUNK = 64
SWA_HEAD_DIM = 64
SWA_Q_HEADS = 16
SWA_KV_HEADS = 2
SWA_GROUP = SWA_Q_HEADS // SWA_KV_HEADS
WINDOW = 128
RWKV_HEAD = 64
RWKV_HEADS = 16
RWKV_LORA = 256
RWKV_COLS = 3 * BRANCH_WIDTH + RWKV_LORA
MEM_TOKENS = 256
MEM_HEADS = 4
MEM_HEAD_DIM = 256
D_FF = 5632
DEEPNORM_ALPHA = (2 * DEPTH) ** 0.25
NEG = -1e30

_O_GQ, _O_GK, _O_GV, _O_GR, _O_GA = 0, 512, 1024, 2048, 3072
_O_SQ, _O_SK, _O_SV = 3088, 4112, 4240
_O_RU = 4368
_O_MQ = 7696
_O_GPRE = 8720
_N_IN = 16912

C_GV, C_GR, C_SQ, C_MQ, C_R, C_K, C_V = 0, 1024, 2048, 3072, 4096, 5120, 6144
C_GPRE = 7168
C_GQ, C_GK = 15360, 15872
C_LORA = 16384
C_SK, C_SV, C_GA = 16640, 16768, 16896
N_PACK = 17408

LANES = 128
SUBLANES = 8
VMEM_BYTES_V7X = 64 * 1024 * 1024
VMEM_LIMIT = VMEM_BYTES_V7X - 8 * 1024 * 1024


def _params(sem):
    return pltpu.CompilerParams(dimension_semantics=sem, vmem_limit_bytes=VMEM_LIMIT)


def _softplus(z):
    return jnp.maximum(z, 0.0) + jnp.log(1.0 + jnp.exp(-jnp.abs(z)))


def _sigmoid(z):
    return 1.0 / (1.0 + jnp.exp(-z))


def _dot(a, b):
    return jnp.dot(a, b, preferred_element_type=f32)


def _dot_nt(a, b):
    return lax.dot_general(a, b, (((1,), (1,)), ((), ())), preferred_element_type=f32)


def _mm_kernel(x_ref, w_ref, o_ref):
    o_ref[...] = _dot(x_ref[...], w_ref[...]).astype(o_ref.dtype)


def matmul(x, w, tm, tn, out_dtype=f32):
    m, k = x.shape
    n = w.shape[1]
    return pl.pallas_call(
        _mm_kernel,
        grid=(m // tm, n // tn),
        in_specs=[pl.BlockSpec((tm, k), lambda i, j: (i, 0)),
                  pl.BlockSpec((k, tn), lambda i, j: (0, j))],
        out_specs=pl.BlockSpec((tm, tn), lambda i, j: (i, j)),
        out_shape=jax.ShapeDtypeStruct((m, n), out_dtype),
        compiler_params=_params(("parallel", "parallel")),
        name="in_proj",
    )(x, w)


def _mm_ln_kernel(x_ref, w_ref, res_ref, g_ref, b_ref, of_ref, ob_ref, acc_ref, *, nk):
    k = pl.program_id(1)

    @pl.when(k == 0)
    def _():
        acc_ref[...] = jnp.zeros_like(acc_ref)

    acc_ref[...] += _dot(x_ref[...], w_ref[...])

    @pl.when(k == nk - 1)
    def _():
        z = DEEPNORM_ALPHA * res_ref[...] + acc_ref[...]
        mu = jnp.mean(z, axis=-1, keepdims=True)
        d = z - mu
        var = jnp.mean(d * d, axis=-1, keepdims=True)
        y = d * lax.rsqrt(var + 1e-5) * g_ref[...] + b_ref[...]
        of_ref[...] = y
        ob_ref[...] = y.astype(bf16)


def matmul_residual_ln(x, w, res, g, b, tm, tk):
    m, k = x.shape
    n = w.shape[1]
    nk = k // tk
    return pl.pallas_call(
        functools.partial(_mm_ln_kernel, nk=nk),
        grid=(m // tm, nk),
        in_specs=[pl.BlockSpec((tm, tk), lambda i, kk: (i, kk)),
                  pl.BlockSpec((tk, n), lambda i, kk: (kk, 0)),
                  pl.BlockSpec((tm, n), lambda i, kk: (i, 0)),
                  pl.BlockSpec((1, n), lambda i, kk: (0, 0)),
                  pl.BlockSpec((1, n), lambda i, kk: (0, 0))],
        out_specs=[pl.BlockSpec((tm, n), lambda i, kk: (i, 0)),
                   pl.BlockSpec((tm, n), lambda i, kk: (i, 0))],
        out_shape=[jax.ShapeDtypeStruct((m, n), f32), jax.ShapeDtypeStruct((m, n), bf16)],
        scratch_shapes=[pltpu.VMEM((tm, n), f32)],
        compiler_params=_params(("parallel", "arbitrary")),
        name="proj_ln",
    )(x, w, res, g, b)


def _merge_kernel(a_ref, b_ref, c_ref, m_ref, g0_ref, g1_ref, g2_ref, g3_ref, gb_ref, w_ref, o_ref):
    acc = None
    for n, (br, gp) in enumerate(((a_ref, g0_ref), (b_ref, g1_ref), (c_ref, g2_ref), (m_ref, g3_ref))):
        y = _dot(br[...], w_ref[n])
        gate = _sigmoid(gp[...] + gb_ref[n:n + 1, :])
        acc = gate * y if acc is None else acc + gate * y
    o_ref[...] = acc.astype(bf16)


def gated_merge(branches, u, gate_b, w_branch, tm, tn):
    m = u.shape[0]
    gp0 = C_GPRE // tn
    per = D_MODEL // tn
    br_spec = pl.BlockSpec((tm, BRANCH_WIDTH), lambda i, j: (i, 0))
    gp_specs = [pl.BlockSpec((tm, tn), functools.partial(lambda i, j, n: (i, gp0 + n * per + j), n=n))
                for n in range(N_BRANCH)]
    return pl.pallas_call(
        _merge_kernel,
        grid=(m // tm, D_MODEL // tn),
        in_specs=[br_spec] * 4 + gp_specs + [
            pl.BlockSpec((N_BRANCH, tn), lambda i, j: (0, j)),
            pl.BlockSpec((N_BRANCH, BRANCH_WIDTH, tn), lambda i, j: (0, 0, j))],
        out_specs=pl.BlockSpec((tm, tn), lambda i, j: (i, j)),
        out_shape=jax.ShapeDtypeStruct((m, D_MODEL), bf16),
        compiler_params=_params(("parallel", "parallel")),
        name="gated_merge",
    )(*branches, u, u, u, u, gate_b, w_branch)


def _ffn_up_kernel(x_ref, wg_ref, wu_ref, o_ref):
    x = x_ref[...]
    g = _dot(x, wg_ref[...])
    up = _dot(x, wu_ref[...])
    o_ref[...] = (g * _sigmoid(g) * up).astype(bf16)


def ffn_up(x, w_gu, tm, tn):
    m, k = x.shape
    nj = D_FF // tn
    return pl.pallas_call(
        _ffn_up_kernel,
        grid=(m // tm, nj),
        in_specs=[pl.BlockSpec((tm, k), lambda i, j: (i, 0)),
                  pl.BlockSpec((k, tn), lambda i, j: (0, j)),
                  pl.BlockSpec((k, tn), lambda i, j: (0, nj + j))],
        out_specs=pl.BlockSpec((tm, tn), lambda i, j: (i, j)),
        out_shape=jax.ShapeDtypeStruct((m, D_FF), bf16),
        compiler_params=_params(("parallel", "parallel")),
        name="ffn_up",
    )(x, w_gu, w_gu)


def _gla_kernel(q_ref, k_ref, v_ref, r_ref, a_ref, aup_ref, ab_ref, ng_ref, nb_ref, s0_ref,
                o_ref, s_ref, *, chunk, l_real, l_pad):
    c = pl.program_id(1)

    @pl.when(c == 0)
    def _():
        s_ref[...] = s0_ref[...]

    la = _dot(a_ref[...].astype(bf16), aup_ref[...]) + ab_ref[...]
    la = (jnp.minimum(la, 0.0) - jnp.log(1.0 + jnp.exp(-jnp.abs(la)))) * (1.0 / GLA_TAU)
    row = lax.broadcasted_iota(jnp.int32, (chunk, GLA_QK), 0)
    kin = k_ref[...]
    if l_real < l_pad:
        real = (c * chunk + row) < l_real
        la = jnp.where(real, la, 0.0)
        kin = jnp.where(real, kin, 0.0)
    b = la
    d = 1
    while d < chunk:
        b = b + jnp.where(row >= d, pltpu.roll(b, d, 0), 0.0)
        d *= 2
    b_last = b[chunk - 1:chunk, :]
    q_dec = q_ref[...] * (GLA_DK ** -0.5) * jnp.exp(b)
    k_inv = kin * jnp.exp(-b)
    k_dec = kin * jnp.exp(b_last - b)
    tt = lax.broadcasted_iota(jnp.int32, (chunk, chunk), 0)
    ss = lax.broadcasted_iota(jnp.int32, (chunk, chunk), 1)
    causal = tt >= ss
    for h in range(GLA_HEADS):
        ks = slice(h * GLA_DK, (h + 1) * GLA_DK)
        vs = slice(h * GLA_DV, (h + 1) * GLA_DV)
        qh = q_dec[:, ks].astype(bf16)
        att = jnp.where(causal, _dot_nt(qh, k_inv[:, ks].astype(bf16)), 0.0)
        vh = v_ref[:, vs].astype(bf16)
        s_old = s_ref[0, h]
        o = _dot(att.astype(bf16), vh) + _dot(qh, s_old.astype(bf16))
        decay_col = jnp.exp(jnp.sum(la[:, ks].T, axis=1, keepdims=True))
        s_ref[0, h] = decay_col * s_old + _dot(k_dec[:, ks].T.astype(bf16), vh)
        mu = jnp.mean(o, axis=-1, keepdims=True)
        dd = o - mu
        var = jnp.mean(dd * dd, axis=-1, keepdims=True)
        nrm = dd * lax.rsqrt(var + 1e-5) * ng_ref[:, vs] + nb_ref[:, vs]
        gr = r_ref[:, vs]
        o_ref[:, vs] = (nrm * (gr * _sigmoid(gr))).astype(bf16)


def gla(u, a_up, a_b, norm_g, norm_b, s0, batch, l_pad, l_real, chunk):
    m = u.shape[0]
    nc = l_pad // chunk
    row = lambda b, c: b * nc + c
    return pl.pallas_call(
        functools.partial(_gla_kernel, chunk=chunk, l_real=l_real, l_pad=l_pad),
        grid=(batch, nc),
        in_specs=[pl.BlockSpec((chunk, GLA_QK), lambda b, c: (row(b, c), C_GQ // GLA_QK)),
                  pl.BlockSpec((chunk, GLA_QK), lambda b, c: (row(b, c), C_GK // GLA_QK)),
                  pl.BlockSpec((chunk, GLA_V), lambda b, c: (row(b, c), C_GV // GLA_V)),
                  pl.BlockSpec((chunk, GLA_V), lambda b, c: (row(b, c), C_GR // GLA_V)),
                  pl.BlockSpec((chunk, LANES), lambda b, c: (row(b, c), C_GA // LANES)),
                  pl.BlockSpec((LANES, GLA_QK), lambda b, c: (0, 0)),
                  pl.BlockSpec((1, GLA_QK), lambda b, c: (0, 0)),
                  pl.BlockSpec((1, GLA_V), lambda b, c: (0, 0)),
                  pl.BlockSpec((1, GLA_V), lambda b, c: (0, 0)),
                  pl.BlockSpec((1, GLA_HEADS, GLA_DK, GLA_DV), lambda b, c: (b, 0, 0, 0))],
        out_specs=[pl.BlockSpec((chunk, GLA_V), lambda b, c: (row(b, c), 0)),
                   pl.BlockSpec((1, GLA_HEADS, GLA_DK, GLA_DV), lambda b, c: (b, 0, 0, 0))],
        out_shape=[jax.ShapeDtypeStruct((m, GLA_V), bf16),
                   jax.ShapeDtypeStruct((batch, GLA_HEADS, GLA_DK, GLA_DV), f32)],
        compiler_params=_params(("parallel", "arbitrary")),
        name="gla",
    )(u, u, u, u, u, a_up, a_b, norm_g, norm_b, s0)


def _alibi_slope(h):
    return 2.0 ** (-8.0 * (h + 1) / SWA_Q_HEADS)


def _swa_head(qh, keys, vals, dists, valids, sink):
    slope, scores = qh[1], []
    for kb, dist, valid in zip(keys, dists, valids):
        s = _dot_nt(qh[0], kb) * (SWA_HEAD_DIM ** -0.5)
        scores.append(jnp.where(valid, s - slope * dist, NEG))
    m = sink
    for s in scores:
        m = jnp.maximum(m, jnp.max(s, axis=-1, keepdims=True))
    ps = [jnp.exp(s - m) for s in scores]
    den = jnp.exp(sink - m)
    for p in ps:
        den = den + jnp.sum(p, axis=-1, keepdims=True)
    o = None
    for p, vb in zip(ps, vals):
        t = _dot((p / den).astype(bf16), vb)
        o = t if o is None else o + t
    return o


def _swa_prompt_kernel(sink_ref, q_ref, kc_ref, kp_ref, vc_ref, vp_ref, o_ref):
    i = pl.program_id(1)
    t = lax.broadcasted_iota(jnp.int32, (WINDOW, WINDOW), 0)
    s = lax.broadcasted_iota(jnp.int32, (WINDOW, WINDOW), 1)
    d_cur = (t - s).astype(f32)
    d_prev = d_cur + float(WINDOW)
    v_cur = t >= s
    lim = jnp.where(i > 0, 0, 2 * WINDOW)
    v_prev = (s - t) >= lim
    kc, kp = kc_ref[...].astype(bf16), kp_ref[...].astype(bf16)
    vc, vp = vc_ref[...].astype(bf16), vp_ref[...].astype(bf16)
    q_all = q_ref[...]
    outs = []
    for h in range(SWA_Q_HEADS):
        kv = h // SWA_GROUP
        ds = slice(kv * SWA_HEAD_DIM, (kv + 1) * SWA_HEAD_DIM)
        qh = q_all[:, h * SWA_HEAD_DIM:(h + 1) * SWA_HEAD_DIM].astype(bf16)
        outs.append(_swa_head((qh, _alibi_slope(h)), (kp[:, ds], kc[:, ds]), (vp[:, ds], vc[:, ds]),
                              (d_prev, d_cur), (v_prev, v_cur), sink_ref[h]))
    o_ref[...] = jnp.concatenate(outs, axis=-1).astype(bf16)


def swa_prompt(u, sinks, batch, seq):
    m = u.shape[0]
    nb = seq // WINDOW
    cur = lambda b, i: b * nb + i
    prev = lambda b, i: b * nb + jnp.maximum(i - 1, 0)
    kcol, vcol = C_SK // LANES, C_SV // LANES
    return pl.pallas_call(
        _swa_prompt_kernel,
        grid=(batch, nb),
        in_specs=[pl.BlockSpec(memory_space=pltpu.SMEM),
                  pl.BlockSpec((WINDOW, BRANCH_WIDTH), lambda b, i: (cur(b, i), C_SQ // BRANCH_WIDTH)),
                  pl.BlockSpec((WINDOW, LANES), lambda b, i: (cur(b, i), kcol)),
                  pl.BlockSpec((WINDOW, LANES), lambda b, i: (prev(b, i), kcol)),
                  pl.BlockSpec((WINDOW, LANES), lambda b, i: (cur(b, i), vcol)),
                  pl.BlockSpec((WINDOW, LANES), lambda b, i: (prev(b, i), vcol))],
        out_specs=pl.BlockSpec((WINDOW, BRANCH_WIDTH), lambda b, i: (cur(b, i), 0)),
        out_shape=jax.ShapeDtypeStruct((m, BRANCH_WIDTH), bf16),
        compiler_params=_params(("parallel", "parallel")),
        name="swa_prompt",
    )(sinks, u, u, u, u, u)


def _swa_sample_kernel(sink_ref, q_ref, kn_ref, vn_ref, kb_ref, vb_ref, o_ref, *, l_pad):
    t = lax.broadcasted_iota(jnp.int32, (l_pad, WINDOW), 0)
    j = lax.broadcasted_iota(jnp.int32, (l_pad, WINDOW), 1)
    d_buf = (WINDOW + t - j).astype(f32)
    v_buf = j >= t
    tn = lax.broadcasted_iota(jnp.int32, (l_pad, l_pad), 0)
    sn = lax.broadcasted_iota(jnp.int32, (l_pad, l_pad), 1)
    d_new = (tn - sn).astype(f32)
    v_new = tn >= sn
    kn, vn = kn_ref[...].astype(bf16), vn_ref[...].astype(bf16)
    kb, vb = kb_ref[...].astype(bf16), vb_ref[...].astype(bf16)
    q_all = q_ref[...]
    outs = []
    for h in range(SWA_Q_HEADS):
        kv = h // SWA_GROUP
        ds = slice(kv * SWA_HEAD_DIM, (kv + 1) * SWA_HEAD_DIM)
        qh = q_all[:, h * SWA_HEAD_DIM:(h + 1) * SWA_HEAD_DIM].astype(bf16)
        outs.append(_swa_head((qh, _alibi_slope(h)), (kb[:, ds], kn[:, ds]), (vb[:, ds], vn[:, ds]),
                              (d_buf, d_new), (v_buf, v_new), sink_ref[h]))
    o_ref[...] = jnp.concatenate(outs, axis=-1).astype(bf16)


def swa_sample(u, kbuf, vbuf, sinks, batch, l_pad):
    m = u.shape[0]
    kcol, vcol = C_SK // LANES, C_SV // LANES
    return pl.pallas_call(
        functools.partial(_swa_sample_kernel, l_pad=l_pad),
        grid=(batch,),
        in_specs=[pl.BlockSpec(memory_space=pltpu.SMEM),
                  pl.BlockSpec((l_pad, BRANCH_WIDTH), lambda b: (b, C_SQ // BRANCH_WIDTH)),
                  pl.BlockSpec((l_pad, LANES), lambda b: (b, kcol)),
                  pl.BlockSpec((l_pad, LANES), lambda b: (b, vcol)),
                  pl.BlockSpec((WINDOW, LANES), lambda b: (b, 0)),
                  pl.BlockSpec((WINDOW, LANES), lambda b: (b, 0))],
        out_specs=pl.BlockSpec((l_pad, BRANCH_WIDTH), lambda b: (b, 0)),
        out_shape=jax.ShapeDtypeStruct((m, BRANCH_WIDTH), bf16),
        compiler_params=_params(("parallel",)),
        name="swa_sample",
    )(sinks, u, u, u, kbuf, vbuf)


def _mem_kernel(q_ref, k_ref, v_ref, o_ref):
    for h in range(MEM_HEADS):
        hs = slice(h * MEM_HEAD_DIM, (h + 1) * MEM_HEAD_DIM)
        s = _dot_nt(q_ref[:, hs].astype(bf16), k_ref[:, hs].astype(bf16)) * (MEM_HEAD_DIM ** -0.5)
        p = jnp.exp(s - jnp.max(s, axis=-1, keepdims=True))
        p = p / jnp.sum(p, axis=-1, keepdims=True)
        o_ref[:, hs] = _dot(p.astype(bf16), v_ref[:, hs].astype(bf16)).astype(bf16)


def mem_attention(u, karr, kcol, varr, vcol, batch, l_pad, tl):
    m = u.shape[0]
    nl = l_pad // tl
    width = MEM_HEADS * MEM_HEAD_DIM
    return pl.pallas_call(
        _mem_kernel,
        grid=(batch, nl),
        in_specs=[pl.BlockSpec((tl, width), lambda b, i: (b * nl + i, C_MQ // width)),
                  pl.BlockSpec((MEM_TOKENS, width), lambda b, i: (b, kcol)),
                  pl.BlockSpec((MEM_TOKENS, width), lambda b, i: (b, vcol))],
        out_specs=pl.BlockSpec((tl, width), lambda b, i: (b * nl + i, 0)),
        out_shape=jax.ShapeDtypeStruct((m, width), bf16),
        compiler_params=_params(("parallel", "parallel")),
        name="mem_attention",
    )(u, karr, varr)


def _seg64_sum(x, ones_ref):
    hi = x.astype(bf16)
    lo = (x - hi.astype(f32)).astype(bf16)
    cols = []
    for j in range(x.shape[1] // LANES):
        js = slice(j * LANES, (j + 1) * LANES)
        cols.append(_dot(hi[:, js], ones_ref[...]) + _dot(lo[:, js], ones_ref[...]))
    return jnp.concatenate(cols, axis=-1)


def _rwkv_kernel(r_ref, k_ref, v_ref, lo_ref, pr_ref, pk_ref, pv_ref, plo_ref,
                 mur_ref, muk_ref, muv_ref, mulo_ref, w0_ref, a0_ref, wl_ref, kk_ref, ka_ref,
                 rk_ref, lg_ref, lb_ref, ones_ref, s0_ref,
                 o_ref, s_ref,
                 cr, ck, cv, clo, w3, kk3, kka3, k3, r3, v3, y3, bonus_s, g_s, *, tb, steps):
    i = pl.program_id(1)

    @pl.when(i == 0)
    def _():
        s_ref[...] = s0_ref[...]
        cr[...] = pr_ref[0]
        ck[...] = pk_ref[0]
        cv[...] = pv_ref[0]
        clo[...] = plo_ref[0]

    def token_shift(x_ref, carry, mu_ref):
        x = x_ref[...]
        row = lax.broadcasted_iota(jnp.int32, x.shape, 0)
        shifted = jnp.where(row == 0, carry[...], pltpu.roll(x, 1, 0))
        carry[...] = x[tb - 1:tb, :]
        return x + (shifted - x) * mu_ref[...]

    r = token_shift(r_ref, cr, mur_ref)
    k0 = token_shift(k_ref, ck, muk_ref)
    v = token_shift(v_ref, cv, muv_ref)
    lo = token_shift(lo_ref, clo, mulo_ref)

    col = lax.broadcasted_iota(jnp.int32, lo.shape, 1)
    act = jnp.where(col < 64, jnp.tanh(lo), jnp.where(col < 128, lo, _sigmoid(lo)))
    proj = _dot(act.astype(bf16), wl_ref[...])
    log_w = -_softplus(-(w0_ref[...] + proj[:, :BRANCH_WIDTH])) - 0.5
    decay = jnp.exp(-jnp.exp(log_w))
    a = _sigmoid(a0_ref[...] + proj[:, BRANCH_WIDTH:2 * BRANCH_WIDTH])
    g_s[...] = proj[:, 2 * BRANCH_WIDTH:]
    kk = k0 * kk_ref[...]
    kk = kk / jnp.maximum(jnp.sqrt(_seg64_sum(kk * kk, ones_ref)), 1e-12)
    k = k0 * (1.0 + (a - 1.0) * ka_ref[...])
    kka = kk * a
    bonus_s[...] = _seg64_sum(r * k * rk_ref[...], ones_ref) * v
    for h in range(RWKV_HEADS):
        hs = slice(h * RWKV_HEAD, (h + 1) * RWKV_HEAD)
        w3[h] = decay[:, hs]
        kk3[h] = kk[:, hs]
        kka3[h] = kka[:, hs]
        k3[h] = k[:, hs]
        r3[h] = r[:, hs]
        v3[h] = v[:, hs]

    eye = (lax.broadcasted_iota(jnp.int32, (RWKV_HEAD, RWKV_HEAD), 0)
           == lax.broadcasted_iota(jnp.int32, (RWKV_HEAD, RWKV_HEAD), 1)).astype(f32)

    def step(t, carry):
        ts = pl.ds(t, 1)
        for h in range(RWKV_HEADS):
            s_old = s_ref[0, h]
            sa = jnp.sum(s_old * kk3[h, ts, :], axis=1, keepdims=True)
            v_col = jnp.sum(eye * v3[h, ts, :], axis=1, keepdims=True)
            s_new = s_old * w3[h, ts, :] - sa * kka3[h, ts, :] + v_col * k3[h, ts, :]
            s_ref[0, h] = s_new
            y_col = jnp.sum(s_new * r3[h, ts, :], axis=1, keepdims=True)
            y3[h, ts, :] = jnp.sum(eye * y_col, axis=0, keepdims=True)
        return carry

    lax.fori_loop(0, steps, step, 0)

    y = jnp.concatenate([y3[h] for h in range(RWKV_HEADS)], axis=-1)
    mu = _seg64_sum(y, ones_ref) * (1.0 / RWKV_HEAD)
    d = y - mu
    var = _seg64_sum(d * d, ones_ref) * (1.0 / RWKV_HEAD)
    yn = d * lax.rsqrt(var + 64e-5) * lg_ref[...] + lb_ref[...]
    o_ref[...] = ((yn + bonus_s[...]) * g_s[...]).astype(bf16)


def rwkv(u, prev, p, s0, ones2, batch, l_pad, l_real, tb):
    m = u.shape[0]
    nblk = l_pad // tb
    steps = tb if l_real == l_pad else l_real
    row = lambda b, i: b * nblk + i
    full = lambda w: pl.BlockSpec((1, w), lambda b, i: (0, 0))
    prev_spec = lambda w: pl.BlockSpec((1, 1, w), lambda b, i: (b, 0, 0))
    bw = BRANCH_WIDTH
    state_spec = pl.BlockSpec((1, RWKV_HEADS, RWKV_HEAD, RWKV_HEAD), lambda b, i: (b, 0, 0, 0))
    head_buf = pltpu.VMEM((RWKV_HEADS, tb, RWKV_HEAD), f32)
    return pl.pallas_call(
        functools.partial(_rwkv_kernel, tb=tb, steps=steps),
        grid=(batch, nblk),
        in_specs=[pl.BlockSpec((tb, bw), lambda b, i: (row(b, i), C_R // bw)),
                  pl.BlockSpec((tb, bw), lambda b, i: (row(b, i), C_K // bw)),
                  pl.BlockSpec((tb, bw), lambda b, i: (row(b, i), C_V // bw)),
                  pl.BlockSpec((tb, RWKV_LORA), lambda b, i: (row(b, i), C_LORA // RWKV_LORA)),
                  prev_spec(bw), prev_spec(bw), prev_spec(bw), prev_spec(RWKV_LORA),
                  full(bw), full(bw), full(bw), full(RWKV_LORA),
                  full(bw), full(bw),
                  pl.BlockSpec((RWKV_LORA, 3 * bw), lambda b, i: (0, 0)),
                  full(bw), full(bw), full(bw), full(bw), full(bw),
                  pl.BlockSpec((LANES, LANES), lambda b, i: (0, 0)),
                  state_spec],
        out_specs=[pl.BlockSpec((tb, bw), lambda b, i: (row(b, i), 0)), state_spec],
        out_shape=[jax.ShapeDtypeStruct((m, bw), bf16),
                   jax.ShapeDtypeStruct((batch, RWKV_HEADS, RWKV_HEAD, RWKV_HEAD), f32)],
        scratch_shapes=[pltpu.VMEM((1, bw), f32), pltpu.VMEM((1, bw), f32), pltpu.VMEM((1, bw), f32),
                        pltpu.VMEM((1, RWKV_LORA), f32)] + [head_buf] * 7 + [
                        pltpu.VMEM((tb, bw), f32), pltpu.VMEM((tb, bw), f32)],
        compiler_params=_params(("parallel", "arbitrary")),
        name="rwkv7",
    )(u, u, u, u, *prev, p['mu_r'], p['mu_k'], p['mu_v'], p['mu_lo'], p['w0'], p['a0'], p['w_lora'],
      p['k_k'], p['k_a'], p['r_k'], p['ln_g'], p['ln_b'], ones2, s0)


def _pack_w_in(w):
    seg = lambda o, n: w[:, o:o + n]
    parts = [seg(_O_GV, 1024), seg(_O_GR, 1024), seg(_O_SQ, 1024), seg(_O_MQ, 1024),
             seg(_O_RU, 3072), seg(_O_GPRE, 8192), seg(_O_GQ, 512), seg(_O_GK, 512),
             seg(_O_RU + 3072, RWKV_LORA), seg(_O_SK, 128), seg(_O_SV, 128), seg(_O_GA, GLA_GATE_RANK),
             jnp.zeros((w.shape[0], N_PACK - C_GA - GLA_GATE_RANK), w.dtype)]
    return jnp.concatenate(parts, axis=1).astype(bf16)


def _layer_params(l, w):
    row = lambda x: x[l].reshape(1, -1)
    mu = w['rwkv_mu'][l]
    z = lambda r, c: jnp.zeros((r, c), f32)
    w_lora = jnp.concatenate([
        jnp.concatenate([w['rwkv_w2'][l], z(64, 1024), z(64, 1024)], axis=1),
        jnp.concatenate([z(64, 1024), w['rwkv_a2'][l], z(64, 1024)], axis=1),
        jnp.concatenate([z(128, 1024), z(128, 1024), w['rwkv_g2'][l]], axis=1)], axis=0).astype(bf16)
    a_up = jnp.concatenate([w['gla_a_up'][l], z(LANES - GLA_GATE_RANK, GLA_QK)], axis=0).astype(bf16)
    return {
        'w_in': _pack_w_in(w['w_in'][l]),
        'gate_b': w['gate_b'][l],
        'a_up': a_up, 'a_b': row(w['gla_a_b']),
        'gla_g': row(w['gla_norm_g']), 'gla_b': row(w['gla_norm_b']),
        'sinks': w['swa_sinks'][l],
        'rwkv': {'mu_r': mu[:1024].reshape(1, -1), 'mu_k': mu[1024:2048].reshape(1, -1),
                 'mu_v': mu[2048:3072].reshape(1, -1), 'mu_lo': mu[3072:].reshape(1, -1),
                 'w0': row(w['rwkv_w0']), 'a0': row(w['rwkv_a0']), 'w_lora': w_lora,
                 'k_k': row(w['rwkv_k_k']), 'k_a': row(w['rwkv_k_a']), 'r_k': row(w['rwkv_r_k']),
                 'ln_g': row(w['rwkv_ln_g']), 'ln_b': row(w['rwkv_ln_b'])},
        'w_mem_kv': w['w_mem_kv'][l].astype(bf16),
        'w_branch': w['w_branch'][l].astype(bf16),
        'w_out': w['w_out'][l].astype(bf16),
        'ln1_g': row(w['ln1_g']), 'ln1_b': row(w['ln1_b']),
        'w_gu': w['w_gu'][l].astype(bf16),
        'w_down': w['w_down'][l].astype(bf16),
        'ln2_g': row(w['ln2_g']), 'ln2_b': row(w['ln2_b']),
    }


def _tiles(m):
    if m >= 1024:
        return {'proj': 1024, 'merge': 512, 'ffn': 1024, 'ln': 256}
    return {'proj': m, 'merge': m, 'ffn': m, 'ln': m}


def _trunk_layer(h_f, h_b, p, ones2, mem_k, mem_v, gla_s0, rwkv_s0, rwkv_prev, swa_buf,
                 batch, l_pad, l_real):
    m = h_f.shape[0]
    tl = _tiles(m)
    u = matmul(h_b, p['w_in'], tl['proj'], 1024)
    o_a, gla_s = gla(u, p['a_up'], p['a_b'], p['gla_g'], p['gla_b'], gla_s0, batch, l_pad, l_real,
                     min(GLA_CHUNK, l_pad))
    if swa_buf is None:
        o_b = swa_prompt(u, p['sinks'], batch, l_pad)
    else:
        o_b = swa_sample(u, swa_buf[0], swa_buf[1], p['sinks'], batch, l_pad)
    o_c, rwkv_s = rwkv(u, rwkv_prev, p['rwkv'], rwkv_s0, ones2, batch, l_pad, l_real, min(64, l_pad))
    o_m = mem_attention(u, mem_k[0], mem_k[1], mem_v[0], mem_v[1], batch, l_pad, min(512, l_pad))
    merged = gated_merge((o_a, o_b, o_c, o_m), u, p['gate_b'], p['w_branch'], tl['merge'], 512)
    x_f, x_b = matmul_residual_ln(merged, p['w_out'], h_f, p['ln1_g'], p['ln1_b'], tl['ln'], D_MODEL)
    act = ffn_up(x_b, p['w_gu'], tl['ffn'], 512)
    y_f, y_b = matmul_residual_ln(act, p['w_down'], x_f, p['ln2_g'], p['ln2_b'], tl['ln'], D_FF // 4)
    return y_f, y_b, u, gla_s, rwkv_s


def _split_ru(x):
    return (x[..., :1024], x[..., 1024:2048], x[..., 2048:3072], x[..., 3072:])


def kernel(x_prompt, x_sample, mem_prompt, cache_swa_k, cache_swa_v, cache_mem_k, cache_mem_v, state_gla, state_rwkv, state_rwkv_shift, w_in, gate_b, gla_a_up, gla_a_b, gla_norm_g, gla_norm_b, swa_sinks, rwkv_mu, rwkv_w0, rwkv_w2, rwkv_a0, rwkv_a2, rwkv_g2, rwkv_k_k, rwkv_k_a, rwkv_r_k, rwkv_ln_g, rwkv_ln_b, w_mem_kv, w_branch, w_out, ln1_g, ln1_b, w_gu, w_down, ln2_g, ln2_b):
    weights = {'w_in': w_in, 'gate_b': gate_b, 'gla_a_up': gla_a_up, 'gla_a_b': gla_a_b,
               'gla_norm_g': gla_norm_g, 'gla_norm_b': gla_norm_b, 'swa_sinks': swa_sinks,
               'rwkv_mu': rwkv_mu, 'rwkv_w0': rwkv_w0, 'rwkv_w2': rwkv_w2, 'rwkv_a0': rwkv_a0,
               'rwkv_a2': rwkv_a2, 'rwkv_g2': rwkv_g2, 'rwkv_k_k': rwkv_k_k, 'rwkv_k_a': rwkv_k_a,
               'rwkv_r_k': rwkv_r_k, 'rwkv_ln_g': rwkv_ln_g, 'rwkv_ln_b': rwkv_ln_b,
               'w_mem_kv': w_mem_kv, 'w_branch': w_branch, 'w_out': w_out, 'ln1_g': ln1_g,
               'ln1_b': ln1_b, 'w_gu': w_gu, 'w_down': w_down, 'ln2_g': ln2_g, 'ln2_b': ln2_b}
    bp, lp, _ = x_prompt.shape
    bs, ls, _ = x_sample.shape
    ls_pad = -(-ls // SUBLANES) * SUBLANES
    mp, ms = bp * lp, bs * ls_pad

    hp_f = x_prompt.reshape(mp, D_MODEL)
    hs_f = jnp.pad(x_sample, ((0, 0), (0, ls_pad - ls), (0, 0))).reshape(ms, D_MODEL)
    hp_b, hs_b = hp_f.astype(bf16), hs_f.astype(bf16)
    mem_b = mem_prompt.reshape(bp * MEM_TOKENS, D_MODEL).astype(bf16)
    half = LANES // 2
    blk = jnp.ones((half, half), f32)
    zero = jnp.zeros((half, half), f32)
    ones2 = jnp.concatenate([jnp.concatenate([blk, zero], 1), jnp.concatenate([zero, blk], 1)], 0).astype(bf16)

    gla0_p = jnp.zeros((bp, GLA_HEADS, GLA_DK, GLA_DV), f32)
    rwkv0_p = jnp.zeros((bp, RWKV_HEADS, RWKV_HEAD, RWKV_HEAD), f32)
    prev0_p = _split_ru(jnp.zeros((bp, 1, RWKV_COLS), f32))

    outs = {k: [] for k in ('p_swk', 'p_swv', 'p_mk', 'p_mv', 'p_gla', 'p_rw', 'p_rs',
                            's_swk', 's_swv', 's_gla', 's_rw', 's_rs')}
    kvw = SWA_KV_HEADS * SWA_HEAD_DIM
    for l in range(DEPTH):
        p = _layer_params(l, weights)
        kv = matmul(mem_b, p['w_mem_kv'], bp * MEM_TOKENS, 512)
        hp_f, hp_b, u, gs, rs = _trunk_layer(hp_f, hp_b, p, ones2, (kv, 0), (kv, 1), gla0_p, rwkv0_p,
                                             prev0_p, None, bp, lp, lp)
        u3 = u.reshape(bp, lp, N_PACK)
        outs['p_swk'].append(u3[:, lp - WINDOW:, C_SK:C_SK + kvw].reshape(bp, WINDOW, SWA_KV_HEADS, SWA_HEAD_DIM))
        outs['p_swv'].append(u3[:, lp - WINDOW:, C_SV:C_SV + kvw].reshape(bp, WINDOW, SWA_KV_HEADS, SWA_HEAD_DIM))
        outs['p_mk'].append(kv[:, :1024].reshape(bp, MEM_TOKENS, MEM_HEADS, MEM_HEAD_DIM))
        outs['p_mv'].append(kv[:, 1024:].reshape(bp, MEM_TOKENS, MEM_HEADS, MEM_HEAD_DIM))
        outs['p_gla'].append(gs)
        outs['p_rw'].append(rs)
        outs['p_rs'].append(jnp.concatenate([u3[:, lp - 1:, C_R:C_R + 3072],
                                             u3[:, lp - 1:, C_LORA:C_LORA + RWKV_LORA]], axis=-1))
        kbuf = cache_swa_k[l].reshape(bs * WINDOW, kvw)
        vbuf = cache_swa_v[l].reshape(bs * WINDOW, kvw)
        mk = cache_mem_k[l].reshape(bs * MEM_TOKENS, MEM_HEADS * MEM_HEAD_DIM)
        mv = cache_mem_v[l].reshape(bs * MEM_TOKENS, MEM_HEADS * MEM_HEAD_DIM)
        hs_f, hs_b, u, gs, rs = _trunk_layer(hs_f, hs_b, p, ones2, (mk, 0), (mv, 0), state_gla[l],
                                             state_rwkv[l], _split_ru(state_rwkv_shift[l]), (kbuf, vbuf),
                                             bs, ls_pad, ls)
        u3 = u.reshape(bs, ls_pad, N_PACK)
        k_new = u3[:, :ls, C_SK:C_SK + kvw].reshape(bs, ls, SWA_KV_HEADS, SWA_HEAD_DIM)
        v_new = u3[:, :ls, C_SV:C_SV + kvw].reshape(bs, ls, SWA_KV_HEADS, SWA_HEAD_DIM)
        outs['s_swk'].append(jnp.concatenate([cache_swa_k[l][:, ls:], k_new], axis=1))
        outs['s_swv'].append(jnp.concatenate([cache_swa_v[l][:, ls:], v_new], axis=1))
        outs['s_gla'].append(gs)
        outs['s_rw'].append(rs)
        outs['s_rs'].append(jnp.concatenate([u3[:, ls - 1:ls, C_R:C_R + 3072],
                                             u3[:, ls - 1:ls, C_LORA:C_LORA + RWKV_LORA]], axis=-1))

    st = {k: jnp.stack(v) for k, v in outs.items()}
    y_prompt = hp_f.reshape(bp, lp, D_MODEL)
    y_sample = hs_f.reshape(bs, ls_pad, D_MODEL)[:, :ls]
    return (y_prompt, y_sample,
            st['p_swk'], st['p_swv'], st['p_mk'], st['p_mv'], st['p_gla'], st['p_rw'], st['p_rs'],
            st['s_swk'], st['s_swv'], st['s_gla'], st['s_rw'], st['s_rs'])
```

```python
import functools

import jax
import jax.numpy as jnp
from jax import lax
from jax.experimental import pallas as pl
from jax.experimental.pallas import tpu as pltpu

f32 = jnp.float32
bf16 = jnp.bfloat16

D_MODEL = 2048
DEPTH = 4
BRANCH_WIDTH = 1024
N_BRANCH = 4
GLA_HEADS = 4
GLA_QK = 512
GLA_V = 1024
GLA_DK = 128
GLA_DV = 256
GLA_GATE_RANK = 16
GLA_TAU = 16.0
GLA_CHUNK = 64
SWA_HEAD_DIM = 64
SWA_Q_HEADS = 16
SWA_KV_HEADS = 2
SWA_GROUP = SWA_Q_HEADS // SWA_KV_HEADS
WINDOW = 128
RWKV_HEAD = 64
RWKV_HEADS = 16
RWKV_LORA = 256
RWKV_COLS = 3 * BRANCH_WIDTH + RWKV_LORA
MEM_TOKENS = 256
MEM_HEADS = 4
MEM_HEAD_DIM = 256
D_FF = 5632
DEEPNORM_ALPHA = (2 * DEPTH) ** 0.25
NEG = -1e30

_O_GQ, _O_GK, _O_GV, _O_GR, _O_GA = 0, 512, 1024, 2048, 3072
_O_SQ, _O_SK, _O_SV = 3088, 4112, 4240
_O_RU = 4368
_O_MQ = 7696
_O_GPRE = 8720
_N_IN = 16912

C_GV, C_GR, C_SQ, C_MQ, C_R, C_K, C_V = 0, 1024, 2048, 3072, 4096, 5120, 6144
C_GPRE = 7168
C_GQ, C_GK = 15360, 15872
C_LORA = 16384
C_SK, C_SV, C_GA = 16640, 16768, 16896
N_PACK = 17408

LANES = 128
SUBLANES = 8
VMEM_BYTES_V7X = 64 * 1024 * 1024
VMEM_LIMIT = VMEM_BYTES_V7X - 8 * 1024 * 1024


def _params(sem):
    return pltpu.CompilerParams(dimension_semantics=sem, vmem_limit_bytes=VMEM_LIMIT)


def _softplus(z):
    return jnp.maximum(z, 0.0) + jnp.log(1.0 + jnp.exp(-jnp.abs(z)))


def _sigmoid(z):
    return 1.0 / (1.0 + jnp.exp(-z))


def _dot(a, b):
    return jnp.dot(a, b, preferred_element_type=f32)


def _dot_nt(a, b):
    return lax.dot_general(a, b, (((1,), (1,)), ((), ())), preferred_element_type=f32)


def _mm_kernel(x_ref, w_ref, o_ref):
    o_ref[...] = _dot(x_ref[...], w_ref[...]).astype(o_ref.dtype)


def matmul(x, w, tm, tn, name, out_dtype=f32):
    m, k = x.shape
    n = w.shape[1]
    return pl.pallas_call(
        _mm_kernel,
        grid=(m // tm, n // tn),
        in_specs=[pl.BlockSpec((tm, k), lambda i, j: (i, 0)),
                  pl.BlockSpec((k, tn), lambda i, j: (0, j))],
        out_specs=pl.BlockSpec((tm, tn), lambda i, j: (i, j)),
        out_shape=jax.ShapeDtypeStruct((m, n), out_dtype),
        compiler_params=_params(("parallel", "parallel")),
        name=name,
    )(x, w)


def _mm_ln_kernel(x_ref, w_ref, res_ref, g_ref, b_ref, of_ref, ob_ref, *, nk):
    k = pl.program_id(1)
    part = _dot(x_ref[...], w_ref[...])

    @pl.when(k == 0)
    def _():
        of_ref[...] = part

    @pl.when(k > 0)
    def _():
        of_ref[...] += part

    @pl.when(k == nk - 1)
    def _():
        z = DEEPNORM_ALPHA * res_ref[...] + of_ref[...]
        mu = jnp.mean(z, axis=-1, keepdims=True)
        d = z - mu
        var = jnp.mean(d * d, axis=-1, keepdims=True)
        y = d * lax.rsqrt(var + 1e-5) * g_ref[...] + b_ref[...]
        of_ref[...] = y
        ob_ref[...] = y.astype(bf16)


def matmul_residual_ln(x, w, res, g, b, tm, tk):
    m, k = x.shape
    n = w.shape[1]
    nk = k // tk
    return pl.pallas_call(
        functools.partial(_mm_ln_kernel, nk=nk),
        grid=(m // tm, nk),
        in_specs=[pl.BlockSpec((tm, tk), lambda i, kk: (i, kk)),
                  pl.BlockSpec((tk, n), lambda i, kk: (kk, 0)),
                  pl.BlockSpec((tm, n), lambda i, kk: (i, 0)),
                  pl.BlockSpec((1, n), lambda i, kk: (0, 0)),
                  pl.BlockSpec((1, n), lambda i, kk: (0, 0))],
        out_specs=[pl.BlockSpec((tm, n), lambda i, kk: (i, 0)),
                   pl.BlockSpec((tm, n), lambda i, kk: (i, 0))],
        out_shape=[jax.ShapeDtypeStruct((m, n), f32), jax.ShapeDtypeStruct((m, n), bf16)],
        compiler_params=_params(("parallel", "arbitrary")),
        name="proj_ln",
    )(x, w, res, g, b)


def _merge_kernel(a_ref, b_ref, c_ref, m_ref, g0_ref, g1_ref, g2_ref, g3_ref, gb_ref, w_ref, o_ref):
    acc = None
    for n, (br, gp) in enumerate(((a_ref, g0_ref), (b_ref, g1_ref), (c_ref, g2_ref), (m_ref, g3_ref))):
        y = _dot(br[...], w_ref[n])
        gate = _sigmoid(gp[...] + gb_ref[n:n + 1, :])
        acc = gate * y if acc is None else acc + gate * y
    o_ref[...] = acc.astype(bf16)


def gated_merge(branches, u, gate_b, w_branch, tm, tn):
    m = u.shape[0]
    gp0 = C_GPRE // tn
    per = D_MODEL // tn
    br_spec = pl.BlockSpec((tm, BRANCH_WIDTH), lambda i, j: (i, 0))
    gp_specs = [pl.BlockSpec((tm, tn), functools.partial(lambda i, j, n: (i, gp0 + n * per + j), n=n))
                for n in range(N_BRANCH)]
    return pl.pallas_call(
        _merge_kernel,
        grid=(m // tm, D_MODEL // tn),
        in_specs=[br_spec] * 4 + gp_specs + [
            pl.BlockSpec((N_BRANCH, tn), lambda i, j: (0, j)),
            pl.BlockSpec((N_BRANCH, BRANCH_WIDTH, tn), lambda i, j: (0, 0, j))],
        out_specs=pl.BlockSpec((tm, tn), lambda i, j: (i, j)),
        out_shape=jax.ShapeDtypeStruct((m, D_MODEL), bf16),
        compiler_params=_params(("parallel", "parallel")),
        name="gated_merge",
    )(*branches, u, u, u, u, gate_b, w_branch)


def _ffn_up_kernel(x_ref, wg_ref, wu_ref, o_ref):
    x = x_ref[...]
    g = _dot(x, wg_ref[...])
    up = _dot(x, wu_ref[...])
    o_ref[...] = (g * _sigmoid(g) * up).astype(bf16)


def ffn_up(x, w_gu, tm, tn):
    m, k = x.shape
    nj = D_FF // tn
    return pl.pallas_call(
        _ffn_up_kernel,
        grid=(m // tm, nj),
        in_specs=[pl.BlockSpec((tm, k), lambda i, j: (i, 0)),
                  pl.BlockSpec((k, tn), lambda i, j: (0, j)),
                  pl.BlockSpec((k, tn), lambda i, j: (0, nj + j))],
        out_specs=pl.BlockSpec((tm, tn), lambda i, j: (i, j)),
        out_shape=jax.ShapeDtypeStruct((m, D_FF), bf16),
        compiler_params=_params(("parallel", "parallel")),
        name="ffn_up",
    )(x, w_gu, w_gu)


def _gla_kernel(q_ref, k_ref, v_ref, r_ref, a_ref, aup_ref, ab_ref, ng_ref, nb_ref, s0_ref,
                o_ref, s_ref, *, chunk, l_real, l_pad):
    c = pl.program_id(1)

    @pl.when(c == 0)
    def _():
        s_ref[...] = s0_ref[...]

    la = _dot(a_ref[...].astype(bf16), aup_ref[...]) + ab_ref[...]
    la = (jnp.minimum(la, 0.0) - jnp.log(1.0 + jnp.exp(-jnp.abs(la)))) * (1.0 / GLA_TAU)
    row = lax.broadcasted_iota(jnp.int32, (chunk, GLA_QK), 0)
    kin = k_ref[...]
    if l_real < l_pad:
        real = (c * chunk + row) < l_real
        la = jnp.where(real, la, 0.0)
        kin = jnp.where(real, kin, 0.0)
    b = la
    d = 1
    while d < chunk:
        b = b + jnp.where(row >= d, pltpu.roll(b, d, 0), 0.0)
        d *= 2
    b_last = b[chunk - 1:chunk, :]
    q_dec = q_ref[...] * (GLA_DK ** -0.5) * jnp.exp(b)
    k_inv = kin * jnp.exp(-b)
    k_dec = kin * jnp.exp(b_last - b)
    tt = lax.broadcasted_iota(jnp.int32, (chunk, chunk), 0)
    ss = lax.broadcasted_iota(jnp.int32, (chunk, chunk), 1)
    causal = tt >= ss
    for h in range(GLA_HEADS):
        ks = slice(h * GLA_DK, (h + 1) * GLA_DK)
        vs = slice(h * GLA_DV, (h + 1) * GLA_DV)
        qh = q_dec[:, ks].astype(bf16)
        att = jnp.where(causal, _dot_nt(qh, k_inv[:, ks].astype(bf16)), 0.0)
        vh = v_ref[:, vs].astype(bf16)
        s_old = s_ref[0, h]
        o = _dot(att.astype(bf16), vh) + _dot(qh, s_old.astype(bf16))
        decay_col = jnp.exp(jnp.sum(la[:, ks].T, axis=1, keepdims=True))
        s_ref[0, h] = decay_col * s_old + _dot(k_dec[:, ks].T.astype(bf16), vh)
        mu = jnp.mean(o, axis=-1, keepdims=True)
        dd = o - mu
        var = jnp.mean(dd * dd, axis=-1, keepdims=True)
        nrm = dd * lax.rsqrt(var + 1e-5) * ng_ref[:, vs] + nb_ref[:, vs]
        gr = r_ref[:, vs]
        o_ref[:, vs] = (nrm * (gr * _sigmoid(gr))).astype(bf16)


def gla(u, a_up, a_b, norm_g, norm_b, s0, batch, l_pad, l_real, chunk):
    m = u.shape[0]
    nc = l_pad // chunk
    row = lambda b, c: b * nc + c
    return pl.pallas_call(
        functools.partial(_gla_kernel, chunk=chunk, l_real=l_real, l_pad=l_pad),
        grid=(batch, nc),
        in_specs=[pl.BlockSpec((chunk, GLA_QK), lambda b, c: (row(b, c), C_GQ // GLA_QK)),
                  pl.BlockSpec((chunk, GLA_QK), lambda b, c: (row(b, c), C_GK // GLA_QK)),
                  pl.BlockSpec((chunk, GLA_V), lambda b, c: (row(b, c), C_GV // GLA_V)),
                  pl.BlockSpec((chunk, GLA_V), lambda b, c: (row(b, c), C_GR // GLA_V)),
                  pl.BlockSpec((chunk, LANES), lambda b, c: (row(b, c), C_GA // LANES)),
                  pl.BlockSpec((LANES, GLA_QK), lambda b, c: (0, 0)),
                  pl.BlockSpec((1, GLA_QK), lambda b, c: (0, 0)),
                  pl.BlockSpec((1, GLA_V), lambda b, c: (0, 0)),
                  pl.BlockSpec((1, GLA_V), lambda b, c: (0, 0)),
                  pl.BlockSpec((1, GLA_HEADS, GLA_DK, GLA_DV), lambda b, c: (b, 0, 0, 0))],
        out_specs=[pl.BlockSpec((chunk, GLA_V), lambda b, c: (row(b, c), 0)),
                   pl.BlockSpec((1, GLA_HEADS, GLA_DK, GLA_DV), lambda b, c: (b, 0, 0, 0))],
        out_shape=[jax.ShapeDtypeStruct((m, GLA_V), bf16),
                   jax.ShapeDtypeStruct((batch, GLA_HEADS, GLA_DK, GLA_DV), f32)],
        compiler_params=_params(("parallel", "arbitrary")),
        name="gla",
    )(u, u, u, u, u, a_up, a_b, norm_g, norm_b, s0)


def _alibi_slopes():
    return 2.0 ** (-8.0 * jnp.arange(1, SWA_Q_HEADS + 1, dtype=f32) / SWA_Q_HEADS)


def _swa_bias(t_pos, key_pos, key_ok):
    dist = (t_pos[:, None] - key_pos[None, :]).astype(f32)
    valid = (dist >= 0) & (dist <= WINDOW) & key_ok[None, :]
    slopes = _alibi_slopes().reshape(SWA_KV_HEADS, SWA_GROUP, 1, 1)
    bias = jnp.where(valid[None, None], -slopes * dist[None, None], NEG)
    return bias.reshape(SWA_KV_HEADS, SWA_GROUP * t_pos.shape[0], key_pos.shape[0])


def _swa_group(q_all, kv, keys, vals, bias, sink_ref, rows):
    ds = slice(kv * SWA_HEAD_DIM, (kv + 1) * SWA_HEAD_DIM)
    heads = range(kv * SWA_GROUP, (kv + 1) * SWA_GROUP)
    q = jnp.concatenate([q_all[:, h * SWA_HEAD_DIM:(h + 1) * SWA_HEAD_DIM] for h in heads], axis=0)
    k2 = jnp.concatenate([kb[:, ds] for kb in keys], axis=0).astype(bf16)
    v2 = jnp.concatenate([vb[:, ds] for vb in vals], axis=0).astype(bf16)
    s = _dot_nt(q.astype(bf16), k2) * (SWA_HEAD_DIM ** -0.5) + bias
    sink = jnp.concatenate([jnp.full((rows, 1), sink_ref[h], f32) for h in heads], axis=0)
    m = jnp.maximum(jnp.max(s, axis=-1, keepdims=True), sink)
    p = jnp.exp(s - m)
    den = jnp.sum(p, axis=-1, keepdims=True) + jnp.exp(sink - m)
    o = _dot((p / den).astype(bf16), v2)
    return [o[g * rows:(g + 1) * rows, :] for g in range(SWA_GROUP)]


def _swa_prompt_kernel(sink_ref, bias_ref, q_ref, kc_ref, kp_ref, vc_ref, vp_ref, o_ref):
    q_all = q_ref[...]
    keys = (kp_ref[...], kc_ref[...])
    vals = (vp_ref[...], vc_ref[...])
    outs = []
    for kv in range(SWA_KV_HEADS):
        outs += _swa_group(q_all, kv, keys, vals, bias_ref[0, kv], sink_ref, WINDOW)
    o_ref[...] = jnp.concatenate(outs, axis=-1).astype(bf16)


def swa_prompt(u, sinks, batch, seq):
    m = u.shape[0]
    nb = seq // WINDOW
    cur = lambda b, i: b * nb + i
    prev = lambda b, i: b * nb + jnp.maximum(i - 1, 0)
    kcol, vcol = C_SK // LANES, C_SV // LANES
    t_pos = jnp.arange(WINDOW)
    key_pos = jnp.arange(2 * WINDOW) - WINDOW
    bias = jnp.stack([_swa_bias(t_pos, key_pos, key_pos >= 0), _swa_bias(t_pos, key_pos, key_pos >= -WINDOW)])
    return pl.pallas_call(
        _swa_prompt_kernel,
        grid=(batch, nb),
        in_specs=[pl.BlockSpec(memory_space=pltpu.SMEM),
                  pl.BlockSpec((1,) + bias.shape[1:], lambda b, i: (jnp.minimum(i, 1), 0, 0, 0)),
                  pl.BlockSpec((WINDOW, BRANCH_WIDTH), lambda b, i: (cur(b, i), C_SQ // BRANCH_WIDTH)),
                  pl.BlockSpec((WINDOW, LANES), lambda b, i: (cur(b, i), kcol)),
                  pl.BlockSpec((WINDOW, LANES), lambda b, i: (prev(b, i), kcol)),
                  pl.BlockSpec((WINDOW, LANES), lambda b, i: (cur(b, i), vcol)),
                  pl.BlockSpec((WINDOW, LANES), lambda b, i: (prev(b, i), vcol))],
        out_specs=pl.BlockSpec((WINDOW, BRANCH_WIDTH), lambda b, i: (cur(b, i), 0)),
        out_shape=jax.ShapeDtypeStruct((m, BRANCH_WIDTH), bf16),
        compiler_params=_params(("parallel", "parallel")),
        name="swa_prompt",
    )(sinks, bias, u, u, u, u, u)


def _swa_sample_kernel(sink_ref, bias_ref, q_ref, kn_ref, vn_ref, kb_ref, vb_ref, o_ref, *, l_pad, reqs):
    for r in range(reqs):
        rs = slice(r * l_pad, (r + 1) * l_pad)
        bs = slice(r * WINDOW, (r + 1) * WINDOW)
        q_all = q_ref[rs, :]
        keys = (kb_ref[bs, :], kn_ref[rs, :])
        vals = (vb_ref[bs, :], vn_ref[rs, :])
        outs = []
        for kv in range(SWA_KV_HEADS):
            outs += _swa_group(q_all, kv, keys, vals, bias_ref[kv], sink_ref, l_pad)
        o_ref[rs, :] = jnp.concatenate(outs, axis=-1).astype(bf16)


def swa_sample(u, kbuf, vbuf, sinks, batch, l_pad, l_real, reqs):
    m = u.shape[0]
    kcol, vcol = C_SK // LANES, C_SV // LANES
    key_pos = jnp.concatenate([jnp.arange(WINDOW) - WINDOW, jnp.arange(l_pad)])
    key_ok = jnp.concatenate([jnp.ones((WINDOW,), bool), jnp.arange(l_pad) < l_real])
    bias = _swa_bias(jnp.arange(l_pad), key_pos, key_ok)
    return pl.pallas_call(
        functools.partial(_swa_sample_kernel, l_pad=l_pad, reqs=reqs),
        grid=(batch // reqs,),
        in_specs=[pl.BlockSpec(memory_space=pltpu.SMEM),
                  pl.BlockSpec(bias.shape, lambda b: (0, 0, 0)),
                  pl.BlockSpec((reqs * l_pad, BRANCH_WIDTH), lambda b: (b, C_SQ // BRANCH_WIDTH)),
                  pl.BlockSpec((reqs * l_pad, LANES), lambda b: (b, kcol)),
                  pl.BlockSpec((reqs * l_pad, LANES), lambda b: (b, vcol)),
                  pl.BlockSpec((reqs * WINDOW, LANES), lambda b: (b, 0)),
                  pl.BlockSpec((reqs * WINDOW, LANES), lambda b: (b, 0))],
        out_specs=pl.BlockSpec((reqs * l_pad, BRANCH_WIDTH), lambda b: (b, 0)),
        out_shape=jax.ShapeDtypeStruct((m, BRANCH_WIDTH), bf16),
        compiler_params=_params(("parallel",)),
        name="swa_sample",
    )(sinks, bias, u, u, u, kbuf, vbuf)


def _mem_kernel(q_ref, k_ref, v_ref, o_ref):
    for h in range(MEM_HEADS):
        hs = slice(h * MEM_HEAD_DIM, (h + 1) * MEM_HEAD_DIM)
        s = _dot_nt(q_ref[:, hs].astype(bf16), k_ref[:, hs].astype(bf16)) * (MEM_HEAD_DIM ** -0.5)
        p = jnp.exp(s - jnp.max(s, axis=-1, keepdims=True))
        p = p / jnp.sum(p, axis=-1, keepdims=True)
        o_ref[:, hs] = _dot(p.astype(bf16), v_ref[:, hs].astype(bf16)).astype(bf16)


def mem_attention(u, karr, kcol, varr, vcol, batch, l_pad, tl):
    m = u.shape[0]
    nl = l_pad // tl
    width = MEM_HEADS * MEM_HEAD_DIM
    return pl.pallas_call(
        _mem_kernel,
        grid=(batch, nl),
        in_specs=[pl.BlockSpec((tl, width), lambda b, i: (b * nl + i, C_MQ // width)),
                  pl.BlockSpec((MEM_TOKENS, width), lambda b, i: (b, kcol)),
                  pl.BlockSpec((MEM_TOKENS, width), lambda b, i: (b, vcol))],
        out_specs=pl.BlockSpec((tl, width), lambda b, i: (b * nl + i, 0)),
        out_shape=jax.ShapeDtypeStruct((m, width), bf16),
        compiler_params=_params(("parallel", "parallel")),
        name="mem_attention",
    )(u, karr, varr)


def _seg64_sum(x, ones_ref):
    hi = x.astype(bf16)
    lo = (x - hi.astype(f32)).astype(bf16)
    cols = []
    for j in range(x.shape[1] // LANES):
        js = slice(j * LANES, (j + 1) * LANES)
        cols.append(_dot(hi[:, js], ones_ref[...]) + _dot(lo[:, js], ones_ref[...]))
    return jnp.concatenate(cols, axis=-1)


def _rwkv_features(in_refs, carries, w_refs, ones_ref, first, tb):
    r_ref, k_ref, v_ref, lo_ref, pr_ref, pk_ref, pv_ref, plo_ref = in_refs
    mur_ref, muk_ref, muv_ref, mulo_ref, w0_ref, a0_ref, wl_ref, kk_ref, ka_ref, rk_ref = w_refs
    cr, ck, cv, clo = carries

    @pl.when(first)
    def _():
        cr[...] = pr_ref[0]
        ck[...] = pk_ref[0]
        cv[...] = pv_ref[0]
        clo[...] = plo_ref[0]

    def token_shift(x_ref, carry, mu_ref):
        x = x_ref[...]
        row = lax.broadcasted_iota(jnp.int32, x.shape, 0)
        shifted = jnp.where(row == 0, carry[...], pltpu.roll(x, 1, 0))
        carry[...] = x[tb - 1:tb, :]
        return x + (shifted - x) * mu_ref[...]

    r = token_shift(r_ref, cr, mur_ref)
    k0 = token_shift(k_ref, ck, muk_ref)
    v = token_shift(v_ref, cv, muv_ref)
    lo = token_shift(lo_ref, clo, mulo_ref)

    col = lax.broadcasted_iota(jnp.int32, lo.shape, 1)
    act = jnp.where(col < 64, jnp.tanh(lo), jnp.where(col < 128, lo, _sigmoid(lo)))
    proj = _dot(act.astype(bf16), wl_ref[...])
    log_w = -_softplus(-(w0_ref[...] + proj[:, :BRANCH_WIDTH])) - 0.5
    log_decay = -jnp.exp(log_w)
    a = _sigmoid(a0_ref[...] + proj[:, BRANCH_WIDTH:2 * BRANCH_WIDTH])
    g = proj[:, 2 * BRANCH_WIDTH:]
    kk = k0 * kk_ref[...]
    kk = kk / jnp.maximum(jnp.sqrt(_seg64_sum(kk * kk, ones_ref)), 1e-12)
    k = k0 * (1.0 + (a - 1.0) * ka_ref[...])
    bonus = _seg64_sum(r * k * rk_ref[...], ones_ref) * v
    return r, k, v, kk, a, log_decay, g, bonus


def _rwkv_output(y, bonus, g, lg_ref, lb_ref, ones_ref):
    mu = _seg64_sum(y, ones_ref) * (1.0 / RWKV_HEAD)
    d = y - mu
    var = _seg64_sum(d * d, ones_ref) * (1.0 / RWKV_HEAD)
    yn = d * lax.rsqrt(var + 64e-5) * lg_ref[...] + lb_ref[...]
    return ((yn + bonus) * g).astype(bf16)


def _rwkv_seq_kernel(*refs, tb, steps):
    in_refs, w_refs = refs[0:8], refs[8:18]
    lg_ref, lb_ref, ones_ref, s0_ref, o_ref, s_ref = refs[18:24]
    carries = refs[24:28]
    w3, kk3, kka3, k3, r3, v3, y3, bonus_s, g_s = refs[28:]
    first = pl.program_id(1) == 0

    @pl.when(first)
    def _():
        s_ref[...] = s0_ref[...]

    r, k, v, kk, a, log_decay, g, bonus = _rwkv_features(in_refs, carries, w_refs, ones_ref, first, tb)
    decay = jnp.exp(log_decay)
    kka = kk * a
    g_s[...] = g
    bonus_s[...] = bonus
    for h in range(RWKV_HEADS):
        hs = slice(h * RWKV_HEAD, (h + 1) * RWKV_HEAD)
        w3[h] = decay[:, hs]
        kk3[h] = kk[:, hs]
        kka3[h] = kka[:, hs]
        k3[h] = k[:, hs]
        r3[h] = r[:, hs]
        v3[h] = v[:, hs]

    eye = (lax.broadcasted_iota(jnp.int32, (RWKV_HEAD, RWKV_HEAD), 0)
           == lax.broadcasted_iota(jnp.int32, (RWKV_HEAD, RWKV_HEAD), 1)).astype(f32)

    def step(t, carry):
        ts = pl.ds(t, 1)
        for h in range(RWKV_HEADS):
            s_old = s_ref[0, h]
            sa = jnp.sum(s_old * kk3[h, ts, :], axis=1, keepdims=True)
            v_col = jnp.sum(eye * v3[h, ts, :], axis=1, keepdims=True)
            s_new = s_old * w3[h, ts, :] - sa * kka3[h, ts, :] + v_col * k3[h, ts, :]
            s_ref[0, h] = s_new
            y_col = jnp.sum(s_new * r3[h, ts, :], axis=1, keepdims=True)
            y3[h, ts, :] = jnp.sum(eye * y_col, axis=0, keepdims=True)
        return carry

    lax.fori_loop(0, steps, step, 0)

    y = jnp.concatenate([y3[h] for h in range(RWKV_HEADS)], axis=-1)
    o_ref[...] = _rwkv_output(y, bonus_s[...], g_s[...], lg_ref, lb_ref, ones_ref)


def _unit_lower_inverse_minus_identity(ns):
    size = ns[0].shape[0]
    t = lax.broadcasted_iota(jnp.int32, (size, size), 0)
    s = lax.broadcasted_iota(jnp.int32, (size, size), 1)
    first = ((t >> 1) == (s >> 1)) & (t > s)
    es = [-jnp.where(first, n, 0.0) for n in ns]
    blk, shift = 4, 2
    while blk <= RWKV_HEAD:
        half = blk // 2
        sel = ((t >> shift) == (s >> shift)) & ((t & (blk - 1)) >= half) & ((s & (blk - 1)) < half)
        cs = [jnp.where(sel, n, 0.0) for n in ns]
        zs = [c + _dot(c.astype(bf16), e.astype(bf16)) for c, e in zip(cs, es)]
        es = [e - z - _dot(e.astype(bf16), z.astype(bf16)) for e, z in zip(es, zs)]
        blk, shift = blk * 2, shift + 1
    return es


def _rwkv_chunk_kernel(*refs, tb):
    in_refs, w_refs = refs[0:8], refs[8:18]
    lg_ref, lb_ref, ones_ref, s0_ref, o_ref, s_ref = refs[18:24]
    carries = refs[24:28]
    sp = refs[28]
    i = pl.program_id(1)
    first = i == 0
    n_pairs = RWKV_HEADS // 2
    hd = RWKV_HEAD

    @pl.when(first)
    def _():
        zero = jnp.zeros((hd, hd), f32)
        for p in range(n_pairs):
            top = jnp.concatenate([s0_ref[0, 2 * p], zero], axis=1)
            bot = jnp.concatenate([zero, s0_ref[0, 2 * p + 1]], axis=1)
            sp[p] = jnp.concatenate([top, bot], axis=0)

    r, k, v, kk, a, lw, g, bonus = _rwkv_features(in_refs, carries, w_refs, ones_ref, first, tb)
    beta = kk * a
    row = lax.broadcasted_iota(jnp.int32, lw.shape, 0)
    cum = lw
    d = 1
    while d < tb:
        cum = cum + jnp.where(row >= d, pltpu.roll(cum, d, 0), 0.0)
        d *= 2
    c_last = cum[tb - 1:tb, :]
    a_t = kk * jnp.exp(cum - lw)
    r_t = r * jnp.exp(cum)
    e_neg = jnp.exp(-cum)
    k_t = k * e_neg
    b_t = beta * e_neg
    e_hat = jnp.exp(c_last - cum)
    k_h = k * e_hat
    b_h = beta * e_hat
    gamma = jnp.exp(c_last)

    lane = lax.broadcasted_iota(jnp.int32, (tb, LANES), 1)
    head0 = lane < hd
    split = lambda x: (jnp.where(head0, x, 0.0), jnp.where(head0, 0.0, x))
    tt = lax.broadcasted_iota(jnp.int32, (LANES, LANES), 0)
    ss = lax.broadcasted_iota(jnp.int32, (LANES, LANES), 1)
    strict = tt > ss
    incl = tt >= ss
    same_head = (tt >= hd) == (ss >= hd)
    fold = lambda m: m[:tb, :] + m[tb:, :]

    pairs = range(n_pairs)
    cols = [slice(p * LANES, (p + 1) * LANES) for p in pairs]
    stack = lambda *xs: jnp.concatenate(xs, axis=0)
    scs = [_dot_nt(stack(*split(a_t[:, ps]), *split(r_t[:, ps])).astype(bf16),
                   stack(*split(k_t[:, ps]), *split(b_t[:, ps])).astype(bf16)) for ps in cols]
    es = _unit_lower_inverse_minus_identity([jnp.where(strict, sc[:2 * tb, 2 * tb:], 0.0) for sc in scs])
    eye = jnp.where(tt == ss, 1.0, 0.0)
    t_cats = [fold(eye + e).astype(bf16) for e in es]
    ak_cats = [fold(jnp.where(strict, sc[:2 * tb, :2 * tb], 0.0)).astype(bf16) for sc in scs]
    r_cats = [jnp.concatenate([fold(jnp.where(incl, sc[2 * tb:, :2 * tb], 0.0)),
                               -fold(jnp.where(incl, sc[2 * tb:, 2 * tb:], 0.0))], axis=1).astype(bf16)
              for sc in scs]
    s_olds = [sp[p] for p in pairs]
    grs = [_dot_nt(stack(a_t[:, ps], r_t[:, ps]).astype(bf16), s_old.astype(bf16))
           for ps, s_old in zip(cols, s_olds)]
    v_sts = [stack(*split(v[:, ps])).astype(bf16) for ps in cols]
    u_rhss = [gr[:tb] + _dot(ak, v_st) for gr, ak, v_st in zip(grs, ak_cats, v_sts)]
    us = [_dot(t_cat, stack(*split(u_rhs)).astype(bf16)) for t_cat, u_rhs in zip(t_cats, u_rhss)]
    ys = [gr[tb:] + _dot(r_cat, stack(v_st, stack(*split(u)).astype(bf16)))
          for gr, r_cat, v_st, u in zip(grs, r_cats, v_sts, us)]
    for p, ps, s_old, u in zip(pairs, cols, s_olds, us):
        vu_t = stack(v[:, ps], -u).T.astype(bf16)
        kb = stack(k_h[:, ps], b_h[:, ps]).astype(bf16)
        sp[p] = s_old * gamma[:, ps] + jnp.where(same_head, _dot(vu_t, kb), 0.0)

    o_ref[...] = _rwkv_output(jnp.concatenate(ys, axis=-1), bonus, g, lg_ref, lb_ref, ones_ref)

    @pl.when(i == pl.num_programs(1) - 1)
    def _():
        for p in range(n_pairs):
            full = sp[p]
            s_ref[0, 2 * p] = full[:hd, :hd]
            s_ref[0, 2 * p + 1] = full[hd:, hd:]


def rwkv(u, prev, p, s0, ones2, batch, l_pad, l_real, tb, chunked):
    m = u.shape[0]
    nblk = l_pad // tb
    row = lambda b, i: b * nblk + i
    full = lambda w: pl.BlockSpec((1, w), lambda b, i: (0, 0))
    prev_spec = lambda w: pl.BlockSpec((1, 1, w), lambda b, i: (b, 0, 0))
    bw = BRANCH_WIDTH
    state_spec = pl.BlockSpec((1, RWKV_HEADS, RWKV_HEAD, RWKV_HEAD), lambda b, i: (b, 0, 0, 0))
    carries = [pltpu.VMEM((1, bw), f32), pltpu.VMEM((1, bw), f32), pltpu.VMEM((1, bw), f32),
               pltpu.VMEM((1, RWKV_LORA), f32)]
    if chunked:
        assert l_real == l_pad and tb == RWKV_HEAD
        body = functools.partial(_rwkv_chunk_kernel, tb=tb)
        scratch = carries + [pltpu.VMEM((RWKV_HEADS // 2, LANES, LANES), f32)]
    else:
        steps = tb if l_real == l_pad else l_real
        body = functools.partial(_rwkv_seq_kernel, tb=tb, steps=steps)
        head_buf = pltpu.VMEM((RWKV_HEADS, tb, RWKV_HEAD), f32)
        scratch = carries + [head_buf] * 7 + [pltpu.VMEM((tb, bw), f32), pltpu.VMEM((tb, bw), f32)]
    return pl.pallas_call(
        body,
        grid=(batch, nblk),
        in_specs=[pl.BlockSpec((tb, bw), lambda b, i: (row(b, i), C_R // bw)),
                  pl.BlockSpec((tb, bw), lambda b, i: (row(b, i), C_K // bw)),
                  pl.BlockSpec((tb, bw), lambda b, i: (row(b, i), C_V // bw)),
                  pl.BlockSpec((tb, RWKV_LORA), lambda b, i: (row(b, i), C_LORA // RWKV_LORA)),
                  prev_spec(bw), prev_spec(bw), prev_spec(bw), prev_spec(RWKV_LORA),
                  full(bw), full(bw), full(bw), full(RWKV_LORA),
                  full(bw), full(bw),
                  pl.BlockSpec((RWKV_LORA, 3 * bw), lambda b, i: (0, 0)),
                  full(bw), full(bw), full(bw), full(bw), full(bw),
                  pl.BlockSpec((LANES, LANES), lambda b, i: (0, 0)),
                  state_spec],
        out_specs=[pl.BlockSpec((tb, bw), lambda b, i: (row(b, i), 0)), state_spec],
        out_shape=[jax.ShapeDtypeStruct((m, bw), bf16),
                   jax.ShapeDtypeStruct((batch, RWKV_HEADS, RWKV_HEAD, RWKV_HEAD), f32)],
        scratch_shapes=scratch,
        compiler_params=_params(("parallel", "arbitrary")),
        name="rwkv7_chunked" if chunked else "rwkv7_seq",
    )(u, u, u, u, *prev, p['mu_r'], p['mu_k'], p['mu_v'], p['mu_lo'], p['w0'], p['a0'], p['w_lora'],
      p['k_k'], p['k_a'], p['r_k'], p['ln_g'], p['ln_b'], ones2, s0)


def _pack_w_in(w):
    seg = lambda o, n: w[:, o:o + n]
    parts = [seg(_O_GV, 1024), seg(_O_GR, 1024), seg(_O_SQ, 1024), seg(_O_MQ, 1024),
             seg(_O_RU, 3072), seg(_O_GPRE, 8192), seg(_O_GQ, 512), seg(_O_GK, 512),
             seg(_O_RU + 3072, RWKV_LORA), seg(_O_SK, 128), seg(_O_SV, 128), seg(_O_GA, GLA_GATE_RANK),
             jnp.zeros((w.shape[0], N_PACK - C_GA - GLA_GATE_RANK), w.dtype)]
    return jnp.concatenate(parts, axis=1).astype(bf16)


def _layer_params(l, w):
    row = lambda x: x[l].reshape(1, -1)
    mu = w['rwkv_mu'][l]
    z = lambda r, c: jnp.zeros((r, c), f32)
    w_lora = jnp.concatenate([
        jnp.concatenate([w['rwkv_w2'][l], z(64, 1024), z(64, 1024)], axis=1),
        jnp.concatenate([z(64, 1024), w['rwkv_a2'][l], z(64, 1024)], axis=1),
        jnp.concatenate([z(128, 1024), z(128, 1024), w['rwkv_g2'][l]], axis=1)], axis=0).astype(bf16)
    a_up = jnp.concatenate([w['gla_a_up'][l], z(LANES - GLA_GATE_RANK, GLA_QK)], axis=0).astype(bf16)
    return {
        'w_in': _pack_w_in(w['w_in'][l]),
        'gate_b': w['gate_b'][l],
        'a_up': a_up, 'a_b': row(w['gla_a_b']),
        'gla_g': row(w['gla_norm_g']), 'gla_b': row(w['gla_norm_b']),
        'sinks': w['swa_sinks'][l],
        'rwkv': {'mu_r': mu[:1024].reshape(1, -1), 'mu_k': mu[1024:2048].reshape(1, -1),
                 'mu_v': mu[2048:3072].reshape(1, -1), 'mu_lo': mu[3072:].reshape(1, -1),
                 'w0': row(w['rwkv_w0']), 'a0': row(w['rwkv_a0']), 'w_lora': w_lora,
                 'k_k': row(w['rwkv_k_k']), 'k_a': row(w['rwkv_k_a']), 'r_k': row(w['rwkv_r_k']),
                 'ln_g': row(w['rwkv_ln_g']), 'ln_b': row(w['rwkv_ln_b'])},
        'w_mem_kv': w['w_mem_kv'][l].astype(bf16),
        'w_branch': w['w_branch'][l].astype(bf16),
        'w_out': w['w_out'][l].astype(bf16),
        'ln1_g': row(w['ln1_g']), 'ln1_b': row(w['ln1_b']),
        'w_gu': w['w_gu'][l].astype(bf16),
        'w_down': w['w_down'][l].astype(bf16),
        'ln2_g': row(w['ln2_g']), 'ln2_b': row(w['ln2_b']),
    }


def _tiles(m):
    if m >= 1024:
        return {'proj': 1024, 'merge': 512, 'ffn': 1024, 'ln': 512}
    return {'proj': m, 'merge': m, 'ffn': m, 'ln': m}


def _trunk_layer(h_f, h_b, p, ones2, mem_k, mem_v, gla_s0, rwkv_s0, rwkv_prev, swa_buf,
                 batch, l_pad, l_real):
    m = h_f.shape[0]
    tl = _tiles(m)
    u = matmul(h_b, p['w_in'], tl['proj'], 1024, "in_proj")
    o_a, gla_s = gla(u, p['a_up'], p['a_b'], p['gla_g'], p['gla_b'], gla_s0, batch, l_pad, l_real,
                     min(GLA_CHUNK, l_pad))
    if swa_buf is None:
        o_b = swa_prompt(u, p['sinks'], batch, l_pad)
    else:
        o_b = swa_sample(u, swa_buf[0], swa_buf[1], p['sinks'], batch, l_pad, l_real, 4)
    o_c, rwkv_s = rwkv(u, rwkv_prev, p['rwkv'], rwkv_s0, ones2, batch, l_pad, l_real, min(RWKV_HEAD, l_pad),
                       chunked=l_real == l_pad and l_pad % RWKV_HEAD == 0)
    o_m = mem_attention(u, mem_k[0], mem_k[1], mem_v[0], mem_v[1], batch, l_pad, min(512, l_pad))
    merged = gated_merge((o_a, o_b, o_c, o_m), u, p['gate_b'], p['w_branch'], tl['merge'], 512)
    x_f, x_b = matmul_residual_ln(merged, p['w_out'], h_f, p['ln1_g'], p['ln1_b'], tl['ln'], D_MODEL)
    act = ffn_up(x_b, p['w_gu'], tl['ffn'], 512)
    y_f, y_b = matmul_residual_ln(act, p['w_down'], x_f, p['ln2_g'], p['ln2_b'], tl['ln'], D_FF // 4)
    return y_f, y_b, u, gla_s, rwkv_s


def _split_ru(x):
    return (x[..., :1024], x[..., 1024:2048], x[..., 2048:3072], x[..., 3072:])


def kernel(x_prompt, x_sample, mem_prompt, cache_swa_k, cache_swa_v, cache_mem_k, cache_mem_v, state_gla, state_rwkv, state_rwkv_shift, w_in, gate_b, gla_a_up, gla_a_b, gla_norm_g, gla_norm_b, swa_sinks, rwkv_mu, rwkv_w0, rwkv_w2, rwkv_a0, rwkv_a2, rwkv_g2, rwkv_k_k, rwkv_k_a, rwkv_r_k, rwkv_ln_g, rwkv_ln_b, w_mem_kv, w_branch, w_out, ln1_g, ln1_b, w_gu, w_down, ln2_g, ln2_b):
    weights = {'w_in': w_in, 'gate_b': gate_b, 'gla_a_up': gla_a_up, 'gla_a_b': gla_a_b,
               'gla_norm_g': gla_norm_g, 'gla_norm_b': gla_norm_b, 'swa_sinks': swa_sinks,
               'rwkv_mu': rwkv_mu, 'rwkv_w0': rwkv_w0, 'rwkv_w2': rwkv_w2, 'rwkv_a0': rwkv_a0,
               'rwkv_a2': rwkv_a2, 'rwkv_g2': rwkv_g2, 'rwkv_k_k': rwkv_k_k, 'rwkv_k_a': rwkv_k_a,
               'rwkv_r_k': rwkv_r_k, 'rwkv_ln_g': rwkv_ln_g, 'rwkv_ln_b': rwkv_ln_b,
               'w_mem_kv': w_mem_kv, 'w_branch': w_branch, 'w_out': w_out, 'ln1_g': ln1_g,
               'ln1_b': ln1_b, 'w_gu': w_gu, 'w_down': w_down, 'ln2_g': ln2_g, 'ln2_b': ln2_b}
    bp, lp, _ = x_prompt.shape
    bs, ls, _ = x_sample.shape
    ls_pad = -(-ls // SUBLANES) * SUBLANES
    mp, ms = bp * lp, bs * ls_pad

    hp_f = x_prompt.reshape(mp, D_MODEL)
    hs_f = jnp.pad(x_sample, ((0, 0), (0, ls_pad - ls), (0, 0))).reshape(ms, D_MODEL)
    hp_b, hs_b = hp_f.astype(bf16), hs_f.astype(bf16)
    mem_b = mem_prompt.reshape(bp * MEM_TOKENS, D_MODEL).astype(bf16)
    half = LANES // 2
    blk = jnp.ones((half, half), f32)
    zero = jnp.zeros((half, half), f32)
    ones2 = jnp.concatenate([jnp.concatenate([blk, zero], 1), jnp.concatenate([zero, blk], 1)], 0).astype(bf16)

    gla0_p = jnp.zeros((bp, GLA_HEADS, GLA_DK, GLA_DV), f32)
    rwkv0_p = jnp.zeros((bp, RWKV_HEADS, RWKV_HEAD, RWKV_HEAD), f32)
    prev0_p = _split_ru(jnp.zeros((bp, 1, RWKV_COLS), f32))

    outs = {k: [] for k in ('p_swk', 'p_swv', 'p_mk', 'p_mv', 'p_gla', 'p_rw', 'p_rs',
                            's_swk', 's_swv', 's_gla', 's_rw', 's_rs')}
    kvw = SWA_KV_HEADS * SWA_HEAD_DIM
    for l in range(DEPTH):
        p = _layer_params(l, weights)
        kv = matmul(mem_b, p['w_mem_kv'], bp * MEM_TOKENS, 512, "mem_kv")
        hp_f, hp_b, u, gs, rs = _trunk_layer(hp_f, hp_b, p, ones2, (kv, 0), (kv, 1), gla0_p, rwkv0_p,
                                             prev0_p, None, bp, lp, lp)
        u3 = u.reshape(bp, lp, N_PACK)
        outs['p_swk'].append(u3[:, lp - WINDOW:, C_SK:C_SK + kvw].reshape(bp, WINDOW, SWA_KV_HEADS, SWA_HEAD_DIM))
        outs['p_swv'].append(u3[:, lp - WINDOW:, C_SV:C_SV + kvw].reshape(bp, WINDOW, SWA_KV_HEADS, SWA_HEAD_DIM))
        outs['p_mk'].append(kv[:, :1024].reshape(bp, MEM_TOKENS, MEM_HEADS, MEM_HEAD_DIM))
        outs['p_mv'].append(kv[:, 1024:].reshape(bp, MEM_TOKENS, MEM_HEADS, MEM_HEAD_DIM))
        outs['p_gla'].append(gs)
        outs['p_rw'].append(rs)
        outs['p_rs'].append(jnp.concatenate([u3[:, lp - 1:, C_R:C_R + 3072],
                                             u3[:, lp - 1:, C_LORA:C_LORA + RWKV_LORA]], axis=-1))
        kbuf = cache_swa_k[l].reshape(bs * WINDOW, kvw)
        vbuf = cache_swa_v[l].reshape(bs * WINDOW, kvw)
        mk = cache_mem_k[l].reshape(bs * MEM_TOKENS, MEM_HEADS * MEM_HEAD_DIM)
        mv = cache_mem_v[l].reshape(bs * MEM_TOKENS, MEM_HEADS * MEM_HEAD_DIM)
        hs_f, hs_b, u, gs, rs = _trunk_layer(hs_f, hs_b, p, ones2, (mk, 0), (mv, 0), state_gla[l],
                                             state_rwkv[l], _split_ru(state_rwkv_shift[l]), (kbuf, vbuf),
                                             bs, ls_pad, ls)
        u3 = u.reshape(bs, ls_pad, N_PACK)
        k_new = u3[:, :ls, C_SK:C_SK + kvw].reshape(bs, ls, SWA_KV_HEADS, SWA_HEAD_DIM)
        v_new = u3[:, :ls, C_SV:C_SV + kvw].reshape(bs, ls, SWA_KV_HEADS, SWA_HEAD_DIM)
        outs['s_swk'].append(jnp.concatenate([cache_swa_k[l][:, ls:], k_new], axis=1))
        outs['s_swv'].append(jnp.concatenate([cache_swa_v[l][:, ls:], v_new], axis=1))
        outs['s_gla'].append(gs)
        outs['s_rw'].append(rs)
        outs['s_rs'].append(jnp.concatenate([u3[:, ls - 1:ls, C_R:C_R + 3072],
                                             u3[:, ls - 1:ls, C_LORA:C_LORA + RWKV_LORA]], axis=-1))

    st = {k: jnp.stack(v) for k, v in outs.items()}
    y_prompt = hp_f.reshape(bp, lp, D_MODEL)
    y_sample = hs_f.reshape(bs, ls_pad, D_MODEL)[:, :ls]
    return (y_prompt, y_sample,
            st['p_swk'], st['p_swv'], st['p_mk'], st['p_mv'], st['p_gla'], st['p_rw'], st['p_rs'],
            st['s_swk'], st['s_swv'], st['s_gla'], st['s_rw'], st['s_rs'])
```

```python
import functools

import jax
import jax.numpy as jnp
from jax import lax
from jax.experimental import pallas as pl
from jax.experimental.pallas import tpu as pltpu

f32 = jnp.float32
bf16 = jnp.bfloat16

D_MODEL = 2048
DEPTH = 4
BRANCH_WIDTH = 1024
N_BRANCH = 4
GLA_HEADS = 4
GLA_QK = 512
GLA_V = 1024
GLA_DK = 128
GLA_DV = 256
GLA_GATE_RANK = 16
GLA_TAU = 16.0
GLA_CHUNK = 64
SWA_HEAD_DIM = 64
SWA_Q_HEADS = 16
SWA_KV_HEADS = 2
SWA_GROUP = SWA_Q_HEADS // SWA_KV_HEADS
WINDOW = 128
RWKV_HEAD = 64
RWKV_HEADS = 16
RWKV_LORA = 256
RWKV_COLS = 3 * BRANCH_WIDTH + RWKV_LORA
MEM_TOKENS = 256
MEM_HEADS = 4
MEM_HEAD_DIM = 256
D_FF = 5632
DEEPNORM_ALPHA = (2 * DEPTH) ** 0.25
NEG = -1e30

_O_GQ, _O_GK, _O_GV, _O_GR, _O_GA = 0, 512, 1024, 2048, 3072
_O_SQ, _O_SK, _O_SV = 3088, 4112, 4240
_O_RU = 4368
_O_MQ = 7696
_O_GPRE = 8720
_N_IN = 16912

C_GV, C_GR, C_SQ, C_MQ, C_R, C_K, C_V = 0, 1024, 2048, 3072, 4096, 5120, 6144
C_GPRE = 7168
C_GQ, C_GK = 15360, 15872
C_LORA = 16384
C_SK, C_SV, C_GA = 16640, 16768, 16896
N_PACK = 17408

LANES = 128
SUBLANES = 8
VMEM_BYTES_V7X = 64 * 1024 * 1024
VMEM_LIMIT = VMEM_BYTES_V7X - 8 * 1024 * 1024


def _params(sem):
    return pltpu.CompilerParams(dimension_semantics=sem, vmem_limit_bytes=VMEM_LIMIT)


def _softplus(z):
    return jnp.maximum(z, 0.0) + jnp.log(1.0 + jnp.exp(-jnp.abs(z)))


def _sigmoid(z):
    return 1.0 / (1.0 + jnp.exp(-z))


def _dot(a, b):
    return jnp.dot(a, b, preferred_element_type=f32)


def _dot_nt(a, b):
    return lax.dot_general(a, b, (((1,), (1,)), ((), ())), preferred_element_type=f32)


def _mm_kernel(x_ref, w_ref, o_ref, *, trans_w):
    mm = _dot_nt if trans_w else _dot
    o_ref[...] = mm(x_ref[...], w_ref[...])


def matmul(x, w, layer, tm, tn, name, trans_w=False):
    m, k = x.shape
    n = w.shape[1] if trans_w else w.shape[2]
    w_spec = (pl.BlockSpec((None, tn, k), lambda i, j: (layer, j, 0)) if trans_w
              else pl.BlockSpec((None, k, tn), lambda i, j: (layer, 0, j)))
    return pl.pallas_call(
        functools.partial(_mm_kernel, trans_w=trans_w),
        grid=(m // tm, n // tn),
        in_specs=[pl.BlockSpec((tm, k), lambda i, j: (i, 0)), w_spec],
        out_specs=pl.BlockSpec((tm, tn), lambda i, j: (i, j)),
        out_shape=jax.ShapeDtypeStruct((m, n), f32),
        compiler_params=_params(("parallel", "parallel")),
        name=name,
    )(x, w)


def _mm_ln_kernel(x_ref, w_ref, res_ref, g_ref, b_ref, of_ref, ob_ref, *, nk):
    k = pl.program_id(1)
    part = _dot(x_ref[...], w_ref[...])

    @pl.when(k == 0)
    def _():
        of_ref[...] = part

    @pl.when(k > 0)
    def _():
        of_ref[...] += part

    @pl.when(k == nk - 1)
    def _():
        z = DEEPNORM_ALPHA * res_ref[...] + of_ref[...]
        mu = jnp.mean(z, axis=-1, keepdims=True)
        d = z - mu
        var = jnp.mean(d * d, axis=-1, keepdims=True)
        y = d * lax.rsqrt(var + 1e-5) * g_ref[...] + b_ref[...]
        of_ref[...] = y
        ob_ref[...] = y.astype(bf16)


def matmul_residual_ln(x, w, layer, res, g, b, tm, tk):
    m, k = x.shape
    n = w.shape[2]
    nk = k // tk
    return pl.pallas_call(
        functools.partial(_mm_ln_kernel, nk=nk),
        grid=(m // tm, nk),
        in_specs=[pl.BlockSpec((tm, tk), lambda i, kk: (i, kk)),
                  pl.BlockSpec((None, tk, n), lambda i, kk: (layer, kk, 0)),
                  pl.BlockSpec((tm, n), lambda i, kk: (i, 0)),
                  pl.BlockSpec((1, n), lambda i, kk: (0, 0)),
                  pl.BlockSpec((1, n), lambda i, kk: (0, 0))],
        out_specs=[pl.BlockSpec((tm, n), lambda i, kk: (i, 0)),
                   pl.BlockSpec((tm, n), lambda i, kk: (i, 0))],
        out_shape=[jax.ShapeDtypeStruct((m, n), f32), jax.ShapeDtypeStruct((m, n), bf16)],
        compiler_params=_params(("parallel", "arbitrary")),
        name="proj_ln",
    )(x, w, res, g, b)


def _merge_kernel(a_ref, b_ref, c_ref, m_ref, g0_ref, g1_ref, g2_ref, g3_ref, gb_ref, w_ref, o_ref):
    acc = None
    for n, (br, gp) in enumerate(((a_ref, g0_ref), (b_ref, g1_ref), (c_ref, g2_ref), (m_ref, g3_ref))):
        y = _dot(br[...], w_ref[n])
        gate = _sigmoid(gp[...] + gb_ref[n:n + 1, :])
        acc = gate * y if acc is None else acc + gate * y
    o_ref[...] = acc.astype(bf16)


def gated_merge(branches, u, gate_b, w_branch, layer, tm, tn):
    m = u.shape[0]
    gp0 = C_GPRE // tn
    per = D_MODEL // tn
    br_spec = pl.BlockSpec((tm, BRANCH_WIDTH), lambda i, j: (i, 0))
    gp_specs = [pl.BlockSpec((tm, tn), functools.partial(lambda i, j, n: (i, gp0 + n * per + j), n=n))
                for n in range(N_BRANCH)]
    return pl.pallas_call(
        _merge_kernel,
        grid=(m // tm, D_MODEL // tn),
        in_specs=[br_spec] * 4 + gp_specs + [
            pl.BlockSpec((None, N_BRANCH, tn), lambda i, j: (layer, 0, j)),
            pl.BlockSpec((None, N_BRANCH, BRANCH_WIDTH, tn), lambda i, j: (layer, 0, 0, j))],
        out_specs=pl.BlockSpec((tm, tn), lambda i, j: (i, j)),
        out_shape=jax.ShapeDtypeStruct((m, D_MODEL), bf16),
        compiler_params=_params(("parallel", "parallel")),
        name="gated_merge",
    )(*branches, u, u, u, u, gate_b, w_branch)


def _ffn_up_kernel(x_ref, wg_ref, wu_ref, o_ref):
    x = x_ref[...]
    g = _dot(x, wg_ref[...])
    up = _dot(x, wu_ref[...])
    o_ref[...] = (g * _sigmoid(g) * up).astype(bf16)


def ffn_up(x, w_gu, layer, tm, tn):
    m, k = x.shape
    nj = D_FF // tn
    return pl.pallas_call(
        _ffn_up_kernel,
        grid=(m // tm, nj),
        in_specs=[pl.BlockSpec((tm, k), lambda i, j: (i, 0)),
                  pl.BlockSpec((None, k, tn), lambda i, j: (layer, 0, j)),
                  pl.BlockSpec((None, k, tn), lambda i, j: (layer, 0, nj + j))],
        out_specs=pl.BlockSpec((tm, tn), lambda i, j: (i, j)),
        out_shape=jax.ShapeDtypeStruct((m, D_FF), bf16),
        compiler_params=_params(("parallel", "parallel")),
        name="ffn_up",
    )(x, w_gu, w_gu)


def _gla_kernel(q_ref, k_ref, v_ref, r_ref, a_ref, aup_ref, ab_ref, ng_ref, nb_ref, s0_ref,
                o_ref, s_ref, *, chunk, l_real, l_pad):
    c = pl.program_id(1)

    @pl.when(c == 0)
    def _():
        s_ref[...] = s0_ref[...]

    la = _dot(a_ref[...].astype(bf16), aup_ref[...]) + ab_ref[...]
    la = (jnp.minimum(la, 0.0) - jnp.log(1.0 + jnp.exp(-jnp.abs(la)))) * (1.0 / GLA_TAU)
    row = lax.broadcasted_iota(jnp.int32, (chunk, GLA_QK), 0)
    kin = k_ref[...]
    if l_real < l_pad:
        real = (c * chunk + row) < l_real
        la = jnp.where(real, la, 0.0)
        kin = jnp.where(real, kin, 0.0)
    b = la
    d = 1
    while d < chunk:
        b = b + jnp.where(row >= d, pltpu.roll(b, d, 0), 0.0)
        d *= 2
    b_last = b[chunk - 1:chunk, :]
    q_dec = q_ref[...] * (GLA_DK ** -0.5) * jnp.exp(b)
    k_inv = kin * jnp.exp(-b)
    k_dec = kin * jnp.exp(b_last - b)
    tt = lax.broadcasted_iota(jnp.int32, (chunk, chunk), 0)
    ss = lax.broadcasted_iota(jnp.int32, (chunk, chunk), 1)
    causal = tt >= ss
    for h in range(GLA_HEADS):
        ks = slice(h * GLA_DK, (h + 1) * GLA_DK)
        vs = slice(h * GLA_DV, (h + 1) * GLA_DV)
        qh = q_dec[:, ks].astype(bf16)
        att = jnp.where(causal, _dot_nt(qh, k_inv[:, ks].astype(bf16)), 0.0)
        vh = v_ref[:, vs].astype(bf16)
        s_old = s_ref[0, h]
        o = _dot(att.astype(bf16), vh) + _dot(qh, s_old.astype(bf16))
        decay_col = jnp.exp(jnp.sum(la[:, ks].T, axis=1, keepdims=True))
        s_ref[0, h] = decay_col * s_old + _dot(k_dec[:, ks].T.astype(bf16), vh)
        mu = jnp.mean(o, axis=-1, keepdims=True)
        dd = o - mu
        var = jnp.mean(dd * dd, axis=-1, keepdims=True)
        nrm = dd * lax.rsqrt(var + 1e-5) * ng_ref[:, vs] + nb_ref[:, vs]
        gr = r_ref[:, vs]
        o_ref[:, vs] = (nrm * (gr * _sigmoid(gr))).astype(bf16)


def gla(u, a_up, a_b, norm_g, norm_b, s0, layer, batch, l_pad, l_real, chunk):
    m = u.shape[0]
    nc = l_pad // chunk
    row = lambda b, c: b * nc + c
    return pl.pallas_call(
        functools.partial(_gla_kernel, chunk=chunk, l_real=l_real, l_pad=l_pad),
        grid=(batch, nc),
        in_specs=[pl.BlockSpec((chunk, GLA_QK), lambda b, c: (row(b, c), C_GQ // GLA_QK)),
                  pl.BlockSpec((chunk, GLA_QK), lambda b, c: (row(b, c), C_GK // GLA_QK)),
                  pl.BlockSpec((chunk, GLA_V), lambda b, c: (row(b, c), C_GV // GLA_V)),
                  pl.BlockSpec((chunk, GLA_V), lambda b, c: (row(b, c), C_GR // GLA_V)),
                  pl.BlockSpec((chunk, LANES), lambda b, c: (row(b, c), C_GA // LANES)),
                  pl.BlockSpec((LANES, GLA_QK), lambda b, c: (0, 0)),
                  pl.BlockSpec((1, GLA_QK), lambda b, c: (0, 0)),
                  pl.BlockSpec((1, GLA_V), lambda b, c: (0, 0)),
                  pl.BlockSpec((1, GLA_V), lambda b, c: (0, 0)),
                  pl.BlockSpec((None, 1, GLA_HEADS, GLA_DK, GLA_DV), lambda b, c: (layer, b, 0, 0, 0))],
        out_specs=[pl.BlockSpec((chunk, GLA_V), lambda b, c: (row(b, c), 0)),
                   pl.BlockSpec((1, GLA_HEADS, GLA_DK, GLA_DV), lambda b, c: (b, 0, 0, 0))],
        out_shape=[jax.ShapeDtypeStruct((m, GLA_V), bf16),
                   jax.ShapeDtypeStruct((batch, GLA_HEADS, GLA_DK, GLA_DV), f32)],
        compiler_params=_params(("parallel", "arbitrary")),
        name="gla",
    )(u, u, u, u, u, a_up, a_b, norm_g, norm_b, s0)


def _alibi_slopes():
    return 2.0 ** (-8.0 * jnp.arange(1, SWA_Q_HEADS + 1, dtype=f32) / SWA_Q_HEADS)


def _swa_bias(t_pos, key_pos, key_ok):
    dist = (t_pos[:, None] - key_pos[None, :]).astype(f32)
    valid = (dist >= 0) & (dist <= WINDOW) & key_ok[None, :]
    slopes = _alibi_slopes().reshape(SWA_KV_HEADS, SWA_GROUP, 1, 1)
    bias = jnp.where(valid[None, None], -slopes * dist[None, None], NEG)
    return bias.reshape(SWA_KV_HEADS, SWA_GROUP * t_pos.shape[0], key_pos.shape[0])


def _swa_group(q_all, kv, keys, vals, bias, sink_ref, rows):
    ds = slice(kv * SWA_HEAD_DIM, (kv + 1) * SWA_HEAD_DIM)
    heads = range(kv * SWA_GROUP, (kv + 1) * SWA_GROUP)
    q = jnp.concatenate([q_all[:, h * SWA_HEAD_DIM:(h + 1) * SWA_HEAD_DIM] for h in heads], axis=0)
    k2 = jnp.concatenate([kb[:, ds] for kb in keys], axis=0).astype(bf16)
    v2 = jnp.concatenate([vb[:, ds] for vb in vals], axis=0).astype(bf16)
    s = _dot_nt(q.astype(bf16), k2) * (SWA_HEAD_DIM ** -0.5) + bias
    sink = jnp.concatenate([jnp.full((rows, 1), sink_ref[h], f32) for h in heads], axis=0)
    m = jnp.maximum(jnp.max(s, axis=-1, keepdims=True), sink)
    p = jnp.exp(s - m)
    den = jnp.sum(p, axis=-1, keepdims=True) + jnp.exp(sink - m)
    o = _dot((p / den).astype(bf16), v2)
    return [o[g * rows:(g + 1) * rows, :] for g in range(SWA_GROUP)]


def _swa_prompt_kernel(sink_ref, bias_ref, q_ref, kc_ref, kp_ref, vc_ref, vp_ref, o_ref):
    q_all = q_ref[...]
    keys = (kp_ref[...], kc_ref[...])
    vals = (vp_ref[...], vc_ref[...])
    outs = []
    for kv in range(SWA_KV_HEADS):
        outs += _swa_group(q_all, kv, keys, vals, bias_ref[0, kv], sink_ref, WINDOW)
    o_ref[...] = jnp.concatenate(outs, axis=-1).astype(bf16)


def swa_prompt(u, sinks, batch, seq):
    m = u.shape[0]
    nb = seq // WINDOW
    cur = lambda b, i: b * nb + i
    prev = lambda b, i: b * nb + jnp.maximum(i - 1, 0)
    kcol, vcol = C_SK // LANES, C_SV // LANES
    t_pos = jnp.arange(WINDOW)
    key_pos = jnp.arange(2 * WINDOW) - WINDOW
    bias = jnp.stack([_swa_bias(t_pos, key_pos, key_pos >= 0), _swa_bias(t_pos, key_pos, key_pos >= -WINDOW)])
    return pl.pallas_call(
        _swa_prompt_kernel,
        grid=(batch, nb),
        in_specs=[pl.BlockSpec(memory_space=pltpu.SMEM),
                  pl.BlockSpec((1,) + bias.shape[1:], lambda b, i: (jnp.minimum(i, 1), 0, 0, 0)),
                  pl.BlockSpec((WINDOW, BRANCH_WIDTH), lambda b, i: (cur(b, i), C_SQ // BRANCH_WIDTH)),
                  pl.BlockSpec((WINDOW, LANES), lambda b, i: (cur(b, i), kcol)),
                  pl.BlockSpec((WINDOW, LANES), lambda b, i: (prev(b, i), kcol)),
                  pl.BlockSpec((WINDOW, LANES), lambda b, i: (cur(b, i), vcol)),
                  pl.BlockSpec((WINDOW, LANES), lambda b, i: (prev(b, i), vcol))],
        out_specs=pl.BlockSpec((WINDOW, BRANCH_WIDTH), lambda b, i: (cur(b, i), 0)),
        out_shape=jax.ShapeDtypeStruct((m, BRANCH_WIDTH), bf16),
        compiler_params=_params(("parallel", "parallel")),
        name="swa_prompt",
    )(sinks, bias, u, u, u, u, u)


def _swa_sample_kernel(sink_ref, bias_ref, q_ref, kn_ref, vn_ref, kb_ref, vb_ref, o_ref, *, l_pad, reqs):
    for r in range(reqs):
        rs = slice(r * l_pad, (r + 1) * l_pad)
        bs = slice(r * WINDOW, (r + 1) * WINDOW)
        q_all = q_ref[rs, :]
        keys = (kb_ref[bs, :], kn_ref[rs, :])
        vals = (vb_ref[bs, :], vn_ref[rs, :])
        outs = []
        for kv in range(SWA_KV_HEADS):
            outs += _swa_group(q_all, kv, keys, vals, bias_ref[kv], sink_ref, l_pad)
        o_ref[rs, :] = jnp.concatenate(outs, axis=-1).astype(bf16)


def swa_sample(u, kbuf, vbuf, layer, sinks, batch, l_pad, l_real, reqs):
    m = u.shape[0]
    kcol, vcol = C_SK // LANES, C_SV // LANES
    key_pos = jnp.concatenate([jnp.arange(WINDOW) - WINDOW, jnp.arange(l_pad)])
    key_ok = jnp.concatenate([jnp.ones((WINDOW,), bool), jnp.arange(l_pad) < l_real])
    bias = _swa_bias(jnp.arange(l_pad), key_pos, key_ok)
    return pl.pallas_call(
        functools.partial(_swa_sample_kernel, l_pad=l_pad, reqs=reqs),
        grid=(batch // reqs,),
        in_specs=[pl.BlockSpec(memory_space=pltpu.SMEM),
                  pl.BlockSpec(bias.shape, lambda b: (0, 0, 0)),
                  pl.BlockSpec((reqs * l_pad, BRANCH_WIDTH), lambda b: (b, C_SQ // BRANCH_WIDTH)),
                  pl.BlockSpec((reqs * l_pad, LANES), lambda b: (b, kcol)),
                  pl.BlockSpec((reqs * l_pad, LANES), lambda b: (b, vcol)),
                  pl.BlockSpec((None, reqs * WINDOW, LANES), lambda b: (layer, b, 0)),
                  pl.BlockSpec((None, reqs * WINDOW, LANES), lambda b: (layer, b, 0))],
        out_specs=pl.BlockSpec((reqs * l_pad, BRANCH_WIDTH), lambda b: (b, 0)),
        out_shape=jax.ShapeDtypeStruct((m, BRANCH_WIDTH), bf16),
        compiler_params=_params(("parallel",)),
        name="swa_sample",
    )(sinks, bias, u, u, u, kbuf, vbuf)


def _mem_kernel(q_ref, k_ref, v_ref, o_ref, *, head_axis):
    for h in range(MEM_HEADS):
        hs = slice(h * MEM_HEAD_DIM, (h + 1) * MEM_HEAD_DIM)
        kh, vh = (k_ref[:, h, :], v_ref[:, h, :]) if head_axis else (k_ref[:, hs], v_ref[:, hs])
        s = _dot_nt(q_ref[:, hs].astype(bf16), kh.astype(bf16)) * (MEM_HEAD_DIM ** -0.5)
        p = jnp.exp(s - jnp.max(s, axis=-1, keepdims=True))
        p = p / jnp.sum(p, axis=-1, keepdims=True)
        o_ref[:, hs] = _dot(p.astype(bf16), vh.astype(bf16)).astype(bf16)


def mem_attention(u, kv, cache, layer, batch, l_pad, tl):
    m = u.shape[0]
    nl = l_pad // tl
    width = MEM_HEADS * MEM_HEAD_DIM
    if cache is None:
        kv_args = (kv, kv)
        kv_specs = [pl.BlockSpec((MEM_TOKENS, width), lambda b, i: (b, 0)),
                    pl.BlockSpec((MEM_TOKENS, width), lambda b, i: (b, 1))]
    else:
        kv_args = cache
        kv_specs = [pl.BlockSpec((None, None, MEM_TOKENS, MEM_HEADS, MEM_HEAD_DIM),
                                 lambda b, i: (layer, b, 0, 0, 0))] * 2
    return pl.pallas_call(
        functools.partial(_mem_kernel, head_axis=cache is not None),
        grid=(batch, nl),
        in_specs=[pl.BlockSpec((tl, width), lambda b, i: (b * nl + i, C_MQ // width))] + kv_specs,
        out_specs=pl.BlockSpec((tl, width), lambda b, i: (b * nl + i, 0)),
        out_shape=jax.ShapeDtypeStruct((m, width), bf16),
        compiler_params=_params(("parallel", "parallel")),
        name="mem_attention",
    )(u, *kv_args)


def _seg64_sum(x, ones_ref):
    hi = x.astype(bf16)
    lo = (x - hi.astype(f32)).astype(bf16)
    cols = []
    for j in range(x.shape[1] // LANES):
        js = slice(j * LANES, (j + 1) * LANES)
        cols.append(_dot(hi[:, js], ones_ref[...]) + _dot(lo[:, js], ones_ref[...]))
    return jnp.concatenate(cols, axis=-1)


def _rwkv_features(xs, prevs, w_refs, ones_ref):
    mur_ref, muk_ref, muv_ref, mulo_ref, w0_ref, a0_ref, wl_ref, kk_ref, ka_ref, rk_ref = w_refs

    def token_shift(x, prev, mu_ref):
        row = lax.broadcasted_iota(jnp.int32, x.shape, 0)
        shifted = jnp.where(row == 0, prev, pltpu.roll(x, 1, 0))
        return x + (shifted - x) * mu_ref[...]

    r, k0, v, lo = (token_shift(x, pv, mu) for x, pv, mu in zip(xs, prevs, (mur_ref, muk_ref, muv_ref, mulo_ref)))
    col = lax.broadcasted_iota(jnp.int32, lo.shape, 1)
    act = jnp.where(col < 64, jnp.tanh(lo), jnp.where(col < 128, lo, _sigmoid(lo)))
    proj = _dot(act.astype(bf16), wl_ref[...])
    log_w = -_softplus(-(w0_ref[...] + proj[:, :BRANCH_WIDTH])) - 0.5
    log_decay = -jnp.exp(log_w)
    a = _sigmoid(a0_ref[...] + proj[:, BRANCH_WIDTH:2 * BRANCH_WIDTH])
    g = proj[:, 2 * BRANCH_WIDTH:]
    kk = k0 * kk_ref[...]
    kk = kk / jnp.maximum(jnp.sqrt(_seg64_sum(kk * kk, ones_ref)), 1e-12)
    k = k0 * (1.0 + (a - 1.0) * ka_ref[...])
    bonus = _seg64_sum(r * k * rk_ref[...], ones_ref) * v
    return r, k, v, kk, a, log_decay, g, bonus


def _rwkv_output(y, bonus, g, lg_ref, lb_ref, ones_ref):
    mu = _seg64_sum(y, ones_ref) * (1.0 / RWKV_HEAD)
    d = y - mu
    var = _seg64_sum(d * d, ones_ref) * (1.0 / RWKV_HEAD)
    yn = d * lax.rsqrt(var + 64e-5) * lg_ref[...] + lb_ref[...]
    return ((yn + bonus) * g).astype(bf16)


def _rwkv_seq_kernel(*refs, tb, steps, reqs):
    x_refs, p_refs, w_refs = refs[0:4], refs[4:8], refs[8:18]
    lg_ref, lb_ref, ones_ref, s0_ref, o_ref, s_ref = refs[18:24]
    w3, kk3, kka3, k3, r3, v3, y3, bonus_s, g_s = refs[24:]
    s_ref[...] = s0_ref[...]
    y3[...] = jnp.zeros_like(y3)

    for q in range(reqs):
        rows = slice(q * tb, (q + 1) * tb)
        xs = [x[rows, :] for x in x_refs]
        prevs = [p[q] for p in p_refs]
        r, k, v, kk, a, log_decay, g, bonus = _rwkv_features(xs, prevs, w_refs, ones_ref)
        decay = jnp.exp(log_decay)
        kka = kk * a
        g_s[rows, :] = g
        bonus_s[rows, :] = bonus
        for h in range(RWKV_HEADS):
            hs = slice(h * RWKV_HEAD, (h + 1) * RWKV_HEAD)
            j = q * RWKV_HEADS + h
            w3[j] = decay[:, hs]
            kk3[j] = kk[:, hs]
            kka3[j] = kka[:, hs]
            k3[j] = k[:, hs]
            r3[j] = r[:, hs]
            v3[j] = v[:, hs]

    eye = (lax.broadcasted_iota(jnp.int32, (RWKV_HEAD, RWKV_HEAD), 0)
           == lax.broadcasted_iota(jnp.int32, (RWKV_HEAD, RWKV_HEAD), 1)).astype(f32)

    def step(t, carry):
        ts = pl.ds(t, 1)
        for q in range(reqs):
            for h in range(RWKV_HEADS):
                j = q * RWKV_HEADS + h
                s_old = s_ref[q, h]
                sa = jnp.sum(s_old * kk3[j, ts, :], axis=1, keepdims=True)
                v_col = jnp.sum(eye * v3[j, ts, :], axis=1, keepdims=True)
                s_new = s_old * w3[j, ts, :] - sa * kka3[j, ts, :] + v_col * k3[j, ts, :]
                s_ref[q, h] = s_new
                y_col = jnp.sum(s_new * r3[j, ts, :], axis=1, keepdims=True)
                y3[j, ts, :] = jnp.sum(eye * y_col, axis=0, keepdims=True)
        return carry

    lax.fori_loop(0, steps, step, 0)

    for q in range(reqs):
        rows = slice(q * tb, (q + 1) * tb)
        y = jnp.concatenate([y3[q * RWKV_HEADS + h] for h in range(RWKV_HEADS)], axis=-1)
        o_ref[rows, :] = _rwkv_output(y, bonus_s[rows, :], g_s[rows, :], lg_ref, lb_ref, ones_ref)


def _unit_lower_inverse_minus_identity(ns):
    size = ns[0].shape[0]
    t = lax.broadcasted_iota(jnp.int32, (size, size), 0)
    s = lax.broadcasted_iota(jnp.int32, (size, size), 1)
    first = ((t >> 1) == (s >> 1)) & (t > s)
    es = [-jnp.where(first, n, 0.0) for n in ns]
    blk, shift = 4, 2
    while blk <= RWKV_HEAD:
        half = blk // 2
        sel = ((t >> shift) == (s >> shift)) & ((t & (blk - 1)) >= half) & ((s & (blk - 1)) < half)
        cs = [jnp.where(sel, n, 0.0) for n in ns]
        zs = [c + _dot(c.astype(bf16), e.astype(bf16)) for c, e in zip(cs, es)]
        es = [e - z - _dot(e.astype(bf16), z.astype(bf16)) for e, z in zip(es, zs)]
        blk, shift = blk * 2, shift + 1
    return es


def _rwkv_chunk_kernel(*refs, tb):
    x_refs, p_refs, w_refs = refs[0:4], refs[4:8], refs[8:18]
    lg_ref, lb_ref, ones_ref, s0_ref, o_ref, s_ref = refs[18:24]
    carries = refs[24:28]
    sp = refs[28]
    i = pl.program_id(1)
    n_pairs = RWKV_HEADS // 2
    hd = RWKV_HEAD

    @pl.when(i == 0)
    def _():
        zero = jnp.zeros((hd, hd), f32)
        for p in range(n_pairs):
            top = jnp.concatenate([s0_ref[0, 2 * p], zero], axis=1)
            bot = jnp.concatenate([zero, s0_ref[0, 2 * p + 1]], axis=1)
            sp[p] = jnp.concatenate([top, bot], axis=0)
        for carry, p_ref in zip(carries, p_refs):
            carry[...] = p_ref[0]

    xs = [x[...] for x in x_refs]
    prevs = [carry[...] for carry in carries]
    for carry, x in zip(carries, xs):
        carry[...] = x[tb - 1:tb, :]
    r, k, v, kk, a, lw, g, bonus = _rwkv_features(xs, prevs, w_refs, ones_ref)
    beta = kk * a
    row = lax.broadcasted_iota(jnp.int32, lw.shape, 0)
    cum = lw
    d = 1
    while d < tb:
        cum = cum + jnp.where(row >= d, pltpu.roll(cum, d, 0), 0.0)
        d *= 2
    c_last = cum[tb - 1:tb, :]
    a_t = kk * jnp.exp(cum - lw)
    r_t = r * jnp.exp(cum)
    e_neg = jnp.exp(-cum)
    k_t = k * e_neg
    b_t = beta * e_neg
    e_hat = jnp.exp(c_last - cum)
    k_h = k * e_hat
    b_h = beta * e_hat
    gamma = jnp.exp(c_last)

    lane = lax.broadcasted_iota(jnp.int32, (tb, LANES), 1)
    head0 = lane < hd
    split = lambda x: (jnp.where(head0, x, 0.0), jnp.where(head0, 0.0, x))
    tt = lax.broadcasted_iota(jnp.int32, (LANES, LANES), 0)
    ss = lax.broadcasted_iota(jnp.int32, (LANES, LANES), 1)
    strict = tt > ss
    incl = tt >= ss
    same_head = (tt >= hd) == (ss >= hd)
    fold = lambda m: m[:tb, :] + m[tb:, :]

    pairs = range(n_pairs)
    cols = [slice(p * LANES, (p + 1) * LANES) for p in pairs]
    stack = lambda *xs: jnp.concatenate(xs, axis=0)
    scs = [_dot_nt(stack(*split(a_t[:, ps]), *split(r_t[:, ps])).astype(bf16),
                   stack(*split(k_t[:, ps]), *split(b_t[:, ps])).astype(bf16)) for ps in cols]
    es = _unit_lower_inverse_minus_identity([jnp.where(strict, sc[:2 * tb, 2 * tb:], 0.0) for sc in scs])
    eye = jnp.where(tt == ss, 1.0, 0.0)
    t_cats = [fold(eye + e).astype(bf16) for e in es]
    ak_cats = [fold(jnp.where(strict, sc[:2 * tb, :2 * tb], 0.0)).astype(bf16) for sc in scs]
    r_cats = [jnp.concatenate([fold(jnp.where(incl, sc[2 * tb:, :2 * tb], 0.0)),
                               -fold(jnp.where(incl, sc[2 * tb:, 2 * tb:], 0.0))], axis=1).astype(bf16)
              for sc in scs]
    s_olds = [sp[p] for p in pairs]
    grs = [_dot_nt(stack(a_t[:, ps], r_t[:, ps]).astype(bf16), s_old.astype(bf16))
           for ps, s_old in zip(cols, s_olds)]
    v_sts = [stack(*split(v[:, ps])).astype(bf16) for ps in cols]
    u_rhss = [gr[:tb] + _dot(ak, v_st) for gr, ak, v_st in zip(grs, ak_cats, v_sts)]
    us = [_dot(t_cat, stack(*split(u_rhs)).astype(bf16)) for t_cat, u_rhs in zip(t_cats, u_rhss)]
    ys = [gr[tb:] + _dot(r_cat, stack(v_st, stack(*split(u)).astype(bf16)))
          for gr, r_cat, v_st, u in zip(grs, r_cats, v_sts, us)]
    for p, ps, s_old, u in zip(pairs, cols, s_olds, us):
        vu_t = stack(v[:, ps], -u).T.astype(bf16)
        kb = stack(k_h[:, ps], b_h[:, ps]).astype(bf16)
        sp[p] = s_old * gamma[:, ps] + jnp.where(same_head, _dot(vu_t, kb), 0.0)

    o_ref[...] = _rwkv_output(jnp.concatenate(ys, axis=-1), bonus, g, lg_ref, lb_ref, ones_ref)

    @pl.when(i == pl.num_programs(1) - 1)
    def _():
        for p in range(n_pairs):
            full = sp[p]
            s_ref[0, 2 * p] = full[:hd, :hd]
            s_ref[0, 2 * p + 1] = full[hd:, hd:]


def rwkv(u, prev, p, s0, layer, ones2, batch, l_pad, l_real, chunked, reqs=1):
    m = u.shape[0]
    bw = BRANCH_WIDTH
    full = lambda w: pl.BlockSpec((1, w), lambda b, i: (0, 0))
    if chunked:
        tb, nb, nblk = RWKV_HEAD, 1, l_pad // RWKV_HEAD
        assert l_real == l_pad and l_pad % tb == 0
        body = functools.partial(_rwkv_chunk_kernel, tb=tb)
        scratch = [pltpu.VMEM((1, bw), f32), pltpu.VMEM((1, bw), f32), pltpu.VMEM((1, bw), f32),
                   pltpu.VMEM((1, RWKV_LORA), f32), pltpu.VMEM((RWKV_HEADS // 2, LANES, LANES), f32)]
    else:
        tb, nb, nblk = l_pad, reqs, 1
        assert batch % reqs == 0
        body = functools.partial(_rwkv_seq_kernel, tb=tb, steps=l_real, reqs=reqs)
        head_buf = pltpu.VMEM((reqs * RWKV_HEADS, tb, RWKV_HEAD), f32)
        scratch = [head_buf] * 7 + [pltpu.VMEM((reqs * tb, bw), f32), pltpu.VMEM((reqs * tb, bw), f32)]
    rows = nb * tb
    row = lambda b, i: b * nblk + i
    prev_spec = lambda w: pl.BlockSpec((None, nb, 1, w), lambda b, i: (layer, b, 0, 0))
    return pl.pallas_call(
        body,
        grid=(batch // nb, nblk),
        in_specs=[pl.BlockSpec((rows, bw), lambda b, i: (row(b, i), C_R // bw)),
                  pl.BlockSpec((rows, bw), lambda b, i: (row(b, i), C_K // bw)),
                  pl.BlockSpec((rows, bw), lambda b, i: (row(b, i), C_V // bw)),
                  pl.BlockSpec((rows, RWKV_LORA), lambda b, i: (row(b, i), C_LORA // RWKV_LORA)),
                  prev_spec(bw), prev_spec(bw), prev_spec(bw), prev_spec(RWKV_LORA),
                  full(bw), full(bw), full(bw), full(RWKV_LORA),
                  full(bw), full(bw),
                  pl.BlockSpec((RWKV_LORA, 3 * bw), lambda b, i: (0, 0)),
                  full(bw), full(bw), full(bw), full(bw), full(bw),
                  pl.BlockSpec((LANES, LANES), lambda b, i: (0, 0)),
                  pl.BlockSpec((None, nb, RWKV_HEADS, RWKV_HEAD, RWKV_HEAD), lambda b, i: (layer, b, 0, 0, 0))],
        out_specs=[pl.BlockSpec((rows, bw), lambda b, i: (row(b, i), 0)),
                   pl.BlockSpec((nb, RWKV_HEADS, RWKV_HEAD, RWKV_HEAD), lambda b, i: (b, 0, 0, 0))],
        out_shape=[jax.ShapeDtypeStruct((m, bw), bf16),
                   jax.ShapeDtypeStruct((batch, RWKV_HEADS, RWKV_HEAD, RWKV_HEAD), f32)],
        scratch_shapes=scratch,
        compiler_params=_params(("parallel", "arbitrary")),
        name="rwkv7_chunked" if chunked else "rwkv7_seq",
    )(u, u, u, u, *prev, p['mu_r'], p['mu_k'], p['mu_v'], p['mu_lo'], p['w0'], p['a0'], p['w_lora'],
      p['k_k'], p['k_a'], p['r_k'], p['ln_g'], p['ln_b'], ones2, s0)


def _pack_w_in_t(w_in):
    wt = jnp.swapaxes(w_in, 1, 2)
    seg = lambda o, n: wt[:, o:o + n]
    parts = [seg(_O_GV, 1024), seg(_O_GR, 1024), seg(_O_SQ, 1024), seg(_O_MQ, 1024),
             seg(_O_RU, 3072), seg(_O_GPRE, 8192), seg(_O_GQ, 512), seg(_O_GK, 512),
             seg(_O_RU + 3072, RWKV_LORA), seg(_O_SK, 128), seg(_O_SV, 128), seg(_O_GA, GLA_GATE_RANK),
             jnp.zeros((wt.shape[0], N_PACK - C_GA - GLA_GATE_RANK, wt.shape[2]), wt.dtype)]
    return jnp.concatenate(parts, axis=1).astype(bf16)


def _layer_params(l, w):
    row = lambda x: x[l].reshape(1, -1)
    mu = w['rwkv_mu'][l]
    z = lambda r, c: jnp.zeros((r, c), f32)
    w_lora = jnp.concatenate([
        jnp.concatenate([w['rwkv_w2'][l], z(64, 1024), z(64, 1024)], axis=1),
        jnp.concatenate([z(64, 1024), w['rwkv_a2'][l], z(64, 1024)], axis=1),
        jnp.concatenate([z(128, 1024), z(128, 1024), w['rwkv_g2'][l]], axis=1)], axis=0).astype(bf16)
    a_up = jnp.concatenate([w['gla_a_up'][l], z(LANES - GLA_GATE_RANK, GLA_QK)], axis=0).astype(bf16)
    return {
        'a_up': a_up, 'a_b': row(w['gla_a_b']),
        'gla_g': row(w['gla_norm_g']), 'gla_b': row(w['gla_norm_b']),
        'sinks': w['swa_sinks'][l],
        'rwkv': {'mu_r': mu[:1024].reshape(1, -1), 'mu_k': mu[1024:2048].reshape(1, -1),
                 'mu_v': mu[2048:3072].reshape(1, -1), 'mu_lo': mu[3072:].reshape(1, -1),
                 'w0': row(w['rwkv_w0']), 'a0': row(w['rwkv_a0']), 'w_lora': w_lora,
                 'k_k': row(w['rwkv_k_k']), 'k_a': row(w['rwkv_k_a']), 'r_k': row(w['rwkv_r_k']),
                 'ln_g': row(w['rwkv_ln_g']), 'ln_b': row(w['rwkv_ln_b'])},
        'ln1_g': row(w['ln1_g']), 'ln1_b': row(w['ln1_b']),
        'ln2_g': row(w['ln2_g']), 'ln2_b': row(w['ln2_b']),
    }


def _tiles(m):
    if m >= 1024:
        return {'proj': 1024, 'merge': 512, 'ffn': 1024, 'ln': 512}
    return {'proj': m, 'merge': m, 'ffn': m, 'ln': m}


def _trunk_layer(h_f, h_b, l, p, big, ones2, mem_kv, mem_cache, gla_s0, rwkv_s0, rwkv_prev, swa_buf,
                 batch, l_pad, l_real):
    m = h_f.shape[0]
    tl = _tiles(m)
    u = matmul(h_b, big['w_in_t'], l, tl['proj'], 1024, "in_proj", trans_w=True)
    o_a, gla_s = gla(u, p['a_up'], p['a_b'], p['gla_g'], p['gla_b'], gla_s0[0], gla_s0[1], batch, l_pad, l_real,
                     min(GLA_CHUNK, l_pad))
    if swa_buf is None:
        o_b = swa_prompt(u, p['sinks'], batch, l_pad)
    else:
        o_b = swa_sample(u, swa_buf[0], swa_buf[1], l, p['sinks'], batch, l_pad, l_real, 4)
    chunked = l_real == l_pad and l_pad % RWKV_HEAD == 0
    o_c, rwkv_s = rwkv(u, rwkv_prev[0], p['rwkv'], rwkv_s0[0], rwkv_s0[1], ones2, batch, l_pad, l_real,
                       chunked, reqs=1 if chunked else 4)
    o_m = mem_attention(u, mem_kv, mem_cache, l, batch, l_pad, min(512, l_pad))
    merged = gated_merge((o_a, o_b, o_c, o_m), u, big['gate_b'], big['w_branch'], l, tl['merge'], 512)
    x_f, x_b = matmul_residual_ln(merged, big['w_out'], l, h_f, p['ln1_g'], p['ln1_b'], tl['ln'], D_MODEL)
    act = ffn_up(x_b, big['w_gu'], l, tl['ffn'], 512)
    y_f, y_b = matmul_residual_ln(act, big['w_down'], l, x_f, p['ln2_g'], p['ln2_b'], tl['ln'], D_FF // 4)
    return y_f, y_b, u, gla_s, rwkv_s


def _split_ru(x):
    return (x[..., :1024], x[..., 1024:2048], x[..., 2048:3072], x[..., 3072:])


def kernel(x_prompt, x_sample, mem_prompt, cache_swa_k, cache_swa_v, cache_mem_k, cache_mem_v, state_gla, state_rwkv, state_rwkv_shift, w_in, gate_b, gla_a_up, gla_a_b, gla_norm_g, gla_norm_b, swa_sinks, rwkv_mu, rwkv_w0, rwkv_w2, rwkv_a0, rwkv_a2, rwkv_g2, rwkv_k_k, rwkv_k_a, rwkv_r_k, rwkv_ln_g, rwkv_ln_b, w_mem_kv, w_branch, w_out, ln1_g, ln1_b, w_gu, w_down, ln2_g, ln2_b):
    weights = {'gla_a_up': gla_a_up, 'gla_a_b': gla_a_b,
               'gla_norm_g': gla_norm_g, 'gla_norm_b': gla_norm_b, 'swa_sinks': swa_sinks,
               'rwkv_mu': rwkv_mu, 'rwkv_w0': rwkv_w0, 'rwkv_w2': rwkv_w2, 'rwkv_a0': rwkv_a0,
               'rwkv_a2': rwkv_a2, 'rwkv_g2': rwkv_g2, 'rwkv_k_k': rwkv_k_k, 'rwkv_k_a': rwkv_k_a,
               'rwkv_r_k': rwkv_r_k, 'rwkv_ln_g': rwkv_ln_g, 'rwkv_ln_b': rwkv_ln_b,
               'ln1_g': ln1_g, 'ln1_b': ln1_b, 'ln2_g': ln2_g, 'ln2_b': ln2_b}
    big = {'w_in_t': _pack_w_in_t(w_in), 'gate_b': gate_b, 'w_branch': w_branch.astype(bf16),
           'w_out': w_out.astype(bf16), 'w_gu': w_gu.astype(bf16), 'w_down': w_down.astype(bf16)}
    w_mem_kv_b = w_mem_kv.astype(bf16)
    bp, lp, _ = x_prompt.shape
    bs, ls, _ = x_sample.shape
    ls_pad = -(-ls // SUBLANES) * SUBLANES
    mp, ms = bp * lp, bs * ls_pad

    hp_f = x_prompt.reshape(mp, D_MODEL)
    hs_f = jnp.pad(x_sample, ((0, 0), (0, ls_pad - ls), (0, 0))).reshape(ms, D_MODEL)
    hp_b, hs_b = hp_f.astype(bf16), hs_f.astype(bf16)
    mem_b = mem_prompt.reshape(bp * MEM_TOKENS, D_MODEL).astype(bf16)
    half = LANES // 2
    blk = jnp.ones((half, half), f32)
    zero = jnp.zeros((half, half), f32)
    ones2 = jnp.concatenate([jnp.concatenate([blk, zero], 1), jnp.concatenate([zero, blk], 1)], 0).astype(bf16)

    gla0_p = jnp.zeros((1, bp, GLA_HEADS, GLA_DK, GLA_DV), f32)
    rwkv0_p = jnp.zeros((1, bp, RWKV_HEADS, RWKV_HEAD, RWKV_HEAD), f32)
    prev0_p = _split_ru(jnp.zeros((1, bp, 1, RWKV_COLS), f32))
    prev_s = _split_ru(state_rwkv_shift)
    kvw = SWA_KV_HEADS * SWA_HEAD_DIM
    kbuf = cache_swa_k.reshape(DEPTH, bs * WINDOW, kvw)
    vbuf = cache_swa_v.reshape(DEPTH, bs * WINDOW, kvw)

    outs = {k: [] for k in ('p_swk', 'p_swv', 'p_mk', 'p_mv', 'p_gla', 'p_rw', 'p_rs',
                            's_swk', 's_swv', 's_gla', 's_rw', 's_rs')}
    for l in range(DEPTH):
        p = _layer_params(l, weights)
        kv = matmul(mem_b, w_mem_kv_b, l, bp * MEM_TOKENS, 512, "mem_kv")
        hp_f, hp_b, u, gs, rs = _trunk_layer(hp_f, hp_b, l, p, big, ones2, kv, None, (gla0_p, 0), (rwkv0_p, 0),
                                             (prev0_p, 0), None, bp, lp, lp)
        u3 = u.reshape(bp, lp, N_PACK)
        outs['p_swk'].append(u3[:, lp - WINDOW:, C_SK:C_SK + kvw].reshape(bp, WINDOW, SWA_KV_HEADS, SWA_HEAD_DIM))
        outs['p_swv'].append(u3[:, lp - WINDOW:, C_SV:C_SV + kvw].reshape(bp, WINDOW, SWA_KV_HEADS, SWA_HEAD_DIM))
        outs['p_mk'].append(kv[:, :1024].reshape(bp, MEM_TOKENS, MEM_HEADS, MEM_HEAD_DIM))
        outs['p_mv'].append(kv[:, 1024:].reshape(bp, MEM_TOKENS, MEM_HEADS, MEM_HEAD_DIM))
        outs['p_gla'].append(gs)
        outs['p_rw'].append(rs)
        outs['p_rs'].append(jnp.concatenate([u3[:, lp - 1:, C_R:C_R + 3072],
                                             u3[:, lp - 1:, C_LORA:C_LORA + RWKV_LORA]], axis=-1))
        hs_f, hs_b, u, gs, rs = _trunk_layer(hs_f, hs_b, l, p, big, ones2, None, (cache_mem_k, cache_mem_v),
                                             (state_gla, l), (state_rwkv, l), (prev_s, l), (kbuf, vbuf),
                                             bs, ls_pad, ls)
        u3 = u.reshape(bs, ls_pad, N_PACK)
        k_new = u3[:, :ls, C_SK:C_SK + kvw].reshape(bs, ls, SWA_KV_HEADS, SWA_HEAD_DIM)
        v_new = u3[:, :ls, C_SV:C_SV + kvw].reshape(bs, ls, SWA_KV_HEADS, SWA_HEAD_DIM)
        outs['s_swk'].append(jnp.concatenate([cache_swa_k[l][:, ls:], k_new], axis=1))
        outs['s_swv'].append(jnp.concatenate([cache_swa_v[l][:, ls:], v_new], axis=1))
        outs['s_gla'].append(gs)
        outs['s_rw'].append(rs)
        outs['s_rs'].append(jnp.concatenate([u3[:, ls - 1:ls, C_R:C_R + 3072],
                                             u3[:, ls - 1:ls, C_LORA:C_LORA + RWKV_LORA]], axis=-1))

    st = {k: jnp.stack(v) for k, v in outs.items()}
    y_prompt = hp_f.reshape(bp, lp, D_MODEL)
    y_sample = hs_f.reshape(bs, ls_pad, D_MODEL)[:, :ls]
    return (y_prompt, y_sample,
            st['p_swk'], st['p_swv'], st['p_mk'], st['p_mv'], st['p_gla'], st['p_rw'], st['p_rs'],
            st['s_swk'], st['s_swv'], st['s_gla'], st['s_rw'], st['s_rs'])
```

```python
import functools

import jax
import jax.numpy as jnp
from jax import lax
from jax.experimental import pallas as pl
from jax.experimental.pallas import tpu as pltpu

f32 = jnp.float32
bf16 = jnp.bfloat16

D_MODEL = 2048
DEPTH = 4
BRANCH_WIDTH = 1024
N_BRANCH = 4
GLA_HEADS = 4
GLA_QK = 512
GLA_V = 1024
GLA_DK = 128
GLA_DV = 256
GLA_GATE_RANK = 16
GLA_TAU = 16.0
GLA_CHUNK = 64
SWA_HEAD_DIM = 64
SWA_Q_HEADS = 16
SWA_KV_HEADS = 2
SWA_GROUP = SWA_Q_HEADS // SWA_KV_HEADS
WINDOW = 128
RWKV_HEAD = 64
RWKV_HEADS = 16
RWKV_LORA = 256
RWKV_COLS = 3 * BRANCH_WIDTH + RWKV_LORA
MEM_TOKENS = 256
MEM_HEADS = 4
MEM_HEAD_DIM = 256
D_FF = 5632
DEEPNORM_ALPHA = (2 * DEPTH) ** 0.25
NEG = -1e30

_O_GQ, _O_GK, _O_GV, _O_GR, _O_GA = 0, 512, 1024, 2048, 3072
_O_SQ, _O_SK, _O_SV = 3088, 4112, 4240
_O_RU = 4368
_O_MQ = 7696
_O_GPRE = 8720
_N_IN = 16912

C_GV, C_GR, C_SQ, C_MQ, C_R, C_K, C_V = 0, 1024, 2048, 3072, 4096, 5120, 6144
C_GPRE = 7168
C_GQ, C_GK = 15360, 15872
C_LORA = 16384
C_SK, C_SV, C_GA = 16640, 16768, 16896
N_PACK = 17408

LANES = 128
SUBLANES = 8
VMEM_BYTES_V7X = 64 * 1024 * 1024
VMEM_LIMIT = VMEM_BYTES_V7X - 8 * 1024 * 1024


def _params(sem):
    return pltpu.CompilerParams(dimension_semantics=sem, vmem_limit_bytes=VMEM_LIMIT)


def _softplus(z):
    return jnp.maximum(z, 0.0) + jnp.log(1.0 + jnp.exp(-jnp.abs(z)))


def _sigmoid(z):
    return 1.0 / (1.0 + jnp.exp(-z))


def _dot(a, b):
    return jnp.dot(a, b, preferred_element_type=f32)


def _dot_nt(a, b):
    return lax.dot_general(a, b, (((1,), (1,)), ((), ())), preferred_element_type=f32)


def _mm_kernel(x_ref, w_ref, o_ref, *, trans_w):
    mm = _dot_nt if trans_w else _dot
    o_ref[...] = mm(x_ref[...], w_ref[...])


def matmul(x, w, layer, tm, tn, name, trans_w=False):
    m, k = x.shape
    n = w.shape[1] if trans_w else w.shape[2]
    w_spec = (pl.BlockSpec((None, tn, k), lambda i, j: (layer, j, 0)) if trans_w
              else pl.BlockSpec((None, k, tn), lambda i, j: (layer, 0, j)))
    return pl.pallas_call(
        functools.partial(_mm_kernel, trans_w=trans_w),
        grid=(m // tm, n // tn),
        in_specs=[pl.BlockSpec((tm, k), lambda i, j: (i, 0)), w_spec],
        out_specs=pl.BlockSpec((tm, tn), lambda i, j: (i, j)),
        out_shape=jax.ShapeDtypeStruct((m, n), f32),
        compiler_params=_params(("parallel", "parallel")),
        name=name,
    )(x, w)


def _mm_ln_kernel(x_ref, w_ref, res_ref, g_ref, b_ref, of_ref, ob_ref, *, nk):
    k = pl.program_id(1)
    part = _dot(x_ref[...], w_ref[...])

    @pl.when(k == 0)
    def _():
        of_ref[...] = part

    @pl.when(k > 0)
    def _():
        of_ref[...] += part

    @pl.when(k == nk - 1)
    def _():
        z = DEEPNORM_ALPHA * res_ref[...] + of_ref[...]
        mu = jnp.mean(z, axis=-1, keepdims=True)
        d = z - mu
        var = jnp.mean(d * d, axis=-1, keepdims=True)
        y = d * lax.rsqrt(var + 1e-5) * g_ref[...] + b_ref[...]
        of_ref[...] = y
        ob_ref[...] = y.astype(bf16)


def matmul_residual_ln(x, w, layer, res, g, b, tm, tk):
    m, k = x.shape
    n = w.shape[2]
    nk = k // tk
    return pl.pallas_call(
        functools.partial(_mm_ln_kernel, nk=nk),
        grid=(m // tm, nk),
        in_specs=[pl.BlockSpec((tm, tk), lambda i, kk: (i, kk)),
                  pl.BlockSpec((None, tk, n), lambda i, kk: (layer, kk, 0)),
                  pl.BlockSpec((tm, n), lambda i, kk: (i, 0)),
                  pl.BlockSpec((1, n), lambda i, kk: (0, 0)),
                  pl.BlockSpec((1, n), lambda i, kk: (0, 0))],
        out_specs=[pl.BlockSpec((tm, n), lambda i, kk: (i, 0)),
                   pl.BlockSpec((tm, n), lambda i, kk: (i, 0))],
        out_shape=[jax.ShapeDtypeStruct((m, n), f32), jax.ShapeDtypeStruct((m, n), bf16)],
        compiler_params=_params(("parallel", "arbitrary")),
        name="proj_ln",
    )(x, w, res, g, b)


def _merge_kernel(a_ref, b_ref, c_ref, m_ref, g0_ref, g1_ref, g2_ref, g3_ref, gb_ref, w_ref, o_ref, w_bf):
    @pl.when(pl.program_id(1) == 0)
    def _():
        w_bf[...] = w_ref[...].astype(bf16)

    acc = None
    for n, (br, gp) in enumerate(((a_ref, g0_ref), (b_ref, g1_ref), (c_ref, g2_ref), (m_ref, g3_ref))):
        y = _dot(br[...], w_bf[n])
        gate = _sigmoid(gp[...] + gb_ref[n:n + 1, :])
        acc = gate * y if acc is None else acc + gate * y
    o_ref[...] = acc.astype(bf16)


def gated_merge(branches, u, gate_b, w_branch, layer, tm, tn):
    m = u.shape[0]
    gp0 = C_GPRE // tn
    per = D_MODEL // tn
    br_spec = pl.BlockSpec((tm, BRANCH_WIDTH), lambda j, i: (i, 0))
    gp_specs = [pl.BlockSpec((tm, tn), functools.partial(lambda j, i, n: (i, gp0 + n * per + j), n=n))
                for n in range(N_BRANCH)]
    return pl.pallas_call(
        _merge_kernel,
        grid=(D_MODEL // tn, m // tm),
        in_specs=[br_spec] * 4 + gp_specs + [
            pl.BlockSpec((None, N_BRANCH, tn), lambda j, i: (layer, 0, j)),
            pl.BlockSpec((None, N_BRANCH, BRANCH_WIDTH, tn), lambda j, i: (layer, 0, 0, j))],
        out_specs=pl.BlockSpec((tm, tn), lambda j, i: (i, j)),
        out_shape=jax.ShapeDtypeStruct((m, D_MODEL), bf16),
        scratch_shapes=[pltpu.VMEM((N_BRANCH, BRANCH_WIDTH, tn), bf16)],
        compiler_params=_params(("parallel", "arbitrary")),
        name="gated_merge",
    )(*branches, u, u, u, u, gate_b, w_branch)


def _ffn_up_kernel(x_ref, wg_ref, wu_ref, o_ref, wg_bf, wu_bf):
    @pl.when(pl.program_id(1) == 0)
    def _():
        wg_bf[...] = wg_ref[...].astype(bf16)
        wu_bf[...] = wu_ref[...].astype(bf16)

    x = x_ref[...]
    g = _dot(x, wg_bf[...])
    up = _dot(x, wu_bf[...])
    o_ref[...] = (g * _sigmoid(g) * up).astype(bf16)


def ffn_up(x, w_gu, layer, tm, tn):
    m, k = x.shape
    nj = D_FF // tn
    return pl.pallas_call(
        _ffn_up_kernel,
        grid=(nj, m // tm),
        in_specs=[pl.BlockSpec((tm, k), lambda j, i: (i, 0)),
                  pl.BlockSpec((None, k, tn), lambda j, i: (layer, 0, j)),
                  pl.BlockSpec((None, k, tn), lambda j, i: (layer, 0, nj + j))],
        out_specs=pl.BlockSpec((tm, tn), lambda j, i: (i, j)),
        out_shape=jax.ShapeDtypeStruct((m, D_FF), bf16),
        scratch_shapes=[pltpu.VMEM((k, tn), bf16), pltpu.VMEM((k, tn), bf16)],
        compiler_params=_params(("parallel", "arbitrary")),
        name="ffn_up",
    )(x, w_gu, w_gu)


def _gla_kernel(q_ref, k_ref, v_ref, r_ref, a_ref, aup_ref, ab_ref, ng_ref, nb_ref, s0_ref,
                o_ref, s_ref, *, chunk, l_real, l_pad):
    c = pl.program_id(1)

    @pl.when(c == 0)
    def _():
        s_ref[...] = s0_ref[...]

    la = _dot(a_ref[...].astype(bf16), aup_ref[...]) + ab_ref[...]
    la = (jnp.minimum(la, 0.0) - jnp.log(1.0 + jnp.exp(-jnp.abs(la)))) * (1.0 / GLA_TAU)
    row = lax.broadcasted_iota(jnp.int32, (chunk, GLA_QK), 0)
    kin = k_ref[...]
    if l_real < l_pad:
        real = (c * chunk + row) < l_real
        la = jnp.where(real, la, 0.0)
        kin = jnp.where(real, kin, 0.0)
    b = la
    d = 1
    while d < chunk:
        b = b + jnp.where(row >= d, pltpu.roll(b, d, 0), 0.0)
        d *= 2
    b_last = b[chunk - 1:chunk, :]
    q_dec = q_ref[...] * (GLA_DK ** -0.5) * jnp.exp(b)
    k_inv = kin * jnp.exp(-b)
    k_dec = kin * jnp.exp(b_last - b)
    tt = lax.broadcasted_iota(jnp.int32, (chunk, chunk), 0)
    ss = lax.broadcasted_iota(jnp.int32, (chunk, chunk), 1)
    causal = tt >= ss
    heads = range(GLA_HEADS)
    kss = [slice(h * GLA_DK, (h + 1) * GLA_DK) for h in heads]
    vss = [slice(h * GLA_DV, (h + 1) * GLA_DV) for h in heads]
    qhs = [q_dec[:, ks].astype(bf16) for ks in kss]
    atts = [jnp.where(causal, _dot_nt(qh, k_inv[:, ks].astype(bf16)), 0.0).astype(bf16) for qh, ks in zip(qhs, kss)]
    vhs = [v_ref[:, vs].astype(bf16) for vs in vss]
    s_olds = [s_ref[0, h] for h in heads]
    os_ = [_dot(att, vh) + _dot(qh, s_old.astype(bf16)) for att, vh, qh, s_old in zip(atts, vhs, qhs, s_olds)]
    decay_cols = [jnp.exp(jnp.sum(la[:, ks].T, axis=1, keepdims=True)) for ks in kss]
    for h, ks, vh, s_old, decay_col in zip(heads, kss, vhs, s_olds, decay_cols):
        s_ref[0, h] = decay_col * s_old + _dot(k_dec[:, ks].T.astype(bf16), vh)
    for vs, o in zip(vss, os_):
        mu = jnp.mean(o, axis=-1, keepdims=True)
        dd = o - mu
        var = jnp.mean(dd * dd, axis=-1, keepdims=True)
        nrm = dd * lax.rsqrt(var + 1e-5) * ng_ref[:, vs] + nb_ref[:, vs]
        gr = r_ref[:, vs]
        o_ref[:, vs] = (nrm * (gr * _sigmoid(gr))).astype(bf16)


def gla(u, a_up, a_b, norm_g, norm_b, s0, layer, batch, l_pad, l_real, chunk):
    m = u.shape[0]
    nc = l_pad // chunk
    row = lambda b, c: b * nc + c
    return pl.pallas_call(
        functools.partial(_gla_kernel, chunk=chunk, l_real=l_real, l_pad=l_pad),
        grid=(batch, nc),
        in_specs=[pl.BlockSpec((chunk, GLA_QK), lambda b, c: (row(b, c), C_GQ // GLA_QK)),
                  pl.BlockSpec((chunk, GLA_QK), lambda b, c: (row(b, c), C_GK // GLA_QK)),
                  pl.BlockSpec((chunk, GLA_V), lambda b, c: (row(b, c), C_GV // GLA_V)),
                  pl.BlockSpec((chunk, GLA_V), lambda b, c: (row(b, c), C_GR // GLA_V)),
                  pl.BlockSpec((chunk, LANES), lambda b, c: (row(b, c), C_GA // LANES)),
                  pl.BlockSpec((LANES, GLA_QK), lambda b, c: (0, 0)),
                  pl.BlockSpec((1, GLA_QK), lambda b, c: (0, 0)),
                  pl.BlockSpec((1, GLA_V), lambda b, c: (0, 0)),
                  pl.BlockSpec((1, GLA_V), lambda b, c: (0, 0)),
                  pl.BlockSpec((None, 1, GLA_HEADS, GLA_DK, GLA_DV), lambda b, c: (layer, b, 0, 0, 0))],
        out_specs=[pl.BlockSpec((chunk, GLA_V), lambda b, c: (row(b, c), 0)),
                   pl.BlockSpec((1, GLA_HEADS, GLA_DK, GLA_DV), lambda b, c: (b, 0, 0, 0))],
        out_shape=[jax.ShapeDtypeStruct((m, GLA_V), bf16),
                   jax.ShapeDtypeStruct((batch, GLA_HEADS, GLA_DK, GLA_DV), f32)],
        compiler_params=_params(("parallel", "arbitrary")),
        name="gla",
    )(u, u, u, u, u, a_up, a_b, norm_g, norm_b, s0)


def _alibi_slopes():
    return 2.0 ** (-8.0 * jnp.arange(1, SWA_Q_HEADS + 1, dtype=f32) / SWA_Q_HEADS)


def _swa_bias(t_pos, key_pos, key_ok):
    dist = (t_pos[:, None] - key_pos[None, :]).astype(f32)
    valid = (dist >= 0) & (dist <= WINDOW) & key_ok[None, :]
    slopes = _alibi_slopes().reshape(SWA_KV_HEADS, SWA_GROUP, 1, 1)
    bias = jnp.where(valid[None, None], -slopes * dist[None, None], NEG)
    return bias.reshape(SWA_KV_HEADS, SWA_GROUP * t_pos.shape[0], key_pos.shape[0])


def _swa_group(q_all, kv, keys, vals, bias, sink_ref, rows):
    ds = slice(kv * SWA_HEAD_DIM, (kv + 1) * SWA_HEAD_DIM)
    heads = range(kv * SWA_GROUP, (kv + 1) * SWA_GROUP)
    q = jnp.concatenate([q_all[:, h * SWA_HEAD_DIM:(h + 1) * SWA_HEAD_DIM] for h in heads], axis=0)
    k2 = jnp.concatenate([kb[:, ds] for kb in keys], axis=0).astype(bf16)
    v2 = jnp.concatenate([vb[:, ds] for vb in vals], axis=0).astype(bf16)
    s = _dot_nt(q.astype(bf16), k2) * (SWA_HEAD_DIM ** -0.5) + bias
    sink = jnp.concatenate([jnp.full((rows, 1), sink_ref[h], f32) for h in heads], axis=0)
    m = jnp.maximum(jnp.max(s, axis=-1, keepdims=True), sink)
    p = jnp.exp(s - m)
    den = jnp.sum(p, axis=-1, keepdims=True) + jnp.exp(sink - m)
    o = _dot(p.astype(bf16), v2) / den
    return [o[g * rows:(g + 1) * rows, :] for g in range(SWA_GROUP)]


def _swa_prompt_kernel(sink_ref, bias_ref, q_ref, kc_ref, kp_ref, vc_ref, vp_ref, o_ref):
    q_all = q_ref[...]
    keys = (kp_ref[...], kc_ref[...])
    vals = (vp_ref[...], vc_ref[...])
    outs = []
    for kv in range(SWA_KV_HEADS):
        outs += _swa_group(q_all, kv, keys, vals, bias_ref[0, kv], sink_ref, WINDOW)
    o_ref[...] = jnp.concatenate(outs, axis=-1).astype(bf16)


def swa_prompt(u, sinks, batch, seq):
    m = u.shape[0]
    nb = seq // WINDOW
    cur = lambda b, i: b * nb + i
    prev = lambda b, i: b * nb + jnp.maximum(i - 1, 0)
    kcol, vcol = C_SK // LANES, C_SV // LANES
    t_pos = jnp.arange(WINDOW)
    key_pos = jnp.arange(2 * WINDOW) - WINDOW
    bias = jnp.stack([_swa_bias(t_pos, key_pos, key_pos >= 0), _swa_bias(t_pos, key_pos, key_pos >= -WINDOW)])
    return pl.pallas_call(
        _swa_prompt_kernel,
        grid=(batch, nb),
        in_specs=[pl.BlockSpec(memory_space=pltpu.SMEM),
                  pl.BlockSpec((1,) + bias.shape[1:], lambda b, i: (jnp.minimum(i, 1), 0, 0, 0)),
                  pl.BlockSpec((WINDOW, BRANCH_WIDTH), lambda b, i: (cur(b, i), C_SQ // BRANCH_WIDTH)),
                  pl.BlockSpec((WINDOW, LANES), lambda b, i: (cur(b, i), kcol)),
                  pl.BlockSpec((WINDOW, LANES), lambda b, i: (prev(b, i), kcol)),
                  pl.BlockSpec((WINDOW, LANES), lambda b, i: (cur(b, i), vcol)),
                  pl.BlockSpec((WINDOW, LANES), lambda b, i: (prev(b, i), vcol))],
        out_specs=pl.BlockSpec((WINDOW, BRANCH_WIDTH), lambda b, i: (cur(b, i), 0)),
        out_shape=jax.ShapeDtypeStruct((m, BRANCH_WIDTH), bf16),
        compiler_params=_params(("parallel", "parallel")),
        name="swa_prompt",
    )(sinks, bias, u, u, u, u, u)


def _swa_sample_kernel(sink_ref, bias_ref, q_ref, kn_ref, vn_ref, kb_ref, vb_ref, o_ref, *, l_pad, reqs):
    for r in range(reqs):
        rs = slice(r * l_pad, (r + 1) * l_pad)
        bs = slice(r * WINDOW, (r + 1) * WINDOW)
        q_all = q_ref[rs, :]
        keys = (kb_ref[bs, :], kn_ref[rs, :])
        vals = (vb_ref[bs, :], vn_ref[rs, :])
        outs = []
        for kv in range(SWA_KV_HEADS):
            outs += _swa_group(q_all, kv, keys, vals, bias_ref[kv], sink_ref, l_pad)
        o_ref[rs, :] = jnp.concatenate(outs, axis=-1).astype(bf16)


def swa_sample(u, kbuf, vbuf, layer, sinks, batch, l_pad, l_real, reqs):
    m = u.shape[0]
    kcol, vcol = C_SK // LANES, C_SV // LANES
    key_pos = jnp.concatenate([jnp.arange(WINDOW) - WINDOW, jnp.arange(l_pad)])
    key_ok = jnp.concatenate([jnp.ones((WINDOW,), bool), jnp.arange(l_pad) < l_real])
    bias = _swa_bias(jnp.arange(l_pad), key_pos, key_ok)
    return pl.pallas_call(
        functools.partial(_swa_sample_kernel, l_pad=l_pad, reqs=reqs),
        grid=(batch // reqs,),
        in_specs=[pl.BlockSpec(memory_space=pltpu.SMEM),
                  pl.BlockSpec(bias.shape, lambda b: (0, 0, 0)),
                  pl.BlockSpec((reqs * l_pad, BRANCH_WIDTH), lambda b: (b, C_SQ // BRANCH_WIDTH)),
                  pl.BlockSpec((reqs * l_pad, LANES), lambda b: (b, kcol)),
                  pl.BlockSpec((reqs * l_pad, LANES), lambda b: (b, vcol)),
                  pl.BlockSpec((None, reqs * WINDOW, LANES), lambda b: (layer, b, 0)),
                  pl.BlockSpec((None, reqs * WINDOW, LANES), lambda b: (layer, b, 0))],
        out_specs=pl.BlockSpec((reqs * l_pad, BRANCH_WIDTH), lambda b: (b, 0)),
        out_shape=jax.ShapeDtypeStruct((m, BRANCH_WIDTH), bf16),
        compiler_params=_params(("parallel",)),
        name="swa_sample",
    )(sinks, bias, u, u, u, kbuf, vbuf)


def _mem_kernel(q_ref, k_ref, v_ref, o_ref, *, head_axis):
    for h in range(MEM_HEADS):
        hs = slice(h * MEM_HEAD_DIM, (h + 1) * MEM_HEAD_DIM)
        kh, vh = (k_ref[:, h, :], v_ref[:, h, :]) if head_axis else (k_ref[:, hs], v_ref[:, hs])
        s = _dot_nt(q_ref[:, hs].astype(bf16), kh.astype(bf16)) * (MEM_HEAD_DIM ** -0.5)
        p = jnp.exp(s - jnp.max(s, axis=-1, keepdims=True))
        o = _dot(p.astype(bf16), vh.astype(bf16)) / jnp.sum(p, axis=-1, keepdims=True)
        o_ref[:, hs] = o.astype(bf16)


def mem_attention(u, kv, cache, layer, batch, l_pad, tl):
    m = u.shape[0]
    nl = l_pad // tl
    width = MEM_HEADS * MEM_HEAD_DIM
    if cache is None:
        kv_args = (kv, kv)
        kv_specs = [pl.BlockSpec((MEM_TOKENS, width), lambda b, i: (b, 0)),
                    pl.BlockSpec((MEM_TOKENS, width), lambda b, i: (b, 1))]
    else:
        kv_args = cache
        kv_specs = [pl.BlockSpec((None, None, MEM_TOKENS, MEM_HEADS, MEM_HEAD_DIM),
                                 lambda b, i: (layer, b, 0, 0, 0))] * 2
    return pl.pallas_call(
        functools.partial(_mem_kernel, head_axis=cache is not None),
        grid=(batch, nl),
        in_specs=[pl.BlockSpec((tl, width), lambda b, i: (b * nl + i, C_MQ // width))] + kv_specs,
        out_specs=pl.BlockSpec((tl, width), lambda b, i: (b * nl + i, 0)),
        out_shape=jax.ShapeDtypeStruct((m, width), bf16),
        compiler_params=_params(("parallel", "parallel")),
        name="mem_attention",
    )(u, *kv_args)


def _seg64_sum(x, ones_ref):
    hi = x.astype(bf16)
    lo = (x - hi.astype(f32)).astype(bf16)
    cols = []
    for j in range(x.shape[1] // LANES):
        js = slice(j * LANES, (j + 1) * LANES)
        cols.append(_dot(hi[:, js], ones_ref[...]) + _dot(lo[:, js], ones_ref[...]))
    return jnp.concatenate(cols, axis=-1)


def _rwkv_features(xs, prevs, w_refs, ones_ref):
    mur_ref, muk_ref, muv_ref, mulo_ref, w0_ref, a0_ref, wl_ref, kk_ref, ka_ref, rk_ref = w_refs

    def token_shift(x, prev, mu_ref):
        row = lax.broadcasted_iota(jnp.int32, x.shape, 0)
        shifted = jnp.where(row == 0, prev, pltpu.roll(x, 1, 0))
        return x + (shifted - x) * mu_ref[...]

    r, k0, v, lo = (token_shift(x, pv, mu) for x, pv, mu in zip(xs, prevs, (mur_ref, muk_ref, muv_ref, mulo_ref)))
    col = lax.broadcasted_iota(jnp.int32, lo.shape, 1)
    act = jnp.where(col < 64, jnp.tanh(lo), jnp.where(col < 128, lo, _sigmoid(lo)))
    proj = _dot(act.astype(bf16), wl_ref[...])
    log_w = -_softplus(-(w0_ref[...] + proj[:, :BRANCH_WIDTH])) - 0.5
    log_decay = -jnp.exp(log_w)
    a = _sigmoid(a0_ref[...] + proj[:, BRANCH_WIDTH:2 * BRANCH_WIDTH])
    g = proj[:, 2 * BRANCH_WIDTH:]
    kk = k0 * kk_ref[...]
    kk = kk / jnp.maximum(jnp.sqrt(_seg64_sum(kk * kk, ones_ref)), 1e-12)
    k = k0 * (1.0 + (a - 1.0) * ka_ref[...])
    bonus = _seg64_sum(r * k * rk_ref[...], ones_ref) * v
    return r, k, v, kk, a, log_decay, g, bonus


def _rwkv_output(y, bonus, g, lg_ref, lb_ref, ones_ref):
    mu = _seg64_sum(y, ones_ref) * (1.0 / RWKV_HEAD)
    d = y - mu
    var = _seg64_sum(d * d, ones_ref) * (1.0 / RWKV_HEAD)
    yn = d * lax.rsqrt(var + 64e-5) * lg_ref[...] + lb_ref[...]
    return ((yn + bonus) * g).astype(bf16)


def _rwkv_seq_kernel(*refs, tb, steps, reqs):
    x_refs, p_refs, w_refs = refs[0:4], refs[4:8], refs[8:18]
    lg_ref, lb_ref, ones_ref, s0_ref, o_ref, s_ref = refs[18:24]
    w3, kk3, kka3, k3, r3, v3, y3, bonus_s, g_s = refs[24:]
    s_ref[...] = s0_ref[...]
    y3[...] = jnp.zeros_like(y3)

    for q in range(reqs):
        rows = slice(q * tb, (q + 1) * tb)
        xs = [x[rows, :] for x in x_refs]
        prevs = [p[q] for p in p_refs]
        r, k, v, kk, a, log_decay, g, bonus = _rwkv_features(xs, prevs, w_refs, ones_ref)
        decay = jnp.exp(log_decay)
        kka = kk * a
        g_s[rows, :] = g
        bonus_s[rows, :] = bonus
        for h in range(RWKV_HEADS):
            hs = slice(h * RWKV_HEAD, (h + 1) * RWKV_HEAD)
            j = q * RWKV_HEADS + h
            w3[j] = decay[:, hs]
            kk3[j] = kk[:, hs]
            kka3[j] = kka[:, hs]
            k3[j] = k[:, hs]
            r3[j] = r[:, hs]
            v3[j] = v[:, hs]

    eye = (lax.broadcasted_iota(jnp.int32, (RWKV_HEAD, RWKV_HEAD), 0)
           == lax.broadcasted_iota(jnp.int32, (RWKV_HEAD, RWKV_HEAD), 1)).astype(f32)

    def step(t, carry):
        ts = pl.ds(t, 1)
        chains = [(q, h, q * RWKV_HEADS + h) for q in range(reqs) for h in range(RWKV_HEADS)]
        s_olds = [s_ref[q, h] for q, h, _ in chains]
        sas = [jnp.sum(s * kk3[j, ts, :], axis=1, keepdims=True) for s, (_, _, j) in zip(s_olds, chains)]
        v_cols = [jnp.sum(eye * v3[j, ts, :], axis=1, keepdims=True) for _, _, j in chains]
        s_news = [s * w3[j, ts, :] - sa * kka3[j, ts, :] + vc * k3[j, ts, :]
                  for s, sa, vc, (_, _, j) in zip(s_olds, sas, v_cols, chains)]
        for s_new, (q, h, _) in zip(s_news, chains):
            s_ref[q, h] = s_new
        y_cols = [jnp.sum(s * r3[j, ts, :], axis=1, keepdims=True) for s, (_, _, j) in zip(s_news, chains)]
        for y_col, (_, _, j) in zip(y_cols, chains):
            y3[j, ts, :] = jnp.sum(eye * y_col, axis=0, keepdims=True)
        return carry

    lax.fori_loop(0, steps, step, 0)

    for q in range(reqs):
        rows = slice(q * tb, (q + 1) * tb)
        y = jnp.concatenate([y3[q * RWKV_HEADS + h] for h in range(RWKV_HEADS)], axis=-1)
        o_ref[rows, :] = _rwkv_output(y, bonus_s[rows, :], g_s[rows, :], lg_ref, lb_ref, ones_ref)


def _unit_lower_inverse_minus_identity(ns):
    size = ns[0].shape[0]
    t = lax.broadcasted_iota(jnp.int32, (size, size), 0)
    s = lax.broadcasted_iota(jnp.int32, (size, size), 1)
    first = ((t >> 1) == (s >> 1)) & (t > s)
    es = [-jnp.where(first, n, 0.0) for n in ns]
    blk, shift = 4, 2
    while blk <= RWKV_HEAD:
        half = blk // 2
        sel = ((t >> shift) == (s >> shift)) & ((t & (blk - 1)) >= half) & ((s & (blk - 1)) < half)
        cs = [jnp.where(sel, n, 0.0) for n in ns]
        zs = [c + _dot(c.astype(bf16), e.astype(bf16)) for c, e in zip(cs, es)]
        es = [e - z - _dot(e.astype(bf16), z.astype(bf16)) for e, z in zip(es, zs)]
        blk, shift = blk * 2, shift + 1
    return es


def _rwkv_chunk_kernel(*refs, tb):
    x_refs, p_refs, w_refs = refs[0:4], refs[4:8], refs[8:18]
    lg_ref, lb_ref, ones_ref, s0_ref, o_ref, s_ref = refs[18:24]
    carries = refs[24:28]
    sp = refs[28]
    i = pl.program_id(1)
    n_pairs = RWKV_HEADS // 2
    hd = RWKV_HEAD

    @pl.when(i == 0)
    def _():
        zero = jnp.zeros((hd, hd), f32)
        for p in range(n_pairs):
            top = jnp.concatenate([s0_ref[0, 2 * p], zero], axis=1)
            bot = jnp.concatenate([zero, s0_ref[0, 2 * p + 1]], axis=1)
            sp[p] = jnp.concatenate([top, bot], axis=0)
        for carry, p_ref in zip(carries, p_refs):
            carry[...] = p_ref[0]

    xs = [x[...] for x in x_refs]
    prevs = [carry[...] for carry in carries]
    for carry, x in zip(carries, xs):
        carry[...] = x[tb - 1:tb, :]
    r, k, v, kk, a, lw, g, bonus = _rwkv_features(xs, prevs, w_refs, ones_ref)
    beta = kk * a
    row = lax.broadcasted_iota(jnp.int32, lw.shape, 0)
    cum = lw
    d = 1
    while d < tb:
        cum = cum + jnp.where(row >= d, pltpu.roll(cum, d, 0), 0.0)
        d *= 2
    c_last = cum[tb - 1:tb, :]
    a_t = kk * jnp.exp(cum - lw)
    r_t = r * jnp.exp(cum)
    e_neg = jnp.exp(-cum)
    k_t = k * e_neg
    b_t = beta * e_neg
    e_hat = jnp.exp(c_last - cum)
    k_h = k * e_hat
    b_h = beta * e_hat
    gamma = jnp.exp(c_last)

    lane = lax.broadcasted_iota(jnp.int32, (tb, LANES), 1)
    head0 = lane < hd
    split = lambda x: (jnp.where(head0, x, 0.0), jnp.where(head0, 0.0, x))
    tt = lax.broadcasted_iota(jnp.int32, (LANES, LANES), 0)
    ss = lax.broadcasted_iota(jnp.int32, (LANES, LANES), 1)
    strict = tt > ss
    incl = tt >= ss
    same_head = (tt >= hd) == (ss >= hd)
    fold = lambda m: m[:tb, :] + m[tb:, :]

    pairs = range(n_pairs)
    cols = [slice(p * LANES, (p + 1) * LANES) for p in pairs]
    stack = lambda *xs: jnp.concatenate(xs, axis=0)
    scs = [_dot_nt(stack(*split(a_t[:, ps]), *split(r_t[:, ps])).astype(bf16),
                   stack(*split(k_t[:, ps]), *split(b_t[:, ps])).astype(bf16)) for ps in cols]
    es = _unit_lower_inverse_minus_identity([jnp.where(strict, sc[:2 * tb, 2 * tb:], 0.0) for sc in scs])
    eye = jnp.where(tt == ss, 1.0, 0.0)
    t_cats = [fold(eye + e).astype(bf16) for e in es]
    ak_cats = [fold(jnp.where(strict, sc[:2 * tb, :2 * tb], 0.0)).astype(bf16) for sc in scs]
    r_cats = [jnp.concatenate([fold(jnp.where(incl, sc[2 * tb:, :2 * tb], 0.0)),
                               -fold(jnp.where(incl, sc[2 * tb:, 2 * tb:], 0.0))], axis=1).astype(bf16)
              for sc in scs]
    s_olds = [sp[p] for p in pairs]
    grs = [_dot_nt(stack(a_t[:, ps], r_t[:, ps]).astype(bf16), s_old.astype(bf16))
           for ps, s_old in zip(cols, s_olds)]
    v_sts = [stack(*split(v[:, ps])).astype(bf16) for ps in cols]
    u_rhss = [gr[:tb] + _dot(ak, v_st) for gr, ak, v_st in zip(grs, ak_cats, v_sts)]
    us = [_dot(t_cat, stack(*split(u_rhs)).astype(bf16)) for t_cat, u_rhs in zip(t_cats, u_rhss)]
    ys = [gr[tb:] + _dot(r_cat, stack(v_st, stack(*split(u)).astype(bf16)))
          for gr, r_cat, v_st, u in zip(grs, r_cats, v_sts, us)]
    for p, ps, s_old, u in zip(pairs, cols, s_olds, us):
        vu_t = stack(v[:, ps], -u).T.astype(bf16)
        kb = stack(k_h[:, ps], b_h[:, ps]).astype(bf16)
        sp[p] = s_old * gamma[:, ps] + jnp.where(same_head, _dot(vu_t, kb), 0.0)

    o_ref[...] = _rwkv_output(jnp.concatenate(ys, axis=-1), bonus, g, lg_ref, lb_ref, ones_ref)

    @pl.when(i == pl.num_programs(1) - 1)
    def _():
        for p in range(n_pairs):
            full = sp[p]
            s_ref[0, 2 * p] = full[:hd, :hd]
            s_ref[0, 2 * p + 1] = full[hd:, hd:]


def rwkv(u, prev, p, s0, layer, ones2, batch, l_pad, l_real, chunked, reqs=1):
    m = u.shape[0]
    bw = BRANCH_WIDTH
    full = lambda w: pl.BlockSpec((1, w), lambda b, i: (0, 0))
    if chunked:
        tb, nb, nblk = RWKV_HEAD, 1, l_pad // RWKV_HEAD
        assert l_real == l_pad and l_pad % tb == 0
        body = functools.partial(_rwkv_chunk_kernel, tb=tb)
        scratch = [pltpu.VMEM((1, bw), f32), pltpu.VMEM((1, bw), f32), pltpu.VMEM((1, bw), f32),
                   pltpu.VMEM((1, RWKV_LORA), f32), pltpu.VMEM((RWKV_HEADS // 2, LANES, LANES), f32)]
    else:
        tb, nb, nblk = l_pad, reqs, 1
        assert batch % reqs == 0
        body = functools.partial(_rwkv_seq_kernel, tb=tb, steps=l_real, reqs=reqs)
        head_buf = pltpu.VMEM((reqs * RWKV_HEADS, tb, RWKV_HEAD), f32)
        scratch = [head_buf] * 7 + [pltpu.VMEM((reqs * tb, bw), f32), pltpu.VMEM((reqs * tb, bw), f32)]
    rows = nb * tb
    row = lambda b, i: b * nblk + i
    prev_spec = lambda w: pl.BlockSpec((None, nb, 1, w), lambda b, i: (layer, b, 0, 0))
    return pl.pallas_call(
        body,
        grid=(batch // nb, nblk),
        in_specs=[pl.BlockSpec((rows, bw), lambda b, i: (row(b, i), C_R // bw)),
                  pl.BlockSpec((rows, bw), lambda b, i: (row(b, i), C_K // bw)),
                  pl.BlockSpec((rows, bw), lambda b, i: (row(b, i), C_V // bw)),
                  pl.BlockSpec((rows, RWKV_LORA), lambda b, i: (row(b, i), C_LORA // RWKV_LORA)),
                  prev_spec(bw), prev_spec(bw), prev_spec(bw), prev_spec(RWKV_LORA),
                  full(bw), full(bw), full(bw), full(RWKV_LORA),
                  full(bw), full(bw),
                  pl.BlockSpec((RWKV_LORA, 3 * bw), lambda b, i: (0, 0)),
                  full(bw), full(bw), full(bw), full(bw), full(bw),
                  pl.BlockSpec((LANES, LANES), lambda b, i: (0, 0)),
                  pl.BlockSpec((None, nb, RWKV_HEADS, RWKV_HEAD, RWKV_HEAD), lambda b, i: (layer, b, 0, 0, 0))],
        out_specs=[pl.BlockSpec((rows, bw), lambda b, i: (row(b, i), 0)),
                   pl.BlockSpec((nb, RWKV_HEADS, RWKV_HEAD, RWKV_HEAD), lambda b, i: (b, 0, 0, 0))],
        out_shape=[jax.ShapeDtypeStruct((m, bw), bf16),
                   jax.ShapeDtypeStruct((batch, RWKV_HEADS, RWKV_HEAD, RWKV_HEAD), f32)],
        scratch_shapes=scratch,
        compiler_params=_params(("parallel", "arbitrary")),
        name="rwkv7_chunked" if chunked else "rwkv7_seq",
    )(u, u, u, u, *prev, p['mu_r'], p['mu_k'], p['mu_v'], p['mu_lo'], p['w0'], p['a0'], p['w_lora'],
      p['k_k'], p['k_a'], p['r_k'], p['ln_g'], p['ln_b'], ones2, s0)


def _pack_w_in_t(w_in):
    wt = jnp.swapaxes(w_in, 1, 2)
    seg = lambda o, n: wt[:, o:o + n]
    parts = [seg(_O_GV, 1024), seg(_O_GR, 1024), seg(_O_SQ, 1024), seg(_O_MQ, 1024),
             seg(_O_RU, 3072), seg(_O_GPRE, 8192), seg(_O_GQ, 512), seg(_O_GK, 512),
             seg(_O_RU + 3072, RWKV_LORA), seg(_O_SK, 128), seg(_O_SV, 128), seg(_O_GA, GLA_GATE_RANK),
             jnp.zeros((wt.shape[0], N_PACK - C_GA - GLA_GATE_RANK, wt.shape[2]), wt.dtype)]
    return jnp.concatenate(parts, axis=1).astype(bf16)


def _layer_params(l, w):
    row = lambda x: x[l].reshape(1, -1)
    mu = w['rwkv_mu'][l]
    z = lambda r, c: jnp.zeros((r, c), f32)
    w_lora = jnp.concatenate([
        jnp.concatenate([w['rwkv_w2'][l], z(64, 1024), z(64, 1024)], axis=1),
        jnp.concatenate([z(64, 1024), w['rwkv_a2'][l], z(64, 1024)], axis=1),
        jnp.concatenate([z(128, 1024), z(128, 1024), w['rwkv_g2'][l]], axis=1)], axis=0).astype(bf16)
    a_up = jnp.concatenate([w['gla_a_up'][l], z(LANES - GLA_GATE_RANK, GLA_QK)], axis=0).astype(bf16)
    return {
        'a_up': a_up, 'a_b': row(w['gla_a_b']),
        'gla_g': row(w['gla_norm_g']), 'gla_b': row(w['gla_norm_b']),
        'sinks': w['swa_sinks'][l],
        'rwkv': {'mu_r': mu[:1024].reshape(1, -1), 'mu_k': mu[1024:2048].reshape(1, -1),
                 'mu_v': mu[2048:3072].reshape(1, -1), 'mu_lo': mu[3072:].reshape(1, -1),
                 'w0': row(w['rwkv_w0']), 'a0': row(w['rwkv_a0']), 'w_lora': w_lora,
                 'k_k': row(w['rwkv_k_k']), 'k_a': row(w['rwkv_k_a']), 'r_k': row(w['rwkv_r_k']),
                 'ln_g': row(w['rwkv_ln_g']), 'ln_b': row(w['rwkv_ln_b'])},
        'ln1_g': row(w['ln1_g']), 'ln1_b': row(w['ln1_b']),
        'ln2_g': row(w['ln2_g']), 'ln2_b': row(w['ln2_b']),
    }


def _tiles(m):
    if m >= 1024:
        return {'proj': 1024, 'merge': 512, 'ffn': 1024, 'ln': 512}
    return {'proj': m, 'merge': m, 'ffn': m, 'ln': m}


def _trunk_layer(h_f, h_b, l, p, big, ones2, mem_kv, mem_cache, gla_s0, rwkv_s0, rwkv_prev, swa_buf,
                 batch, l_pad, l_real):
    m = h_f.shape[0]
    tl = _tiles(m)
    u = matmul(h_b, big['w_in_t'], l, tl['proj'], 1024, "in_proj", trans_w=True)
    o_a, gla_s = gla(u, p['a_up'], p['a_b'], p['gla_g'], p['gla_b'], gla_s0[0], gla_s0[1], batch, l_pad, l_real,
                     min(GLA_CHUNK, l_pad))
    if swa_buf is None:
        o_b = swa_prompt(u, p['sinks'], batch, l_pad)
    else:
        o_b = swa_sample(u, swa_buf[0], swa_buf[1], l, p['sinks'], batch, l_pad, l_real, 4)
    chunked = l_real == l_pad and l_pad % RWKV_HEAD == 0
    o_c, rwkv_s = rwkv(u, rwkv_prev[0], p['rwkv'], rwkv_s0[0], rwkv_s0[1], ones2, batch, l_pad, l_real,
                       chunked, reqs=1 if chunked else 4)
    o_m = mem_attention(u, mem_kv, mem_cache, l, batch, l_pad, min(512, l_pad))
    merged = gated_merge((o_a, o_b, o_c, o_m), u, big['gate_b'], big['w_branch'], l, tl['merge'], 512)
    x_f, x_b = matmul_residual_ln(merged, big['w_out'], l, h_f, p['ln1_g'], p['ln1_b'], tl['ln'], D_MODEL)
    act = ffn_up(x_b, big['w_gu'], l, tl['ffn'], 512)
    y_f, y_b = matmul_residual_ln(act, big['w_down'], l, x_f, p['ln2_g'], p['ln2_b'], tl['ln'], D_FF // 4)
    return y_f, y_b, u, gla_s, rwkv_s


def _split_ru(x):
    return (x[..., :1024], x[..., 1024:2048], x[..., 2048:3072], x[..., 3072:])


def kernel(x_prompt, x_sample, mem_prompt, cache_swa_k, cache_swa_v, cache_mem_k, cache_mem_v, state_gla, state_rwkv, state_rwkv_shift, w_in, gate_b, gla_a_up, gla_a_b, gla_norm_g, gla_norm_b, swa_sinks, rwkv_mu, rwkv_w0, rwkv_w2, rwkv_a0, rwkv_a2, rwkv_g2, rwkv_k_k, rwkv_k_a, rwkv_r_k, rwkv_ln_g, rwkv_ln_b, w_mem_kv, w_branch, w_out, ln1_g, ln1_b, w_gu, w_down, ln2_g, ln2_b):
    weights = {'gla_a_up': gla_a_up, 'gla_a_b': gla_a_b,
               'gla_norm_g': gla_norm_g, 'gla_norm_b': gla_norm_b, 'swa_sinks': swa_sinks,
               'rwkv_mu': rwkv_mu, 'rwkv_w0': rwkv_w0, 'rwkv_w2': rwkv_w2, 'rwkv_a0': rwkv_a0,
               'rwkv_a2': rwkv_a2, 'rwkv_g2': rwkv_g2, 'rwkv_k_k': rwkv_k_k, 'rwkv_k_a': rwkv_k_a,
               'rwkv_r_k': rwkv_r_k, 'rwkv_ln_g': rwkv_ln_g, 'rwkv_ln_b': rwkv_ln_b,
               'ln1_g': ln1_g, 'ln1_b': ln1_b, 'ln2_g': ln2_g, 'ln2_b': ln2_b}
    big = {'w_in_t': _pack_w_in_t(w_in), 'gate_b': gate_b, 'w_branch': w_branch, 'w_gu': w_gu,
           'w_out': w_out.astype(bf16), 'w_down': w_down.astype(bf16)}
    w_mem_kv_b = w_mem_kv.astype(bf16)
    bp, lp, _ = x_prompt.shape
    bs, ls, _ = x_sample.shape
    ls_pad = -(-ls // SUBLANES) * SUBLANES
    mp, ms = bp * lp, bs * ls_pad

    hp_f = x_prompt.reshape(mp, D_MODEL)
    hs_f = jnp.pad(x_sample, ((0, 0), (0, ls_pad - ls), (0, 0))).reshape(ms, D_MODEL)
    hp_b, hs_b = hp_f.astype(bf16), hs_f.astype(bf16)
    mem_b = mem_prompt.reshape(bp * MEM_TOKENS, D_MODEL).astype(bf16)
    half = LANES // 2
    blk = jnp.ones((half, half), f32)
    zero = jnp.zeros((half, half), f32)
    ones2 = jnp.concatenate([jnp.concatenate([blk, zero], 1), jnp.concatenate([zero, blk], 1)], 0).astype(bf16)

    gla0_p = jnp.zeros((1, bp, GLA_HEADS, GLA_DK, GLA_DV), f32)
    rwkv0_p = jnp.zeros((1, bp, RWKV_HEADS, RWKV_HEAD, RWKV_HEAD), f32)
    prev0_p = _split_ru(jnp.zeros((1, bp, 1, RWKV_COLS), f32))
    prev_s = _split_ru(state_rwkv_shift)
    kvw = SWA_KV_HEADS * SWA_HEAD_DIM
    kbuf = cache_swa_k.reshape(DEPTH, bs * WINDOW, kvw)
    vbuf = cache_swa_v.reshape(DEPTH, bs * WINDOW, kvw)

    outs = {k: [] for k in ('p_swk', 'p_swv', 'p_mk', 'p_mv', 'p_gla', 'p_rw', 'p_rs',
                            's_swk', 's_swv', 's_gla', 's_rw', 's_rs')}
    for l in range(DEPTH):
        p = _layer_params(l, weights)
        kv = matmul(mem_b, w_mem_kv_b, l, bp * MEM_TOKENS, 512, "mem_kv")
        hp_f, hp_b, u, gs, rs = _trunk_layer(hp_f, hp_b, l, p, big, ones2, kv, None, (gla0_p, 0), (rwkv0_p, 0),
                                             (prev0_p, 0), None, bp, lp, lp)
        u3 = u.reshape(bp, lp, N_PACK)
        outs['p_swk'].append(u3[:, lp - WINDOW:, C_SK:C_SK + kvw].reshape(bp, WINDOW, SWA_KV_HEADS, SWA_HEAD_DIM))
        outs['p_swv'].append(u3[:, lp - WINDOW:, C_SV:C_SV + kvw].reshape(bp, WINDOW, SWA_KV_HEADS, SWA_HEAD_DIM))
        outs['p_mk'].append(kv[:, :1024].reshape(bp, MEM_TOKENS, MEM_HEADS, MEM_HEAD_DIM))
        outs['p_mv'].append(kv[:, 1024:].reshape(bp, MEM_TOKENS, MEM_HEADS, MEM_HEAD_DIM))
        outs['p_gla'].append(gs)
        outs['p_rw'].append(rs)
        outs['p_rs'].append(jnp.concatenate([u3[:, lp - 1:, C_R:C_R + 3072],
                                             u3[:, lp - 1:, C_LORA:C_LORA + RWKV_LORA]], axis=-1))
        hs_f, hs_b, u, gs, rs = _trunk_layer(hs_f, hs_b, l, p, big, ones2, None, (cache_mem_k, cache_mem_v),
                                             (state_gla, l), (state_rwkv, l), (prev_s, l), (kbuf, vbuf),
                                             bs, ls_pad, ls)
        u3 = u.reshape(bs, ls_pad, N_PACK)
        k_new = u3[:, :ls, C_SK:C_SK + kvw].reshape(bs, ls, SWA_KV_HEADS, SWA_HEAD_DIM)
        v_new = u3[:, :ls, C_SV:C_SV + kvw].reshape(bs, ls, SWA_KV_HEADS, SWA_HEAD_DIM)
        outs['s_swk'].append(jnp.concatenate([cache_swa_k[l][:, ls:], k_new], axis=1))
        outs['s_swv'].append(jnp.concatenate([cache_swa_v[l][:, ls:], v_new], axis=1))
        outs['s_gla'].append(gs)
        outs['s_rw'].append(rs)
        outs['s_rs'].append(jnp.concatenate([u3[:, ls - 1:ls, C_R:C_R + 3072],
                                             u3[:, ls - 1:ls, C_LORA:C_LORA + RWKV_LORA]], axis=-1))

    st = {k: jnp.stack(v) for k, v in outs.items()}
    y_prompt = hp_f.reshape(bp, lp, D_MODEL)
    y_sample = hs_f.reshape(bs, ls_pad, D_MODEL)[:, :ls]
    return (y_prompt, y_sample,
            st['p_swk'], st['p_swv'], st['p_mk'], st['p_mv'], st['p_gla'], st['p_rw'], st['p_rs'],
            st['s_swk'], st['s_swv'], st['s_gla'], st['s_rw'], st['s_rs'])
```

```python
import functools

import jax
import jax.numpy as jnp
from jax import lax
from jax.experimental import pallas as pl
from jax.experimental.pallas import tpu as pltpu

f32 = jnp.float32
bf16 = jnp.bfloat16

D_MODEL = 2048
DEPTH = 4
BRANCH_WIDTH = 1024
N_BRANCH = 4
GLA_HEADS = 4
GLA_QK = 512
GLA_V = 1024
GLA_DK = 128
GLA_DV = 256
GLA_GATE_RANK = 16
GLA_TAU = 16.0
GLA_CHUNK = 64
SWA_HEAD_DIM = 64
SWA_Q_HEADS = 16
SWA_KV_HEADS = 2
SWA_GROUP = SWA_Q_HEADS // SWA_KV_HEADS
WINDOW = 128
RWKV_HEAD = 64
RWKV_HEADS = 16
RWKV_LORA = 256
RWKV_COLS = 3 * BRANCH_WIDTH + RWKV_LORA
MEM_TOKENS = 256
MEM_HEADS = 4
MEM_HEAD_DIM = 256
D_FF = 5632
DEEPNORM_ALPHA = (2 * DEPTH) ** 0.25
NEG = -1e30

_O_GQ, _O_GK, _O_GV, _O_GR, _O_GA = 0, 512, 1024, 2048, 3072
_O_SQ, _O_SK, _O_SV = 3088, 4112, 4240
_O_RU = 4368
_O_MQ = 7696
_O_GPRE = 8720
_N_IN = 16912

C_GV, C_GR, C_SQ, C_MQ, C_R, C_K, C_V = 0, 1024, 2048, 3072, 4096, 5120, 6144
C_GPRE = 7168
C_GQ, C_GK = 15360, 15872
C_LORA = 16384
C_SK, C_SV, C_GA = 16640, 16768, 16896
N_PACK = 17408

LANES = 128
SUBLANES = 8
VMEM_BYTES_V7X = 64 * 1024 * 1024
VMEM_LIMIT = VMEM_BYTES_V7X - 8 * 1024 * 1024


def _params(sem):
    return pltpu.CompilerParams(dimension_semantics=sem, vmem_limit_bytes=VMEM_LIMIT)


def _softplus(z):
    return jnp.maximum(z, 0.0) + jnp.log(1.0 + jnp.exp(-jnp.abs(z)))


def _sigmoid(z):
    return 1.0 / (1.0 + jnp.exp(-z))


def _dot(a, b):
    return jnp.dot(a, b, preferred_element_type=f32)


def _dot_nt(a, b):
    return lax.dot_general(a, b, (((1,), (1,)), ((), ())), preferred_element_type=f32)


def _mm_kernel(x_ref, w_ref, o_ref, *, trans_w):
    mm = _dot_nt if trans_w else _dot
    o_ref[...] = mm(x_ref[...], w_ref[...])


def matmul(x, w, layer, tm, tn, name, trans_w=False):
    m, k = x.shape
    n = w.shape[1] if trans_w else w.shape[2]
    w_spec = (pl.BlockSpec((None, tn, k), lambda i, j: (layer, j, 0)) if trans_w
              else pl.BlockSpec((None, k, tn), lambda i, j: (layer, 0, j)))
    return pl.pallas_call(
        functools.partial(_mm_kernel, trans_w=trans_w),
        grid=(m // tm, n // tn),
        in_specs=[pl.BlockSpec((tm, k), lambda i, j: (i, 0)), w_spec],
        out_specs=pl.BlockSpec((tm, tn), lambda i, j: (i, j)),
        out_shape=jax.ShapeDtypeStruct((m, n), f32),
        compiler_params=_params(("parallel", "parallel")),
        name=name,
    )(x, w)


def _mm_ln_kernel(x_ref, w_ref, res_ref, g_ref, b_ref, of_ref, ob_ref):
    z = DEEPNORM_ALPHA * res_ref[...] + _dot(x_ref[...], w_ref[...])
    mu = jnp.mean(z, axis=-1, keepdims=True)
    d = z - mu
    var = jnp.mean(d * d, axis=-1, keepdims=True)
    y = d * lax.rsqrt(var + 1e-5) * g_ref[...] + b_ref[...]
    of_ref[...] = y
    ob_ref[...] = y.astype(bf16)


def matmul_residual_ln(x, w, layer, res, g, b, tm):
    m, k = x.shape
    n = w.shape[2]
    return pl.pallas_call(
        _mm_ln_kernel,
        grid=(m // tm,),
        in_specs=[pl.BlockSpec((tm, k), lambda i: (i, 0)),
                  pl.BlockSpec((None, k, n), lambda i: (layer, 0, 0), pipeline_mode=pl.Buffered(1)),
                  pl.BlockSpec((tm, n), lambda i: (i, 0)),
                  pl.BlockSpec((1, n), lambda i: (0, 0)),
                  pl.BlockSpec((1, n), lambda i: (0, 0))],
        out_specs=[pl.BlockSpec((tm, n), lambda i: (i, 0)),
                   pl.BlockSpec((tm, n), lambda i: (i, 0))],
        out_shape=[jax.ShapeDtypeStruct((m, n), f32), jax.ShapeDtypeStruct((m, n), bf16)],
        compiler_params=_params(("parallel",)),
        name="proj_ln",
    )(x, w, res, g, b)


def _merge_kernel(a_ref, b_ref, c_ref, m_ref, g0_ref, g1_ref, g2_ref, g3_ref, gb_ref, w_ref, o_ref, w_bf):
    @pl.when(pl.program_id(1) == 0)
    def _():
        w_bf[...] = w_ref[...].astype(bf16)

    acc = None
    for n, (br, gp) in enumerate(((a_ref, g0_ref), (b_ref, g1_ref), (c_ref, g2_ref), (m_ref, g3_ref))):
        y = _dot(br[...], w_bf[n])
        gate = _sigmoid(gp[...] + gb_ref[n:n + 1, :])
        acc = gate * y if acc is None else acc + gate * y
    o_ref[...] = acc.astype(bf16)


def gated_merge(branches, u, gate_b, w_branch, layer, tm, tn):
    m = u.shape[0]
    gp0 = C_GPRE // tn
    per = D_MODEL // tn
    br_spec = pl.BlockSpec((tm, BRANCH_WIDTH), lambda j, i: (i, 0))
    gp_specs = [pl.BlockSpec((tm, tn), functools.partial(lambda j, i, n: (i, gp0 + n * per + j), n=n))
                for n in range(N_BRANCH)]
    return pl.pallas_call(
        _merge_kernel,
        grid=(D_MODEL // tn, m // tm),
        in_specs=[br_spec] * 4 + gp_specs + [
            pl.BlockSpec((None, N_BRANCH, tn), lambda j, i: (layer, 0, j)),
            pl.BlockSpec((None, N_BRANCH, BRANCH_WIDTH, tn), lambda j, i: (layer, 0, 0, j))],
        out_specs=pl.BlockSpec((tm, tn), lambda j, i: (i, j)),
        out_shape=jax.ShapeDtypeStruct((m, D_MODEL), bf16),
        scratch_shapes=[pltpu.VMEM((N_BRANCH, BRANCH_WIDTH, tn), bf16)],
        compiler_params=_params(("parallel", "arbitrary")),
        name="gated_merge",
    )(*branches, u, u, u, u, gate_b, w_branch)


def _ffn_up_kernel(x_ref, wg_ref, wu_ref, o_ref, wg_bf, wu_bf):
    @pl.when(pl.program_id(1) == 0)
    def _():
        wg_bf[...] = wg_ref[...].astype(bf16)
        wu_bf[...] = wu_ref[...].astype(bf16)

    x = x_ref[...]
    g = _dot(x, wg_bf[...])
    up = _dot(x, wu_bf[...])
    o_ref[...] = (g * _sigmoid(g) * up).astype(bf16)


def ffn_up(x, w_gu, layer, tm, tn):
    m, k = x.shape
    nj = D_FF // tn
    return pl.pallas_call(
        _ffn_up_kernel,
        grid=(nj, m // tm),
        in_specs=[pl.BlockSpec((tm, k), lambda j, i: (i, 0)),
                  pl.BlockSpec((None, k, tn), lambda j, i: (layer, 0, j)),
                  pl.BlockSpec((None, k, tn), lambda j, i: (layer, 0, nj + j))],
        out_specs=pl.BlockSpec((tm, tn), lambda j, i: (i, j)),
        out_shape=jax.ShapeDtypeStruct((m, D_FF), bf16),
        scratch_shapes=[pltpu.VMEM((k, tn), bf16), pltpu.VMEM((k, tn), bf16)],
        compiler_params=_params(("parallel", "arbitrary")),
        name="ffn_up",
    )(x, w_gu, w_gu)


def _gla_kernel(q_ref, k_ref, v_ref, r_ref, a_ref, aup_ref, ab_ref, ng_ref, nb_ref, s0_ref,
                o_ref, s_ref, *, chunk, l_real, l_pad):
    c = pl.program_id(1)

    @pl.when(c == 0)
    def _():
        s_ref[...] = s0_ref[...]

    la = _dot(a_ref[...].astype(bf16), aup_ref[...]) + ab_ref[...]
    la = (jnp.minimum(la, 0.0) - jnp.log(1.0 + jnp.exp(-jnp.abs(la)))) * (1.0 / GLA_TAU)
    row = lax.broadcasted_iota(jnp.int32, (chunk, GLA_QK), 0)
    kin = k_ref[...]
    if l_real < l_pad:
        real = (c * chunk + row) < l_real
        la = jnp.where(real, la, 0.0)
        kin = jnp.where(real, kin, 0.0)
    b = la
    d = 1
    while d < chunk:
        b = b + jnp.where(row >= d, pltpu.roll(b, d, 0), 0.0)
        d *= 2
    b_last = b[chunk - 1:chunk, :]
    q_dec = q_ref[...] * (GLA_DK ** -0.5) * jnp.exp(b)
    k_inv = kin * jnp.exp(-b)
    k_dec = kin * jnp.exp(b_last - b)
    tt = lax.broadcasted_iota(jnp.int32, (chunk, chunk), 0)
    ss = lax.broadcasted_iota(jnp.int32, (chunk, chunk), 1)
    causal = tt >= ss
    heads = range(GLA_HEADS)
    kss = [slice(h * GLA_DK, (h + 1) * GLA_DK) for h in heads]
    vss = [slice(h * GLA_DV, (h + 1) * GLA_DV) for h in heads]
    qhs = [q_dec[:, ks].astype(bf16) for ks in kss]
    atts = [jnp.where(causal, _dot_nt(qh, k_inv[:, ks].astype(bf16)), 0.0).astype(bf16) for qh, ks in zip(qhs, kss)]
    vhs = [v_ref[:, vs].astype(bf16) for vs in vss]
    s_olds = [s_ref[0, h] for h in heads]
    os_ = [_dot(att, vh) + _dot(qh, s_old.astype(bf16)) for att, vh, qh, s_old in zip(atts, vhs, qhs, s_olds)]
    decay_cols = [jnp.exp(jnp.sum(la[:, ks].T, axis=1, keepdims=True)) for ks in kss]
    for h, ks, vh, s_old, decay_col in zip(heads, kss, vhs, s_olds, decay_cols):
        s_ref[0, h] = decay_col * s_old + _dot(k_dec[:, ks].T.astype(bf16), vh)
    for vs, o in zip(vss, os_):
        mu = jnp.mean(o, axis=-1, keepdims=True)
        dd = o - mu
        var = jnp.mean(dd * dd, axis=-1, keepdims=True)
        nrm = dd * lax.rsqrt(var + 1e-5) * ng_ref[:, vs] + nb_ref[:, vs]
        gr = r_ref[:, vs]
        o_ref[:, vs] = (nrm * (gr * _sigmoid(gr))).astype(bf16)


def gla(u, a_up, a_b, norm_g, norm_b, s0, layer, batch, l_pad, l_real, chunk):
    m = u.shape[0]
    nc = l_pad // chunk
    row = lambda b, c: b * nc + c
    return pl.pallas_call(
        functools.partial(_gla_kernel, chunk=chunk, l_real=l_real, l_pad=l_pad),
        grid=(batch, nc),
        in_specs=[pl.BlockSpec((chunk, GLA_QK), lambda b, c: (row(b, c), C_GQ // GLA_QK)),
                  pl.BlockSpec((chunk, GLA_QK), lambda b, c: (row(b, c), C_GK // GLA_QK)),
                  pl.BlockSpec((chunk, GLA_V), lambda b, c: (row(b, c), C_GV // GLA_V)),
                  pl.BlockSpec((chunk, GLA_V), lambda b, c: (row(b, c), C_GR // GLA_V)),
                  pl.BlockSpec((chunk, LANES), lambda b, c: (row(b, c), C_GA // LANES)),
                  pl.BlockSpec((LANES, GLA_QK), lambda b, c: (0, 0)),
                  pl.BlockSpec((1, GLA_QK), lambda b, c: (0, 0)),
                  pl.BlockSpec((1, GLA_V), lambda b, c: (0, 0)),
                  pl.BlockSpec((1, GLA_V), lambda b, c: (0, 0)),
                  pl.BlockSpec((None, 1, GLA_HEADS, GLA_DK, GLA_DV), lambda b, c: (layer, b, 0, 0, 0))],
        out_specs=[pl.BlockSpec((chunk, GLA_V), lambda b, c: (row(b, c), 0)),
                   pl.BlockSpec((1, GLA_HEADS, GLA_DK, GLA_DV), lambda b, c: (b, 0, 0, 0))],
        out_shape=[jax.ShapeDtypeStruct((m, GLA_V), bf16),
                   jax.ShapeDtypeStruct((batch, GLA_HEADS, GLA_DK, GLA_DV), f32)],
        compiler_params=_params(("parallel", "arbitrary")),
        name="gla",
    )(u, u, u, u, u, a_up, a_b, norm_g, norm_b, s0)


def _alibi_slopes():
    return 2.0 ** (-8.0 * jnp.arange(1, SWA_Q_HEADS + 1, dtype=f32) / SWA_Q_HEADS)


def _swa_bias(t_pos, key_pos, key_ok):
    dist = (t_pos[:, None] - key_pos[None, :]).astype(f32)
    valid = (dist >= 0) & (dist <= WINDOW) & key_ok[None, :]
    slopes = _alibi_slopes().reshape(SWA_KV_HEADS, SWA_GROUP, 1, 1)
    bias = jnp.where(valid[None, None], -slopes * dist[None, None], NEG)
    return bias.reshape(SWA_KV_HEADS, SWA_GROUP * t_pos.shape[0], key_pos.shape[0])


def _swa_groups(q_all, keys, vals, biases, sink_ref, rows):
    kvs = range(SWA_KV_HEADS)
    dss = [slice(kv * SWA_HEAD_DIM, (kv + 1) * SWA_HEAD_DIM) for kv in kvs]
    heads = [range(kv * SWA_GROUP, (kv + 1) * SWA_GROUP) for kv in kvs]
    qs = [jnp.concatenate([q_all[:, h * SWA_HEAD_DIM:(h + 1) * SWA_HEAD_DIM] for h in hh], axis=0).astype(bf16)
          for hh in heads]
    k2s = [jnp.concatenate([kb[:, ds] for kb in keys], axis=0).astype(bf16) for ds in dss]
    v2s = [jnp.concatenate([vb[:, ds] for vb in vals], axis=0).astype(bf16) for ds in dss]
    ss = [_dot_nt(q, k2) * (SWA_HEAD_DIM ** -0.5) + bias for q, k2, bias in zip(qs, k2s, biases)]
    sinks = [jnp.concatenate([jnp.full((rows, 1), sink_ref[h], f32) for h in hh], axis=0) for hh in heads]
    ms = [jnp.maximum(jnp.max(s, axis=-1, keepdims=True), sink) for s, sink in zip(ss, sinks)]
    ps = [jnp.exp(s - m) for s, m in zip(ss, ms)]
    dens = [jnp.sum(p, axis=-1, keepdims=True) + jnp.exp(sink - m) for p, sink, m in zip(ps, sinks, ms)]
    os_ = [_dot(p.astype(bf16), v2) / den for p, v2, den in zip(ps, v2s, dens)]
    return [o[g * rows:(g + 1) * rows, :] for o in os_ for g in range(SWA_GROUP)]


def _swa_prompt_kernel(sink_ref, bias_ref, q_ref, kc_ref, kp_ref, vc_ref, vp_ref, o_ref):
    keys = (kp_ref[...], kc_ref[...])
    vals = (vp_ref[...], vc_ref[...])
    biases = [bias_ref[0, kv] for kv in range(SWA_KV_HEADS)]
    outs = _swa_groups(q_ref[...], keys, vals, biases, sink_ref, WINDOW)
    o_ref[...] = jnp.concatenate(outs, axis=-1).astype(bf16)


def swa_prompt(u, sinks, batch, seq):
    m = u.shape[0]
    nb = seq // WINDOW
    cur = lambda b, i: b * nb + i
    prev = lambda b, i: b * nb + jnp.maximum(i - 1, 0)
    kcol, vcol = C_SK // LANES, C_SV // LANES
    t_pos = jnp.arange(WINDOW)
    key_pos = jnp.arange(2 * WINDOW) - WINDOW
    bias = jnp.stack([_swa_bias(t_pos, key_pos, key_pos >= 0), _swa_bias(t_pos, key_pos, key_pos >= -WINDOW)])
    return pl.pallas_call(
        _swa_prompt_kernel,
        grid=(batch, nb),
        in_specs=[pl.BlockSpec(memory_space=pltpu.SMEM),
                  pl.BlockSpec((1,) + bias.shape[1:], lambda b, i: (jnp.minimum(i, 1), 0, 0, 0)),
                  pl.BlockSpec((WINDOW, BRANCH_WIDTH), lambda b, i: (cur(b, i), C_SQ // BRANCH_WIDTH)),
                  pl.BlockSpec((WINDOW, LANES), lambda b, i: (cur(b, i), kcol)),
                  pl.BlockSpec((WINDOW, LANES), lambda b, i: (prev(b, i), kcol)),
                  pl.BlockSpec((WINDOW, LANES), lambda b, i: (cur(b, i), vcol)),
                  pl.BlockSpec((WINDOW, LANES), lambda b, i: (prev(b, i), vcol))],
        out_specs=pl.BlockSpec((WINDOW, BRANCH_WIDTH), lambda b, i: (cur(b, i), 0)),
        out_shape=jax.ShapeDtypeStruct((m, BRANCH_WIDTH), bf16),
        compiler_params=_params(("parallel", "parallel")),
        name="swa_prompt",
    )(sinks, bias, u, u, u, u, u)


def _swa_sample_kernel(sink_ref, bias_ref, q_ref, kn_ref, vn_ref, kb_ref, vb_ref, o_ref, *, l_pad, reqs):
    for r in range(reqs):
        rs = slice(r * l_pad, (r + 1) * l_pad)
        bs = slice(r * WINDOW, (r + 1) * WINDOW)
        keys = (kb_ref[bs, :], kn_ref[rs, :])
        vals = (vb_ref[bs, :], vn_ref[rs, :])
        biases = [bias_ref[kv] for kv in range(SWA_KV_HEADS)]
        outs = _swa_groups(q_ref[rs, :], keys, vals, biases, sink_ref, l_pad)
        o_ref[rs, :] = jnp.concatenate(outs, axis=-1).astype(bf16)


def swa_sample(u, kbuf, vbuf, layer, sinks, batch, l_pad, l_real, reqs):
    m = u.shape[0]
    kcol, vcol = C_SK // LANES, C_SV // LANES
    key_pos = jnp.concatenate([jnp.arange(WINDOW) - WINDOW, jnp.arange(l_pad)])
    key_ok = jnp.concatenate([jnp.ones((WINDOW,), bool), jnp.arange(l_pad) < l_real])
    bias = _swa_bias(jnp.arange(l_pad), key_pos, key_ok)
    return pl.pallas_call(
        functools.partial(_swa_sample_kernel, l_pad=l_pad, reqs=reqs),
        grid=(batch // reqs,),
        in_specs=[pl.BlockSpec(memory_space=pltpu.SMEM),
                  pl.BlockSpec(bias.shape, lambda b: (0, 0, 0)),
                  pl.BlockSpec((reqs * l_pad, BRANCH_WIDTH), lambda b: (b, C_SQ // BRANCH_WIDTH)),
                  pl.BlockSpec((reqs * l_pad, LANES), lambda b: (b, kcol)),
                  pl.BlockSpec((reqs * l_pad, LANES), lambda b: (b, vcol)),
                  pl.BlockSpec((None, reqs * WINDOW, LANES), lambda b: (layer, b, 0)),
                  pl.BlockSpec((None, reqs * WINDOW, LANES), lambda b: (layer, b, 0))],
        out_specs=pl.BlockSpec((reqs * l_pad, BRANCH_WIDTH), lambda b: (b, 0)),
        out_shape=jax.ShapeDtypeStruct((m, BRANCH_WIDTH), bf16),
        compiler_params=_params(("parallel",)),
        name="swa_sample",
    )(sinks, bias, u, u, u, kbuf, vbuf)


def _mem_kernel(q_ref, k_ref, v_ref, o_ref, *, head_axis):
    for h in range(MEM_HEADS):
        hs = slice(h * MEM_HEAD_DIM, (h + 1) * MEM_HEAD_DIM)
        kh, vh = (k_ref[:, h, :], v_ref[:, h, :]) if head_axis else (k_ref[:, hs], v_ref[:, hs])
        s = _dot_nt(q_ref[:, hs].astype(bf16), kh.astype(bf16)) * (MEM_HEAD_DIM ** -0.5)
        p = jnp.exp(s - jnp.max(s, axis=-1, keepdims=True))
        o = _dot(p.astype(bf16), vh.astype(bf16)) / jnp.sum(p, axis=-1, keepdims=True)
        o_ref[:, hs] = o.astype(bf16)


def mem_attention(u, kv, cache, layer, batch, l_pad, tl):
    m = u.shape[0]
    nl = l_pad // tl
    width = MEM_HEADS * MEM_HEAD_DIM
    if cache is None:
        kv_args = (kv, kv)
        kv_specs = [pl.BlockSpec((MEM_TOKENS, width), lambda b, i: (b, 0)),
                    pl.BlockSpec((MEM_TOKENS, width), lambda b, i: (b, 1))]
    else:
        kv_args = cache
        kv_specs = [pl.BlockSpec((None, None, MEM_TOKENS, MEM_HEADS, MEM_HEAD_DIM),
                                 lambda b, i: (layer, b, 0, 0, 0))] * 2
    return pl.pallas_call(
        functools.partial(_mem_kernel, head_axis=cache is not None),
        grid=(batch, nl),
        in_specs=[pl.BlockSpec((tl, width), lambda b, i: (b * nl + i, C_MQ // width))] + kv_specs,
        out_specs=pl.BlockSpec((tl, width), lambda b, i: (b * nl + i, 0)),
        out_shape=jax.ShapeDtypeStruct((m, width), bf16),
        compiler_params=_params(("parallel", "parallel")),
        name="mem_attention",
    )(u, *kv_args)


def _seg64_sum(x, ones_ref):
    hi = x.astype(bf16)
    lo = (x - hi.astype(f32)).astype(bf16)
    cols = []
    for j in range(x.shape[1] // LANES):
        js = slice(j * LANES, (j + 1) * LANES)
        cols.append(_dot(hi[:, js], ones_ref[...]) + _dot(lo[:, js], ones_ref[...]))
    return jnp.concatenate(cols, axis=-1)


def _rwkv_features(xs, prevs, w_refs, ones_ref):
    mur_ref, muk_ref, muv_ref, mulo_ref, w0_ref, a0_ref, wl_ref, kk_ref, ka_ref, rk_ref = w_refs

    def token_shift(x, prev, mu_ref):
        row = lax.broadcasted_iota(jnp.int32, x.shape, 0)
        shifted = jnp.where(row == 0, prev, pltpu.roll(x, 1, 0))
        return x + (shifted - x) * mu_ref[...]

    r, k0, v, lo = (token_shift(x, pv, mu) for x, pv, mu in zip(xs, prevs, (mur_ref, muk_ref, muv_ref, mulo_ref)))
    col = lax.broadcasted_iota(jnp.int32, lo.shape, 1)
    act = jnp.where(col < 64, jnp.tanh(lo), jnp.where(col < 128, lo, _sigmoid(lo)))
    proj = _dot(act.astype(bf16), wl_ref[...])
    log_w = -_softplus(-(w0_ref[...] + proj[:, :BRANCH_WIDTH])) - 0.5
    log_decay = -jnp.exp(log_w)
    a = _sigmoid(a0_ref[...] + proj[:, BRANCH_WIDTH:2 * BRANCH_WIDTH])
    g = proj[:, 2 * BRANCH_WIDTH:]
    kk = k0 * kk_ref[...]
    kk = kk / jnp.maximum(jnp.sqrt(_seg64_sum(kk * kk, ones_ref)), 1e-12)
    k = k0 * (1.0 + (a - 1.0) * ka_ref[...])
    bonus = _seg64_sum(r * k * rk_ref[...], ones_ref) * v
    return r, k, v, kk, a, log_decay, g, bonus


def _rwkv_output(y, bonus, g, lg_ref, lb_ref, ones_ref):
    mu = _seg64_sum(y, ones_ref) * (1.0 / RWKV_HEAD)
    d = y - mu
    var = _seg64_sum(d * d, ones_ref) * (1.0 / RWKV_HEAD)
    yn = d * lax.rsqrt(var + 64e-5) * lg_ref[...] + lb_ref[...]
    return ((yn + bonus) * g).astype(bf16)


def _rwkv_seq_kernel(*refs, tb, steps, reqs):
    x_refs, p_refs, w_refs = refs[0:4], refs[4:8], refs[8:18]
    lg_ref, lb_ref, ones_ref, s0_ref, o_ref, s_ref = refs[18:24]
    w3, kk3, kka3, k3, r3, v3, y3, bonus_s, g_s = refs[24:]
    s_ref[...] = s0_ref[...]
    y3[...] = jnp.zeros_like(y3)

    for q in range(reqs):
        rows = slice(q * tb, (q + 1) * tb)
        xs = [x[rows, :] for x in x_refs]
        prevs = [p[q] for p in p_refs]
        r, k, v, kk, a, log_decay, g, bonus = _rwkv_features(xs, prevs, w_refs, ones_ref)
        decay = jnp.exp(log_decay)
        kka = kk * a
        g_s[rows, :] = g
        bonus_s[rows, :] = bonus
        for h in range(RWKV_HEADS):
            hs = slice(h * RWKV_HEAD, (h + 1) * RWKV_HEAD)
            j = q * RWKV_HEADS + h
            w3[j] = decay[:, hs]
            kk3[j] = kk[:, hs]
            kka3[j] = kka[:, hs]
            k3[j] = k[:, hs]
            r3[j] = r[:, hs]
            v3[j] = v[:, hs]

    eye = (lax.broadcasted_iota(jnp.int32, (RWKV_HEAD, RWKV_HEAD), 0)
           == lax.broadcasted_iota(jnp.int32, (RWKV_HEAD, RWKV_HEAD), 1)).astype(f32)

    def step(t, carry):
        ts = pl.ds(t, 1)
        chains = [(q, h, q * RWKV_HEADS + h) for q in range(reqs) for h in range(RWKV_HEADS)]
        s_olds = [s_ref[q, h] for q, h, _ in chains]
        sas = [jnp.sum(s * kk3[j, ts, :], axis=1, keepdims=True) for s, (_, _, j) in zip(s_olds, chains)]
        v_cols = [jnp.sum(eye * v3[j, ts, :], axis=1, keepdims=True) for _, _, j in chains]
        s_news = [s * w3[j, ts, :] - sa * kka3[j, ts, :] + vc * k3[j, ts, :]
                  for s, sa, vc, (_, _, j) in zip(s_olds, sas, v_cols, chains)]
        for s_new, (q, h, _) in zip(s_news, chains):
            s_ref[q, h] = s_new
        y_cols = [jnp.sum(s * r3[j, ts, :], axis=1, keepdims=True) for s, (_, _, j) in zip(s_news, chains)]
        for y_col, (_, _, j) in zip(y_cols, chains):
            y3[j, ts, :] = jnp.sum(eye * y_col, axis=0, keepdims=True)
        return carry

    lax.fori_loop(0, steps, step, 0)

    for q in range(reqs):
        rows = slice(q * tb, (q + 1) * tb)
        y = jnp.concatenate([y3[q * RWKV_HEADS + h] for h in range(RWKV_HEADS)], axis=-1)
        o_ref[rows, :] = _rwkv_output(y, bonus_s[rows, :], g_s[rows, :], lg_ref, lb_ref, ones_ref)


def _unit_lower_inverse_minus_identity(ns):
    size = ns[0].shape[0]
    t = lax.broadcasted_iota(jnp.int32, (size, size), 0)
    s = lax.broadcasted_iota(jnp.int32, (size, size), 1)
    first = ((t >> 1) == (s >> 1)) & (t > s)
    es = [-jnp.where(first, n, 0.0) for n in ns]
    blk, shift = 4, 2
    while blk <= RWKV_HEAD:
        half = blk // 2
        sel = ((t >> shift) == (s >> shift)) & ((t & (blk - 1)) >= half) & ((s & (blk - 1)) < half)
        cs = [jnp.where(sel, n, 0.0) for n in ns]
        zs = [c + _dot(c.astype(bf16), e.astype(bf16)) for c, e in zip(cs, es)]
        es = [e - z - _dot(e.astype(bf16), z.astype(bf16)) for e, z in zip(es, zs)]
        blk, shift = blk * 2, shift + 1
    return es


def _rwkv_chunk_kernel(*refs, tb):
    x_refs, p_refs, w_refs = refs[0:4], refs[4:8], refs[8:18]
    lg_ref, lb_ref, ones_ref, s0_ref, o_ref, s_ref = refs[18:24]
    carries = refs[24:28]
    sp = refs[28]
    i = pl.program_id(1)
    n_pairs = RWKV_HEADS // 2
    hd = RWKV_HEAD

    @pl.when(i == 0)
    def _():
        zero = jnp.zeros((hd, hd), f32)
        for p in range(n_pairs):
            top = jnp.concatenate([s0_ref[0, 2 * p], zero], axis=1)
            bot = jnp.concatenate([zero, s0_ref[0, 2 * p + 1]], axis=1)
            sp[p] = jnp.concatenate([top, bot], axis=0)
        for carry, p_ref in zip(carries, p_refs):
            carry[...] = p_ref[0]

    xs = [x[...] for x in x_refs]
    prevs = [carry[...] for carry in carries]
    for carry, x in zip(carries, xs):
        carry[...] = x[tb - 1:tb, :]
    r, k, v, kk, a, lw, g, bonus = _rwkv_features(xs, prevs, w_refs, ones_ref)
    beta = kk * a
    row = lax.broadcasted_iota(jnp.int32, lw.shape, 0)
    cum = lw
    d = 1
    while d < tb:
        cum = cum + jnp.where(row >= d, pltpu.roll(cum, d, 0), 0.0)
        d *= 2
    c_last = cum[tb - 1:tb, :]
    a_t = kk * jnp.exp(cum - lw)
    r_t = r * jnp.exp(cum)
    e_neg = jnp.exp(-cum)
    k_t = k * e_neg
    b_t = beta * e_neg
    e_hat = jnp.exp(c_last - cum)
    k_h = k * e_hat
    b_h = beta * e_hat
    gamma = jnp.exp(c_last)

    lane = lax.broadcasted_iota(jnp.int32, (tb, LANES), 1)
    head0 = lane < hd
    split = lambda x: (jnp.where(head0, x, 0.0), jnp.where(head0, 0.0, x))
    tt = lax.broadcasted_iota(jnp.int32, (LANES, LANES), 0)
    ss = lax.broadcasted_iota(jnp.int32, (LANES, LANES), 1)
    strict = tt > ss
    incl = tt >= ss
    same_head = (tt >= hd) == (ss >= hd)
    fold = lambda m: m[:tb, :] + m[tb:, :]

    pairs = range(n_pairs)
    cols = [slice(p * LANES, (p + 1) * LANES) for p in pairs]
    stack = lambda *xs: jnp.concatenate(xs, axis=0)
    scs = [_dot_nt(stack(*split(a_t[:, ps]), *split(r_t[:, ps])).astype(bf16),
                   stack(*split(k_t[:, ps]), *split(b_t[:, ps])).astype(bf16)) for ps in cols]
    es = _unit_lower_inverse_minus_identity([jnp.where(strict, sc[:2 * tb, 2 * tb:], 0.0) for sc in scs])
    eye = jnp.where(tt == ss, 1.0, 0.0)
    t_cats = [fold(eye + e).astype(bf16) for e in es]
    ak_cats = [fold(jnp.where(strict, sc[:2 * tb, :2 * tb], 0.0)).astype(bf16) for sc in scs]
    r_cats = [jnp.concatenate([fold(jnp.where(incl, sc[2 * tb:, :2 * tb], 0.0)),
                               -fold(jnp.where(incl, sc[2 * tb:, 2 * tb:], 0.0))], axis=1).astype(bf16)
              for sc in scs]
    s_olds = [sp[p] for p in pairs]
    grs = [_dot_nt(stack(a_t[:, ps], r_t[:, ps]).astype(bf16), s_old.astype(bf16))
           for ps, s_old in zip(cols, s_olds)]
    v_sts = [stack(*split(v[:, ps])).astype(bf16) for ps in cols]
    u_rhss = [gr[:tb] + _dot(ak, v_st) for gr, ak, v_st in zip(grs, ak_cats, v_sts)]
    us = [_dot(t_cat, stack(*split(u_rhs)).astype(bf16)) for t_cat, u_rhs in zip(t_cats, u_rhss)]
    ys = [gr[tb:] + _dot(r_cat, stack(v_st, stack(*split(u)).astype(bf16)))
          for gr, r_cat, v_st, u in zip(grs, r_cats, v_sts, us)]
    for p, ps, s_old, u in zip(pairs, cols, s_olds, us):
        vu_t = stack(v[:, ps], -u).T.astype(bf16)
        kb = stack(k_h[:, ps], b_h[:, ps]).astype(bf16)
        sp[p] = s_old * gamma[:, ps] + jnp.where(same_head, _dot(vu_t, kb), 0.0)

    o_ref[...] = _rwkv_output(jnp.concatenate(ys, axis=-1), bonus, g, lg_ref, lb_ref, ones_ref)

    @pl.when(i == pl.num_programs(1) - 1)
    def _():
        for p in range(n_pairs):
            full = sp[p]
            s_ref[0, 2 * p] = full[:hd, :hd]
            s_ref[0, 2 * p + 1] = full[hd:, hd:]


def rwkv(u, prev, p, s0, layer, ones2, batch, l_pad, l_real, chunked, reqs=1):
    m = u.shape[0]
    bw = BRANCH_WIDTH
    full = lambda w: pl.BlockSpec((1, w), lambda b, i: (0, 0))
    if chunked:
        tb, nb, nblk = RWKV_HEAD, 1, l_pad // RWKV_HEAD
        assert l_real == l_pad and l_pad % tb == 0
        body = functools.partial(_rwkv_chunk_kernel, tb=tb)
        scratch = [pltpu.VMEM((1, bw), f32), pltpu.VMEM((1, bw), f32), pltpu.VMEM((1, bw), f32),
                   pltpu.VMEM((1, RWKV_LORA), f32), pltpu.VMEM((RWKV_HEADS // 2, LANES, LANES), f32)]
    else:
        tb, nb, nblk = l_pad, reqs, 1
        assert batch % reqs == 0
        body = functools.partial(_rwkv_seq_kernel, tb=tb, steps=l_real, reqs=reqs)
        head_buf = pltpu.VMEM((reqs * RWKV_HEADS, tb, RWKV_HEAD), f32)
        scratch = [head_buf] * 7 + [pltpu.VMEM((reqs * tb, bw), f32), pltpu.VMEM((reqs * tb, bw), f32)]
    rows = nb * tb
    row = lambda b, i: b * nblk + i
    prev_spec = lambda w: pl.BlockSpec((None, nb, 1, w), lambda b, i: (layer, b, 0, 0))
    return pl.pallas_call(
        body,
        grid=(batch // nb, nblk),
        in_specs=[pl.BlockSpec((rows, bw), lambda b, i: (row(b, i), C_R // bw)),
                  pl.BlockSpec((rows, bw), lambda b, i: (row(b, i), C_K // bw)),
                  pl.BlockSpec((rows, bw), lambda b, i: (row(b, i), C_V // bw)),
                  pl.BlockSpec((rows, RWKV_LORA), lambda b, i: (row(b, i), C_LORA // RWKV_LORA)),
                  prev_spec(bw), prev_spec(bw), prev_spec(bw), prev_spec(RWKV_LORA),
                  full(bw), full(bw), full(bw), full(RWKV_LORA),
                  full(bw), full(bw),
                  pl.BlockSpec((RWKV_LORA, 3 * bw), lambda b, i: (0, 0)),
                  full(bw), full(bw), full(bw), full(bw), full(bw),
                  pl.BlockSpec((LANES, LANES), lambda b, i: (0, 0)),
                  pl.BlockSpec((None, nb, RWKV_HEADS, RWKV_HEAD, RWKV_HEAD), lambda b, i: (layer, b, 0, 0, 0))],
        out_specs=[pl.BlockSpec((rows, bw), lambda b, i: (row(b, i), 0)),
                   pl.BlockSpec((nb, RWKV_HEADS, RWKV_HEAD, RWKV_HEAD), lambda b, i: (b, 0, 0, 0))],
        out_shape=[jax.ShapeDtypeStruct((m, bw), bf16),
                   jax.ShapeDtypeStruct((batch, RWKV_HEADS, RWKV_HEAD, RWKV_HEAD), f32)],
        scratch_shapes=scratch,
        compiler_params=_params(("parallel", "arbitrary")),
        name="rwkv7_chunked" if chunked else "rwkv7_seq",
    )(u, u, u, u, *prev, p['mu_r'], p['mu_k'], p['mu_v'], p['mu_lo'], p['w0'], p['a0'], p['w_lora'],
      p['k_k'], p['k_a'], p['r_k'], p['ln_g'], p['ln_b'], ones2, s0)


def _pack_w_in_t(w_in):
    wt = jnp.swapaxes(w_in, 1, 2)
    seg = lambda o, n: wt[:, o:o + n]
    parts = [seg(_O_GV, 1024), seg(_O_GR, 1024), seg(_O_SQ, 1024), seg(_O_MQ, 1024),
             seg(_O_RU, 3072), seg(_O_GPRE, 8192), seg(_O_GQ, 512), seg(_O_GK, 512),
             seg(_O_RU + 3072, RWKV_LORA), seg(_O_SK, 128), seg(_O_SV, 128), seg(_O_GA, GLA_GATE_RANK),
             jnp.zeros((wt.shape[0], N_PACK - C_GA - GLA_GATE_RANK, wt.shape[2]), wt.dtype)]
    return jnp.concatenate(parts, axis=1).astype(bf16)


def _layer_params(l, w):
    row = lambda x: x[l].reshape(1, -1)
    mu = w['rwkv_mu'][l]
    z = lambda r, c: jnp.zeros((r, c), f32)
    w_lora = jnp.concatenate([
        jnp.concatenate([w['rwkv_w2'][l], z(64, 1024), z(64, 1024)], axis=1),
        jnp.concatenate([z(64, 1024), w['rwkv_a2'][l], z(64, 1024)], axis=1),
        jnp.concatenate([z(128, 1024), z(128, 1024), w['rwkv_g2'][l]], axis=1)], axis=0).astype(bf16)
    a_up = jnp.concatenate([w['gla_a_up'][l], z(LANES - GLA_GATE_RANK, GLA_QK)], axis=0).astype(bf16)
    return {
        'a_up': a_up, 'a_b': row(w['gla_a_b']),
        'gla_g': row(w['gla_norm_g']), 'gla_b': row(w['gla_norm_b']),
        'sinks': w['swa_sinks'][l],
        'rwkv': {'mu_r': mu[:1024].reshape(1, -1), 'mu_k': mu[1024:2048].reshape(1, -1),
                 'mu_v': mu[2048:3072].reshape(1, -1), 'mu_lo': mu[3072:].reshape(1, -1),
                 'w0': row(w['rwkv_w0']), 'a0': row(w['rwkv_a0']), 'w_lora': w_lora,
                 'k_k': row(w['rwkv_k_k']), 'k_a': row(w['rwkv_k_a']), 'r_k': row(w['rwkv_r_k']),
                 'ln_g': row(w['rwkv_ln_g']), 'ln_b': row(w['rwkv_ln_b'])},
        'ln1_g': row(w['ln1_g']), 'ln1_b': row(w['ln1_b']),
        'ln2_g': row(w['ln2_g']), 'ln2_b': row(w['ln2_b']),
    }


def _tiles(m):
    if m >= 1024:
        return {'proj': 2048, 'merge': 512, 'ffn': 2048, 'ln': 256}
    return {'proj': m, 'merge': m, 'ffn': m, 'ln': m}


def _trunk_layer(h_f, h_b, l, p, big, ones2, mem_kv, mem_cache, gla_s0, rwkv_s0, rwkv_prev, swa_buf,
                 batch, l_pad, l_real):
    m = h_f.shape[0]
    tl = _tiles(m)
    u = matmul(h_b, big['w_in_t'], l, tl['proj'], 1024, "in_proj", trans_w=True)
    o_a, gla_s = gla(u, p['a_up'], p['a_b'], p['gla_g'], p['gla_b'], gla_s0[0], gla_s0[1], batch, l_pad, l_real,
                     min(GLA_CHUNK, l_pad))
    if swa_buf is None:
        o_b = swa_prompt(u, p['sinks'], batch, l_pad)
    else:
        o_b = swa_sample(u, swa_buf[0], swa_buf[1], l, p['sinks'], batch, l_pad, l_real, 4)
    chunked = l_real == l_pad and l_pad % RWKV_HEAD == 0
    o_c, rwkv_s = rwkv(u, rwkv_prev[0], p['rwkv'], rwkv_s0[0], rwkv_s0[1], ones2, batch, l_pad, l_real,
                       chunked, reqs=1 if chunked else 4)
    o_m = mem_attention(u, mem_kv, mem_cache, l, batch, l_pad, min(512, l_pad))
    merged = gated_merge((o_a, o_b, o_c, o_m), u, big['gate_b'], big['w_branch'], l, tl['merge'], 512)
    x_f, x_b = matmul_residual_ln(merged, big['w_out'], l, h_f, p['ln1_g'], p['ln1_b'], tl['ln'])
    act = ffn_up(x_b, big['w_gu'], l, tl['ffn'], 512)
    y_f, y_b = matmul_residual_ln(act, big['w_down'], l, x_f, p['ln2_g'], p['ln2_b'], tl['ln'])
    return y_f, y_b, u, gla_s, rwkv_s


def _split_ru(x):
    return (x[..., :1024], x[..., 1024:2048], x[..., 2048:3072], x[..., 3072:])


def kernel(x_prompt, x_sample, mem_prompt, cache_swa_k, cache_swa_v, cache_mem_k, cache_mem_v, state_gla, state_rwkv, state_rwkv_shift, w_in, gate_b, gla_a_up, gla_a_b, gla_norm_g, gla_norm_b, swa_sinks, rwkv_mu, rwkv_w0, rwkv_w2, rwkv_a0, rwkv_a2, rwkv_g2, rwkv_k_k, rwkv_k_a, rwkv_r_k, rwkv_ln_g, rwkv_ln_b, w_mem_kv, w_branch, w_out, ln1_g, ln1_b, w_gu, w_down, ln2_g, ln2_b):
    weights = {'gla_a_up': gla_a_up, 'gla_a_b': gla_a_b,
               'gla_norm_g': gla_norm_g, 'gla_norm_b': gla_norm_b, 'swa_sinks': swa_sinks,
               'rwkv_mu': rwkv_mu, 'rwkv_w0': rwkv_w0, 'rwkv_w2': rwkv_w2, 'rwkv_a0': rwkv_a0,
               'rwkv_a2': rwkv_a2, 'rwkv_g2': rwkv_g2, 'rwkv_k_k': rwkv_k_k, 'rwkv_k_a': rwkv_k_a,
               'rwkv_r_k': rwkv_r_k, 'rwkv_ln_g': rwkv_ln_g, 'rwkv_ln_b': rwkv_ln_b,
               'ln1_g': ln1_g, 'ln1_b': ln1_b, 'ln2_g': ln2_g, 'ln2_b': ln2_b}
    big = {'w_in_t': _pack_w_in_t(w_in), 'gate_b': gate_b, 'w_branch': w_branch, 'w_gu': w_gu,
           'w_out': w_out.astype(bf16), 'w_down': w_down.astype(bf16)}
    w_mem_kv_b = w_mem_kv.astype(bf16)
    bp, lp, _ = x_prompt.shape
    bs, ls, _ = x_sample.shape
    ls_pad = -(-ls // SUBLANES) * SUBLANES
    mp, ms = bp * lp, bs * ls_pad

    hp_f = x_prompt.reshape(mp, D_MODEL)
    hs_f = jnp.pad(x_sample, ((0, 0), (0, ls_pad - ls), (0, 0))).reshape(ms, D_MODEL)
    hp_b, hs_b = hp_f.astype(bf16), hs_f.astype(bf16)
    mem_b = mem_prompt.reshape(bp * MEM_TOKENS, D_MODEL).astype(bf16)
    half = LANES // 2
    blk = jnp.ones((half, half), f32)
    zero = jnp.zeros((half, half), f32)
    ones2 = jnp.concatenate([jnp.concatenate([blk, zero], 1), jnp.concatenate([zero, blk], 1)], 0).astype(bf16)

    gla0_p = jnp.zeros((1, bp, GLA_HEADS, GLA_DK, GLA_DV), f32)
    rwkv0_p = jnp.zeros((1, bp, RWKV_HEADS, RWKV_HEAD, RWKV_HEAD), f32)
    prev0_p = _split_ru(jnp.zeros((1, bp, 1, RWKV_COLS), f32))
    prev_s = _split_ru(state_rwkv_shift)
    kvw = SWA_KV_HEADS * SWA_HEAD_DIM
    kbuf = cache_swa_k.reshape(DEPTH, bs * WINDOW, kvw)
    vbuf = cache_swa_v.reshape(DEPTH, bs * WINDOW, kvw)

    outs = {k: [] for k in ('p_swk', 'p_swv', 'p_mk', 'p_mv', 'p_gla', 'p_rw', 'p_rs',
                            's_swk', 's_swv', 's_gla', 's_rw', 's_rs')}
    for l in range(DEPTH):
        p = _layer_params(l, weights)
        kv = matmul(mem_b, w_mem_kv_b, l, bp * MEM_TOKENS, 512, "mem_kv")
        hp_f, hp_b, u, gs, rs = _trunk_layer(hp_f, hp_b, l, p, big, ones2, kv, None, (gla0_p, 0), (rwkv0_p, 0),
                                             (prev0_p, 0), None, bp, lp, lp)
        u3 = u.reshape(bp, lp, N_PACK)
        outs['p_swk'].append(u3[:, lp - WINDOW:, C_SK:C_SK + kvw].reshape(bp, WINDOW, SWA_KV_HEADS, SWA_HEAD_DIM))
        outs['p_swv'].append(u3[:, lp - WINDOW:, C_SV:C_SV + kvw].reshape(bp, WINDOW, SWA_KV_HEADS, SWA_HEAD_DIM))
        outs['p_mk'].append(kv[:, :1024].reshape(bp, MEM_TOKENS, MEM_HEADS, MEM_HEAD_DIM))
        outs['p_mv'].append(kv[:, 1024:].reshape(bp, MEM_TOKENS, MEM_HEADS, MEM_HEAD_DIM))
        outs['p_gla'].append(gs)
        outs['p_rw'].append(rs)
        outs['p_rs'].append(jnp.concatenate([u3[:, lp - 1:, C_R:C_R + 3072],
                                             u3[:, lp - 1:, C_LORA:C_LORA + RWKV_LORA]], axis=-1))
        hs_f, hs_b, u, gs, rs = _trunk_layer(hs_f, hs_b, l, p, big, ones2, None, (cache_mem_k, cache_mem_v),
                                             (state_gla, l), (state_rwkv, l), (prev_s, l), (kbuf, vbuf),
                                             bs, ls_pad, ls)
        u3 = u.reshape(bs, ls_pad, N_PACK)
        k_new = u3[:, :ls, C_SK:C_SK + kvw].reshape(bs, ls, SWA_KV_HEADS, SWA_HEAD_DIM)
        v_new = u3[:, :ls, C_SV:C_SV + kvw].reshape(bs, ls, SWA_KV_HEADS, SWA_HEAD_DIM)
        outs['s_swk'].append(jnp.concatenate([cache_swa_k[l][:, ls:], k_new], axis=1))
        outs['s_swv'].append(jnp.concatenate([cache_swa_v[l][:, ls:], v_new], axis=1))
        outs['s_gla'].append(gs)
        outs['s_rw'].append(rs)
        outs['s_rs'].append(jnp.concatenate([u3[:, ls - 1:ls, C_R:C_R + 3072],
                                             u3[:, ls - 1:ls, C_LORA:C_LORA + RWKV_LORA]], axis=-1))

    st = {k: jnp.stack(v) for k, v in outs.items()}
    y_prompt = hp_f.reshape(bp, lp, D_MODEL)
    y_sample = hs_f.reshape(bs, ls_pad, D_MODEL)[:, :ls]
    return (y_prompt, y_sample,
            st['p_swk'], st['p_swv'], st['p_mk'], st['p_mv'], st['p_gla'], st['p_rw'], st['p_rs'],
            st['s_swk'], st['s_swv'], st['s_gla'], st['s_rw'], st['s_rs'])
```

```python
import functools

import jax
import jax.numpy as jnp
from jax import lax
from jax.experimental import pallas as pl
from jax.experimental.pallas import tpu as pltpu

f32 = jnp.float32
bf16 = jnp.bfloat16

D_MODEL = 2048
DEPTH = 4
BRANCH_WIDTH = 1024
N_BRANCH = 4
GLA_HEADS = 4
GLA_QK = 512
GLA_V = 1024
GLA_DK = 128
GLA_DV = 256
GLA_GATE_RANK = 16
GLA_TAU = 16.0
GLA_CHUNK = 64
SWA_HEAD_DIM = 64
SWA_Q_HEADS = 16
SWA_KV_HEADS = 2
SWA_GROUP = SWA_Q_HEADS // SWA_KV_HEADS
WINDOW = 128
RWKV_HEAD = 64
RWKV_HEADS = 16
RWKV_LORA = 256
RWKV_COLS = 3 * BRANCH_WIDTH + RWKV_LORA
MEM_TOKENS = 256
MEM_HEADS = 4
MEM_HEAD_DIM = 256
D_FF = 5632
DEEPNORM_ALPHA = (2 * DEPTH) ** 0.25
NEG = -1e30

_O_GQ, _O_GK, _O_GV, _O_GR, _O_GA = 0, 512, 1024, 2048, 3072
_O_SQ, _O_SK, _O_SV = 3088, 4112, 4240
_O_RU = 4368
_O_MQ = 7696
_O_GPRE = 8720
_N_IN = 16912

C_GV, C_GR, C_SQ, C_MQ, C_R, C_K, C_V = 0, 1024, 2048, 3072, 4096, 5120, 6144
C_GPRE = 7168
C_GQ, C_GK = 15360, 15872
C_LORA = 16384
C_SK, C_SV, C_GA = 16640, 16768, 16896
N_PACK = 17408

LANES = 128
SUBLANES = 8
VMEM_BYTES_V7X = 64 * 1024 * 1024
VMEM_LIMIT = VMEM_BYTES_V7X - 8 * 1024 * 1024


def _params(sem):
    return pltpu.CompilerParams(dimension_semantics=sem, vmem_limit_bytes=VMEM_LIMIT)


def _softplus(z):
    return jnp.maximum(z, 0.0) + jnp.log(1.0 + jnp.exp(-jnp.abs(z)))


def _sigmoid(z):
    return 0.5 * jnp.tanh(0.5 * z) + 0.5


def _dot(a, b):
    return jnp.dot(a, b, preferred_element_type=f32)


def _dot_nt(a, b):
    return lax.dot_general(a, b, (((1,), (1,)), ((), ())), preferred_element_type=f32)


def _mm_kernel(x_ref, w_ref, o_ref, *, trans_w):
    mm = _dot_nt if trans_w else _dot
    o_ref[...] = mm(x_ref[...], w_ref[...])


def matmul(x, w, layer, tm, tn, name, trans_w=False):
    m, k = x.shape
    n = w.shape[1] if trans_w else w.shape[2]
    w_spec = (pl.BlockSpec((None, tn, k), lambda i, j: (layer, j, 0)) if trans_w
              else pl.BlockSpec((None, k, tn), lambda i, j: (layer, 0, j)))
    return pl.pallas_call(
        functools.partial(_mm_kernel, trans_w=trans_w),
        grid=(m // tm, n // tn),
        in_specs=[pl.BlockSpec((tm, k), lambda i, j: (i, 0)), w_spec],
        out_specs=pl.BlockSpec((tm, tn), lambda i, j: (i, j)),
        out_shape=jax.ShapeDtypeStruct((m, n), f32),
        compiler_params=_params(("parallel", "parallel")),
        name=name,
    )(x, w)


def _mm_ln_kernel(x_ref, w_ref, res_ref, g_ref, b_ref, of_ref, ob_ref):
    z = DEEPNORM_ALPHA * res_ref[...] + _dot(x_ref[...], w_ref[...])
    mu = jnp.mean(z, axis=-1, keepdims=True)
    d = z - mu
    var = jnp.mean(d * d, axis=-1, keepdims=True)
    y = d * lax.rsqrt(var + 1e-5) * g_ref[...] + b_ref[...]
    of_ref[...] = y
    ob_ref[...] = y.astype(bf16)


def matmul_residual_ln(x, w, layer, res, g, b, tm):
    m, k = x.shape
    n = w.shape[2]
    return pl.pallas_call(
        _mm_ln_kernel,
        grid=(m // tm,),
        in_specs=[pl.BlockSpec((tm, k), lambda i: (i, 0)),
                  pl.BlockSpec((None, k, n), lambda i: (layer, 0, 0), pipeline_mode=pl.Buffered(1)),
                  pl.BlockSpec((tm, n), lambda i: (i, 0)),
                  pl.BlockSpec((1, n), lambda i: (0, 0)),
                  pl.BlockSpec((1, n), lambda i: (0, 0))],
        out_specs=[pl.BlockSpec((tm, n), lambda i: (i, 0)),
                   pl.BlockSpec((tm, n), lambda i: (i, 0))],
        out_shape=[jax.ShapeDtypeStruct((m, n), f32), jax.ShapeDtypeStruct((m, n), bf16)],
        compiler_params=_params(("parallel",)),
        name="proj_ln",
    )(x, w, res, g, b)


def _merge_kernel(a_ref, b_ref, c_ref, m_ref, g0_ref, g1_ref, g2_ref, g3_ref, gb_ref, w_ref, o_ref, w_bf):
    @pl.when(pl.program_id(1) == 0)
    def _():
        w_bf[...] = w_ref[...].astype(bf16)

    acc = None
    for n, (br, gp) in enumerate(((a_ref, g0_ref), (b_ref, g1_ref), (c_ref, g2_ref), (m_ref, g3_ref))):
        y = _dot(br[...], w_bf[n])
        gate = _sigmoid(gp[...] + gb_ref[n:n + 1, :])
        acc = gate * y if acc is None else acc + gate * y
    o_ref[...] = acc.astype(bf16)


def gated_merge(branches, u, gate_b, w_branch, layer, tm, tn):
    m = u.shape[0]
    gp0 = C_GPRE // tn
    per = D_MODEL // tn
    br_spec = pl.BlockSpec((tm, BRANCH_WIDTH), lambda j, i: (i, 0))
    gp_specs = [pl.BlockSpec((tm, tn), functools.partial(lambda j, i, n: (i, gp0 + n * per + j), n=n))
                for n in range(N_BRANCH)]
    return pl.pallas_call(
        _merge_kernel,
        grid=(D_MODEL // tn, m // tm),
        in_specs=[br_spec] * 4 + gp_specs + [
            pl.BlockSpec((None, N_BRANCH, tn), lambda j, i: (layer, 0, j)),
            pl.BlockSpec((None, N_BRANCH, BRANCH_WIDTH, tn), lambda j, i: (layer, 0, 0, j))],
        out_specs=pl.BlockSpec((tm, tn), lambda j, i: (i, j)),
        out_shape=jax.ShapeDtypeStruct((m, D_MODEL), bf16),
        scratch_shapes=[pltpu.VMEM((N_BRANCH, BRANCH_WIDTH, tn), bf16)],
        compiler_params=_params(("parallel", "arbitrary")),
        name="gated_merge",
    )(*branches, u, u, u, u, gate_b, w_branch)


def _ffn_up_kernel(x_ref, wg_ref, wu_ref, o_ref, wg_bf, wu_bf):
    @pl.when(pl.program_id(1) == 0)
    def _():
        wg_bf[...] = wg_ref[...].astype(bf16)
        wu_bf[...] = wu_ref[...].astype(bf16)

    x = x_ref[...]
    g = _dot(x, wg_bf[...])
    up = _dot(x, wu_bf[...])
    o_ref[...] = (g * _sigmoid(g) * up).astype(bf16)


def ffn_up(x, w_gu, layer, tm, tn):
    m, k = x.shape
    nj = D_FF // tn
    return pl.pallas_call(
        _ffn_up_kernel,
        grid=(nj, m // tm),
        in_specs=[pl.BlockSpec((tm, k), lambda j, i: (i, 0)),
                  pl.BlockSpec((None, k, tn), lambda j, i: (layer, 0, j)),
                  pl.BlockSpec((None, k, tn), lambda j, i: (layer, 0, nj + j))],
        out_specs=pl.BlockSpec((tm, tn), lambda j, i: (i, j)),
        out_shape=jax.ShapeDtypeStruct((m, D_FF), bf16),
        scratch_shapes=[pltpu.VMEM((k, tn), bf16), pltpu.VMEM((k, tn), bf16)],
        compiler_params=_params(("parallel", "arbitrary")),
        name="ffn_up",
    )(x, w_gu, w_gu)


def _gla_kernel(q_ref, k_ref, v_ref, r_ref, a_ref, aup_ref, ab_ref, ng_ref, nb_ref, s0_ref,
                o_ref, s_ref, *, chunk, l_real, l_pad):
    c = pl.program_id(1)

    @pl.when(c == 0)
    def _():
        s_ref[...] = s0_ref[...]

    la = _dot(a_ref[...].astype(bf16), aup_ref[...]) + ab_ref[...]
    la = (jnp.minimum(la, 0.0) - jnp.log(1.0 + jnp.exp(-jnp.abs(la)))) * (1.0 / GLA_TAU)
    row = lax.broadcasted_iota(jnp.int32, (chunk, GLA_QK), 0)
    kin = k_ref[...]
    if l_real < l_pad:
        real = (c * chunk + row) < l_real
        la = jnp.where(real, la, 0.0)
        kin = jnp.where(real, kin, 0.0)
    b = la
    d = 1
    while d < chunk:
        b = b + jnp.where(row >= d, pltpu.roll(b, d, 0), 0.0)
        d *= 2
    b_last = b[chunk - 1:chunk, :]
    q_dec = q_ref[...] * (GLA_DK ** -0.5) * jnp.exp(b)
    k_inv = kin * jnp.exp(-b)
    k_dec = kin * jnp.exp(b_last - b)
    tt = lax.broadcasted_iota(jnp.int32, (chunk, chunk), 0)
    ss = lax.broadcasted_iota(jnp.int32, (chunk, chunk), 1)
    causal = tt >= ss
    heads = range(GLA_HEADS)
    kss = [slice(h * GLA_DK, (h + 1) * GLA_DK) for h in heads]
    vss = [slice(h * GLA_DV, (h + 1) * GLA_DV) for h in heads]
    qhs = [q_dec[:, ks].astype(bf16) for ks in kss]
    atts = [jnp.where(causal, _dot_nt(qh, k_inv[:, ks].astype(bf16)), 0.0).astype(bf16) for qh, ks in zip(qhs, kss)]
    vhs = [v_ref[:, vs].astype(bf16) for vs in vss]
    s_olds = [s_ref[0, h] for h in heads]
    os_ = [_dot(att, vh) + _dot(qh, s_old.astype(bf16)) for att, vh, qh, s_old in zip(atts, vhs, qhs, s_olds)]
    decay_cols = [jnp.exp(jnp.sum(la[:, ks].T, axis=1, keepdims=True)) for ks in kss]
    for h, ks, vh, s_old, decay_col in zip(heads, kss, vhs, s_olds, decay_cols):
        s_ref[0, h] = decay_col * s_old + _dot(k_dec[:, ks].T.astype(bf16), vh)
    for vs, o in zip(vss, os_):
        mu = jnp.mean(o, axis=-1, keepdims=True)
        dd = o - mu
        var = jnp.mean(dd * dd, axis=-1, keepdims=True)
        nrm = dd * lax.rsqrt(var + 1e-5) * ng_ref[:, vs] + nb_ref[:, vs]
        gr = r_ref[:, vs]
        o_ref[:, vs] = (nrm * (gr * _sigmoid(gr))).astype(bf16)


def gla(u, a_up, a_b, norm_g, norm_b, s0, layer, batch, l_pad, l_real, chunk):
    m = u.shape[0]
    nc = l_pad // chunk
    row = lambda b, c: b * nc + c
    return pl.pallas_call(
        functools.partial(_gla_kernel, chunk=chunk, l_real=l_real, l_pad=l_pad),
        grid=(batch, nc),
        in_specs=[pl.BlockSpec((chunk, GLA_QK), lambda b, c: (row(b, c), C_GQ // GLA_QK)),
                  pl.BlockSpec((chunk, GLA_QK), lambda b, c: (row(b, c), C_GK // GLA_QK)),
                  pl.BlockSpec((chunk, GLA_V), lambda b, c: (row(b, c), C_GV // GLA_V)),
                  pl.BlockSpec((chunk, GLA_V), lambda b, c: (row(b, c), C_GR // GLA_V)),
                  pl.BlockSpec((chunk, LANES), lambda b, c: (row(b, c), C_GA // LANES)),
                  pl.BlockSpec((LANES, GLA_QK), lambda b, c: (0, 0)),
                  pl.BlockSpec((1, GLA_QK), lambda b, c: (0, 0)),
                  pl.BlockSpec((1, GLA_V), lambda b, c: (0, 0)),
                  pl.BlockSpec((1, GLA_V), lambda b, c: (0, 0)),
                  pl.BlockSpec((None, 1, GLA_HEADS, GLA_DK, GLA_DV), lambda b, c: (layer, b, 0, 0, 0))],
        out_specs=[pl.BlockSpec((chunk, GLA_V), lambda b, c: (row(b, c), 0)),
                   pl.BlockSpec((1, GLA_HEADS, GLA_DK, GLA_DV), lambda b, c: (b, 0, 0, 0))],
        out_shape=[jax.ShapeDtypeStruct((m, GLA_V), bf16),
                   jax.ShapeDtypeStruct((batch, GLA_HEADS, GLA_DK, GLA_DV), f32)],
        compiler_params=_params(("parallel", "arbitrary")),
        name="gla",
    )(u, u, u, u, u, a_up, a_b, norm_g, norm_b, s0)


def _alibi_slopes():
    return 2.0 ** (-8.0 * jnp.arange(1, SWA_Q_HEADS + 1, dtype=f32) / SWA_Q_HEADS)


def _swa_bias(t_pos, key_pos, key_ok):
    dist = (t_pos[:, None] - key_pos[None, :]).astype(f32)
    valid = (dist >= 0) & (dist <= WINDOW) & key_ok[None, :]
    slopes = _alibi_slopes().reshape(SWA_KV_HEADS, SWA_GROUP, 1, 1)
    bias = jnp.where(valid[None, None], -slopes * dist[None, None], NEG)
    return bias.reshape(SWA_KV_HEADS, SWA_GROUP * t_pos.shape[0], key_pos.shape[0])


def _swa_groups(q_all, keys, vals, biases, sink_ref, rows):
    kvs = range(SWA_KV_HEADS)
    d = SWA_HEAD_DIM
    pairs = SWA_GROUP // 2
    heads = [range(kv * SWA_GROUP, (kv + 1) * SWA_GROUP) for kv in kvs]
    low = lax.broadcasted_iota(jnp.int32, (rows, LANES), 1) < d

    def both_halves(x, kv):
        swapped = pltpu.roll(x, d, 1)
        first = lax.broadcasted_iota(jnp.int32, x.shape, 1) < d
        return jnp.where(first, x, swapped) if kv == 0 else jnp.where(first, swapped, x)

    def stacked_queries(kv):
        parts = []
        for p in range(kv * pairs, (kv + 1) * pairs):
            q_pair = q_all[:, p * LANES:(p + 1) * LANES]
            parts += [jnp.where(low, q_pair, 0.0), jnp.where(low, 0.0, q_pair)]
        return jnp.concatenate(parts, axis=0).astype(bf16)

    qs = [stacked_queries(kv) for kv in kvs]
    k2s = [jnp.concatenate([both_halves(kb, kv) for kb in keys], axis=0).astype(bf16) for kv in kvs]
    v2s = [jnp.concatenate([both_halves(vb, kv) for vb in vals], axis=0).astype(bf16) for kv in kvs]
    ss = [_dot_nt(q, k2) * (d ** -0.5) + bias for q, k2, bias in zip(qs, k2s, biases)]
    sinks = [jnp.concatenate([jnp.full((rows, 1), sink_ref[h], f32) for h in hh], axis=0) for hh in heads]
    ms = [jnp.maximum(jnp.max(s, axis=-1, keepdims=True), sink) for s, sink in zip(ss, sinks)]
    ps = [jnp.exp(s - m) for s, m in zip(ss, ms)]
    dens = [jnp.sum(p, axis=-1, keepdims=True) + jnp.exp(sink - m) for p, sink, m in zip(ps, sinks, ms)]
    os_ = [_dot(p.astype(bf16), v2) / den for p, v2, den in zip(ps, v2s, dens)]
    return [jnp.where(low, o[(2 * p) * rows:(2 * p + 1) * rows, :], o[(2 * p + 1) * rows:(2 * p + 2) * rows, :])
            for o in os_ for p in range(pairs)]


def _swa_prompt_kernel(sink_ref, bias_ref, q_ref, kc_ref, kp_ref, vc_ref, vp_ref, o_ref):
    keys = (kp_ref[...], kc_ref[...])
    vals = (vp_ref[...], vc_ref[...])
    biases = [bias_ref[0, kv] for kv in range(SWA_KV_HEADS)]
    outs = _swa_groups(q_ref[...], keys, vals, biases, sink_ref, WINDOW)
    o_ref[...] = jnp.concatenate(outs, axis=-1).astype(bf16)


def swa_prompt(u, sinks, batch, seq):
    m = u.shape[0]
    nb = seq // WINDOW
    cur = lambda b, i: b * nb + i
    prev = lambda b, i: b * nb + jnp.maximum(i - 1, 0)
    kcol, vcol = C_SK // LANES, C_SV // LANES
    t_pos = jnp.arange(WINDOW)
    key_pos = jnp.arange(2 * WINDOW) - WINDOW
    bias = jnp.stack([_swa_bias(t_pos, key_pos, key_pos >= 0), _swa_bias(t_pos, key_pos, key_pos >= -WINDOW)])
    return pl.pallas_call(
        _swa_prompt_kernel,
        grid=(batch, nb),
        in_specs=[pl.BlockSpec(memory_space=pltpu.SMEM),
                  pl.BlockSpec((1,) + bias.shape[1:], lambda b, i: (jnp.minimum(i, 1), 0, 0, 0)),
                  pl.BlockSpec((WINDOW, BRANCH_WIDTH), lambda b, i: (cur(b, i), C_SQ // BRANCH_WIDTH)),
                  pl.BlockSpec((WINDOW, LANES), lambda b, i: (cur(b, i), kcol)),
                  pl.BlockSpec((WINDOW, LANES), lambda b, i: (prev(b, i), kcol)),
                  pl.BlockSpec((WINDOW, LANES), lambda b, i: (cur(b, i), vcol)),
                  pl.BlockSpec((WINDOW, LANES), lambda b, i: (prev(b, i), vcol))],
        out_specs=pl.BlockSpec((WINDOW, BRANCH_WIDTH), lambda b, i: (cur(b, i), 0)),
        out_shape=jax.ShapeDtypeStruct((m, BRANCH_WIDTH), bf16),
        compiler_params=_params(("parallel", "parallel")),
        name="swa_prompt",
    )(sinks, bias, u, u, u, u, u)


def _swa_sample_kernel(sink_ref, bias_ref, q_ref, kn_ref, vn_ref, kb_ref, vb_ref, o_ref, *, l_pad, reqs):
    for r in range(reqs):
        rs = slice(r * l_pad, (r + 1) * l_pad)
        bs = slice(r * WINDOW, (r + 1) * WINDOW)
        keys = (kb_ref[bs, :], kn_ref[rs, :])
        vals = (vb_ref[bs, :], vn_ref[rs, :])
        biases = [bias_ref[kv] for kv in range(SWA_KV_HEADS)]
        outs = _swa_groups(q_ref[rs, :], keys, vals, biases, sink_ref, l_pad)
        o_ref[rs, :] = jnp.concatenate(outs, axis=-1).astype(bf16)


def swa_sample(u, kbuf, vbuf, layer, sinks, batch, l_pad, l_real, reqs):
    m = u.shape[0]
    kcol, vcol = C_SK // LANES, C_SV // LANES
    key_pos = jnp.concatenate([jnp.arange(WINDOW) - WINDOW, jnp.arange(l_pad)])
    key_ok = jnp.concatenate([jnp.ones((WINDOW,), bool), jnp.arange(l_pad) < l_real])
    bias = _swa_bias(jnp.arange(l_pad), key_pos, key_ok)
    return pl.pallas_call(
        functools.partial(_swa_sample_kernel, l_pad=l_pad, reqs=reqs),
        grid=(batch // reqs,),
        in_specs=[pl.BlockSpec(memory_space=pltpu.SMEM),
                  pl.BlockSpec(bias.shape, lambda b: (0, 0, 0)),
                  pl.BlockSpec((reqs * l_pad, BRANCH_WIDTH), lambda b: (b, C_SQ // BRANCH_WIDTH)),
                  pl.BlockSpec((reqs * l_pad, LANES), lambda b: (b, kcol)),
                  pl.BlockSpec((reqs * l_pad, LANES), lambda b: (b, vcol)),
                  pl.BlockSpec((None, reqs * WINDOW, LANES), lambda b: (layer, b, 0)),
                  pl.BlockSpec((None, reqs * WINDOW, LANES), lambda b: (layer, b, 0))],
        out_specs=pl.BlockSpec((reqs * l_pad, BRANCH_WIDTH), lambda b: (b, 0)),
        out_shape=jax.ShapeDtypeStruct((m, BRANCH_WIDTH), bf16),
        compiler_params=_params(("parallel",)),
        name="swa_sample",
    )(sinks, bias, u, u, u, kbuf, vbuf)


def _mem_kernel(q_ref, k_ref, v_ref, o_ref):
    for h in range(MEM_HEADS):
        hs = slice(h * MEM_HEAD_DIM, (h + 1) * MEM_HEAD_DIM)
        s = _dot_nt(q_ref[:, hs].astype(bf16), k_ref[:, hs].astype(bf16)) * (MEM_HEAD_DIM ** -0.5)
        p = jnp.exp(s - jnp.max(s, axis=-1, keepdims=True))
        o = _dot(p.astype(bf16), v_ref[:, hs].astype(bf16)) / jnp.sum(p, axis=-1, keepdims=True)
        o_ref[:, hs] = o.astype(bf16)


def mem_attention(u, kv, cache, layer, batch, l_pad, tl):
    m = u.shape[0]
    nl = l_pad // tl
    width = MEM_HEADS * MEM_HEAD_DIM
    if cache is None:
        kv_args = (kv, kv)
        kv_specs = [pl.BlockSpec((MEM_TOKENS, width), lambda b, i: (b, 0)),
                    pl.BlockSpec((MEM_TOKENS, width), lambda b, i: (b, 1))]
    else:
        kv_args = cache
        kv_specs = [pl.BlockSpec((MEM_TOKENS, width), lambda b, i: (layer * batch + b, 0))] * 2
    return pl.pallas_call(
        _mem_kernel,
        grid=(batch, nl),
        in_specs=[pl.BlockSpec((tl, width), lambda b, i: (b * nl + i, C_MQ // width))] + kv_specs,
        out_specs=pl.BlockSpec((tl, width), lambda b, i: (b * nl + i, 0)),
        out_shape=jax.ShapeDtypeStruct((m, width), bf16),
        compiler_params=_params(("parallel", "parallel")),
        name="mem_attention",
    )(u, *kv_args)


def _seg64_sum(x, ones_ref):
    hi = x.astype(bf16)
    lo = (x - hi.astype(f32)).astype(bf16)
    cols = []
    for j in range(x.shape[1] // LANES):
        js = slice(j * LANES, (j + 1) * LANES)
        cols.append(_dot(hi[:, js], ones_ref[...]) + _dot(lo[:, js], ones_ref[...]))
    return jnp.concatenate(cols, axis=-1)


def _rwkv_features(xs, prevs, w_refs, ones_ref):
    mur_ref, muk_ref, muv_ref, mulo_ref, w0_ref, a0_ref, wl_ref, kk_ref, ka_ref, rk_ref = w_refs

    def token_shift(x, prev, mu_ref):
        row = lax.broadcasted_iota(jnp.int32, x.shape, 0)
        shifted = jnp.where(row == 0, prev, pltpu.roll(x, 1, 0))
        return x + (shifted - x) * mu_ref[...]

    r, k0, v, lo = (token_shift(x, pv, mu) for x, pv, mu in zip(xs, prevs, (mur_ref, muk_ref, muv_ref, mulo_ref)))
    col = lax.broadcasted_iota(jnp.int32, lo.shape, 1)
    act = jnp.where(col < 64, jnp.tanh(lo), jnp.where(col < 128, lo, _sigmoid(lo)))
    proj = _dot(act.astype(bf16), wl_ref[...])
    log_w = -_softplus(-(w0_ref[...] + proj[:, :BRANCH_WIDTH])) - 0.5
    log_decay = -jnp.exp(log_w)
    a = _sigmoid(a0_ref[...] + proj[:, BRANCH_WIDTH:2 * BRANCH_WIDTH])
    g = proj[:, 2 * BRANCH_WIDTH:]
    kk = k0 * kk_ref[...]
    kk = kk / jnp.maximum(jnp.sqrt(_seg64_sum(kk * kk, ones_ref)), 1e-12)
    k = k0 * (1.0 + (a - 1.0) * ka_ref[...])
    bonus = _seg64_sum(r * k * rk_ref[...], ones_ref) * v
    return r, k, v, kk, a, log_decay, g, bonus


def _rwkv_output(y, bonus, g, lg_ref, lb_ref, ones_ref):
    mu = _seg64_sum(y, ones_ref) * (1.0 / RWKV_HEAD)
    d = y - mu
    var = _seg64_sum(d * d, ones_ref) * (1.0 / RWKV_HEAD)
    yn = d * lax.rsqrt(var + 64e-5) * lg_ref[...] + lb_ref[...]
    return ((yn + bonus) * g).astype(bf16)


def _rwkv_seq_kernel(*refs, tb, steps, reqs):
    x_refs, p_refs, w_refs = refs[0:4], refs[4:8], refs[8:18]
    lg_ref, lb_ref, ones_ref, s0_ref, o_ref, s_ref = refs[18:24]
    w3, kk3, kka3, k3, r3, v3, y3, bonus_s, g_s = refs[24:]
    s_ref[...] = s0_ref[...]
    y3[...] = jnp.zeros_like(y3)

    for q in range(reqs):
        rows = slice(q * tb, (q + 1) * tb)
        xs = [x[rows, :] for x in x_refs]
        prevs = [p[q] for p in p_refs]
        r, k, v, kk, a, log_decay, g, bonus = _rwkv_features(xs, prevs, w_refs, ones_ref)
        decay = jnp.exp(log_decay)
        kka = kk * a
        g_s[rows, :] = g
        bonus_s[rows, :] = bonus
        for h in range(RWKV_HEADS):
            hs = slice(h * RWKV_HEAD, (h + 1) * RWKV_HEAD)
            j = q * RWKV_HEADS + h
            w3[j] = decay[:, hs]
            kk3[j] = kk[:, hs]
            kka3[j] = kka[:, hs]
            k3[j] = k[:, hs]
            r3[j] = r[:, hs]
            v3[j] = v[:, hs]

    eye = (lax.broadcasted_iota(jnp.int32, (RWKV_HEAD, RWKV_HEAD), 0)
           == lax.broadcasted_iota(jnp.int32, (RWKV_HEAD, RWKV_HEAD), 1)).astype(f32)

    def step(t, carry):
        ts = pl.ds(t, 1)
        chains = [(q, h, q * RWKV_HEADS + h) for q in range(reqs) for h in range(RWKV_HEADS)]
        s_olds = [s_ref[q, h] for q, h, _ in chains]
        sas = [jnp.sum(s * kk3[j, ts, :], axis=1, keepdims=True) for s, (_, _, j) in zip(s_olds, chains)]
        v_cols = [jnp.sum(eye * v3[j, ts, :], axis=1, keepdims=True) for _, _, j in chains]
        s_news = [s * w3[j, ts, :] - sa * kka3[j, ts, :] + vc * k3[j, ts, :]
                  for s, sa, vc, (_, _, j) in zip(s_olds, sas, v_cols, chains)]
        for s_new, (q, h, _) in zip(s_news, chains):
            s_ref[q, h] = s_new
        y_cols = [jnp.sum(s * r3[j, ts, :], axis=1, keepdims=True) for s, (_, _, j) in zip(s_news, chains)]
        for y_col, (_, _, j) in zip(y_cols, chains):
            y3[j, ts, :] = jnp.sum(eye * y_col, axis=0, keepdims=True)
        return carry

    lax.fori_loop(0, steps, step, 0)

    for q in range(reqs):
        rows = slice(q * tb, (q + 1) * tb)
        y = jnp.concatenate([y3[q * RWKV_HEADS + h] for h in range(RWKV_HEADS)], axis=-1)
        o_ref[rows, :] = _rwkv_output(y, bonus_s[rows, :], g_s[rows, :], lg_ref, lb_ref, ones_ref)


def _unit_lower_inverse_minus_identity(ns):
    size = ns[0].shape[0]
    t = lax.broadcasted_iota(jnp.int32, (size, size), 0)
    s = lax.broadcasted_iota(jnp.int32, (size, size), 1)
    first = ((t >> 1) == (s >> 1)) & (t > s)
    es = [-jnp.where(first, n, 0.0) for n in ns]
    blk, shift = 4, 2
    while blk <= RWKV_HEAD:
        half = blk // 2
        sel = ((t >> shift) == (s >> shift)) & ((t & (blk - 1)) >= half) & ((s & (blk - 1)) < half)
        cs = [jnp.where(sel, n, 0.0) for n in ns]
        zs = [c + _dot(c.astype(bf16), e.astype(bf16)) for c, e in zip(cs, es)]
        es = [e - z - _dot(e.astype(bf16), z.astype(bf16)) for e, z in zip(es, zs)]
        blk, shift = blk * 2, shift + 1
    return es


def _rwkv_chunk_kernel(*refs, tb):
    x_refs, p_refs, w_refs = refs[0:4], refs[4:8], refs[8:18]
    lg_ref, lb_ref, ones_ref, s0_ref, o_ref, s_ref = refs[18:24]
    carries = refs[24:28]
    sp = refs[28]
    i = pl.program_id(1)
    n_pairs = RWKV_HEADS // 2
    hd = RWKV_HEAD

    @pl.when(i == 0)
    def _():
        zero = jnp.zeros((hd, hd), f32)
        for p in range(n_pairs):
            top = jnp.concatenate([s0_ref[0, 2 * p], zero], axis=1)
            bot = jnp.concatenate([zero, s0_ref[0, 2 * p + 1]], axis=1)
            sp[p] = jnp.concatenate([top, bot], axis=0)
        for carry, p_ref in zip(carries, p_refs):
            carry[...] = p_ref[0]

    xs = [x[...] for x in x_refs]
    prevs = [carry[...] for carry in carries]
    for carry, x in zip(carries, xs):
        carry[...] = x[tb - 1:tb, :]
    r, k, v, kk, a, lw, g, bonus = _rwkv_features(xs, prevs, w_refs, ones_ref)
    beta = kk * a
    row = lax.broadcasted_iota(jnp.int32, lw.shape, 0)
    cum = lw
    d = 1
    while d < tb:
        cum = cum + jnp.where(row >= d, pltpu.roll(cum, d, 0), 0.0)
        d *= 2
    c_last = cum[tb - 1:tb, :]
    a_t = kk * jnp.exp(cum - lw)
    r_t = r * jnp.exp(cum)
    e_neg = jnp.exp(-cum)
    k_t = k * e_neg
    b_t = beta * e_neg
    e_hat = jnp.exp(c_last - cum)
    k_h = k * e_hat
    b_h = beta * e_hat
    gamma = jnp.exp(c_last)

    lane = lax.broadcasted_iota(jnp.int32, (tb, LANES), 1)
    head0 = lane < hd
    split = lambda x: (jnp.where(head0, x, 0.0), jnp.where(head0, 0.0, x))
    tt = lax.broadcasted_iota(jnp.int32, (LANES, LANES), 0)
    ss = lax.broadcasted_iota(jnp.int32, (LANES, LANES), 1)
    strict = tt > ss
    incl = tt >= ss
    same_head = (tt >= hd) == (ss >= hd)
    fold = lambda m: m[:tb, :] + m[tb:, :]

    pairs = range(n_pairs)
    cols = [slice(p * LANES, (p + 1) * LANES) for p in pairs]
    stack = lambda *xs: jnp.concatenate(xs, axis=0)
    scs = [_dot_nt(stack(*split(a_t[:, ps]), *split(r_t[:, ps])).astype(bf16),
                   stack(*split(k_t[:, ps]), *split(b_t[:, ps])).astype(bf16)) for ps in cols]
    es = _unit_lower_inverse_minus_identity([jnp.where(strict, sc[:2 * tb, 2 * tb:], 0.0) for sc in scs])
    eye = jnp.where(tt == ss, 1.0, 0.0)
    t_cats = [fold(eye + e).astype(bf16) for e in es]
    ak_cats = [fold(jnp.where(strict, sc[:2 * tb, :2 * tb], 0.0)).astype(bf16) for sc in scs]
    r_cats = [jnp.concatenate([fold(jnp.where(incl, sc[2 * tb:, :2 * tb], 0.0)),
                               -fold(jnp.where(incl, sc[2 * tb:, 2 * tb:], 0.0))], axis=1).astype(bf16)
              for sc in scs]
    s_olds = [sp[p] for p in pairs]
    grs = [_dot_nt(stack(a_t[:, ps], r_t[:, ps]).astype(bf16), s_old.astype(bf16))
           for ps, s_old in zip(cols, s_olds)]
    v_sts = [stack(*split(v[:, ps])).astype(bf16) for ps in cols]
    u_rhss = [gr[:tb] + _dot(ak, v_st) for gr, ak, v_st in zip(grs, ak_cats, v_sts)]
    us = [_dot(t_cat, stack(*split(u_rhs)).astype(bf16)) for t_cat, u_rhs in zip(t_cats, u_rhss)]
    ys = [gr[tb:] + _dot(r_cat, stack(v_st, stack(*split(u)).astype(bf16)))
          for gr, r_cat, v_st, u in zip(grs, r_cats, v_sts, us)]
    for p, ps, s_old, u in zip(pairs, cols, s_olds, us):
        vu_t = stack(v[:, ps], -u).T.astype(bf16)
        kb = stack(k_h[:, ps], b_h[:, ps]).astype(bf16)
        sp[p] = s_old * gamma[:, ps] + jnp.where(same_head, _dot(vu_t, kb), 0.0)

    o_ref[...] = _rwkv_output(jnp.concatenate(ys, axis=-1), bonus, g, lg_ref, lb_ref, ones_ref)

    @pl.when(i == pl.num_programs(1) - 1)
    def _():
        for p in range(n_pairs):
            full = sp[p]
            s_ref[0, 2 * p] = full[:hd, :hd]
            s_ref[0, 2 * p + 1] = full[hd:, hd:]


def rwkv(u, prev, p, s0, layer, ones2, batch, l_pad, l_real, chunked, reqs=1):
    m = u.shape[0]
    bw = BRANCH_WIDTH
    full = lambda w: pl.BlockSpec((1, w), lambda b, i: (0, 0))
    if chunked:
        tb, nb, nblk = RWKV_HEAD, 1, l_pad // RWKV_HEAD
        assert l_real == l_pad and l_pad % tb == 0
        body = functools.partial(_rwkv_chunk_kernel, tb=tb)
        scratch = [pltpu.VMEM((1, bw), f32), pltpu.VMEM((1, bw), f32), pltpu.VMEM((1, bw), f32),
                   pltpu.VMEM((1, RWKV_LORA), f32), pltpu.VMEM((RWKV_HEADS // 2, LANES, LANES), f32)]
    else:
        tb, nb, nblk = l_pad, reqs, 1
        assert batch % reqs == 0
        body = functools.partial(_rwkv_seq_kernel, tb=tb, steps=l_real, reqs=reqs)
        head_buf = pltpu.VMEM((reqs * RWKV_HEADS, tb, RWKV_HEAD), f32)
        scratch = [head_buf] * 7 + [pltpu.VMEM((reqs * tb, bw), f32), pltpu.VMEM((reqs * tb, bw), f32)]
    rows = nb * tb
    row = lambda b, i: b * nblk + i
    prev_spec = lambda w: pl.BlockSpec((None, nb, 1, w), lambda b, i: (layer, b, 0, 0))
    return pl.pallas_call(
        body,
        grid=(batch // nb, nblk),
        in_specs=[pl.BlockSpec((rows, bw), lambda b, i: (row(b, i), C_R // bw)),
                  pl.BlockSpec((rows, bw), lambda b, i: (row(b, i), C_K // bw)),
                  pl.BlockSpec((rows, bw), lambda b, i: (row(b, i), C_V // bw)),
                  pl.BlockSpec((rows, RWKV_LORA), lambda b, i: (row(b, i), C_LORA // RWKV_LORA)),
                  prev_spec(bw), prev_spec(bw), prev_spec(bw), prev_spec(RWKV_LORA),
                  full(bw), full(bw), full(bw), full(RWKV_LORA),
                  full(bw), full(bw),
                  pl.BlockSpec((RWKV_LORA, 3 * bw), lambda b, i: (0, 0)),
                  full(bw), full(bw), full(bw), full(bw), full(bw),
                  pl.BlockSpec((LANES, LANES), lambda b, i: (0, 0)),
                  pl.BlockSpec((None, nb, RWKV_HEADS, RWKV_HEAD, RWKV_HEAD), lambda b, i: (layer, b, 0, 0, 0))],
        out_specs=[pl.BlockSpec((rows, bw), lambda b, i: (row(b, i), 0)),
                   pl.BlockSpec((nb, RWKV_HEADS, RWKV_HEAD, RWKV_HEAD), lambda b, i: (b, 0, 0, 0))],
        out_shape=[jax.ShapeDtypeStruct((m, bw), bf16),
                   jax.ShapeDtypeStruct((batch, RWKV_HEADS, RWKV_HEAD, RWKV_HEAD), f32)],
        scratch_shapes=scratch,
        compiler_params=_params(("parallel", "arbitrary")),
        name="rwkv7_chunked" if chunked else "rwkv7_seq",
    )(u, u, u, u, *prev, p['mu_r'], p['mu_k'], p['mu_v'], p['mu_lo'], p['w0'], p['a0'], p['w_lora'],
      p['k_k'], p['k_a'], p['r_k'], p['ln_g'], p['ln_b'], ones2, s0)


def _pack_w_in_t(w_in):
    wt = jnp.swapaxes(w_in, 1, 2)
    seg = lambda o, n: wt[:, o:o + n]
    parts = [seg(_O_GV, 1024), seg(_O_GR, 1024), seg(_O_SQ, 1024), seg(_O_MQ, 1024),
             seg(_O_RU, 3072), seg(_O_GPRE, 8192), seg(_O_GQ, 512), seg(_O_GK, 512),
             seg(_O_RU + 3072, RWKV_LORA), seg(_O_SK, 128), seg(_O_SV, 128), seg(_O_GA, GLA_GATE_RANK),
             jnp.zeros((wt.shape[0], N_PACK - C_GA - GLA_GATE_RANK, wt.shape[2]), wt.dtype)]
    return jnp.concatenate(parts, axis=1).astype(bf16)


def _layer_params(l, w):
    row = lambda x: x[l].reshape(1, -1)
    mu = w['rwkv_mu'][l]
    z = lambda r, c: jnp.zeros((r, c), f32)
    w_lora = jnp.concatenate([
        jnp.concatenate([w['rwkv_w2'][l], z(64, 1024), z(64, 1024)], axis=1),
        jnp.concatenate([z(64, 1024), w['rwkv_a2'][l], z(64, 1024)], axis=1),
        jnp.concatenate([z(128, 1024), z(128, 1024), w['rwkv_g2'][l]], axis=1)], axis=0).astype(bf16)
    a_up = jnp.concatenate([w['gla_a_up'][l], z(LANES - GLA_GATE_RANK, GLA_QK)], axis=0).astype(bf16)
    return {
        'a_up': a_up, 'a_b': row(w['gla_a_b']),
        'gla_g': row(w['gla_norm_g']), 'gla_b': row(w['gla_norm_b']),
        'sinks': w['swa_sinks'][l],
        'rwkv': {'mu_r': mu[:1024].reshape(1, -1), 'mu_k': mu[1024:2048].reshape(1, -1),
                 'mu_v': mu[2048:3072].reshape(1, -1), 'mu_lo': mu[3072:].reshape(1, -1),
                 'w0': row(w['rwkv_w0']), 'a0': row(w['rwkv_a0']), 'w_lora': w_lora,
                 'k_k': row(w['rwkv_k_k']), 'k_a': row(w['rwkv_k_a']), 'r_k': row(w['rwkv_r_k']),
                 'ln_g': row(w['rwkv_ln_g']), 'ln_b': row(w['rwkv_ln_b'])},
        'ln1_g': row(w['ln1_g']), 'ln1_b': row(w['ln1_b']),
        'ln2_g': row(w['ln2_g']), 'ln2_b': row(w['ln2_b']),
    }


def _tiles(m):
    if m >= 1024:
        return {'proj': 2048, 'merge': 512, 'ffn': 2048, 'ln': 256}
    return {'proj': m, 'merge': m, 'ffn': m, 'ln': m}


def _trunk_layer(h_f, h_b, l, p, big, ones2, mem_kv, mem_cache, gla_s0, rwkv_s0, rwkv_prev, swa_buf,
                 batch, l_pad, l_real):
    m = h_f.shape[0]
    tl = _tiles(m)
    u = matmul(h_b, big['w_in_t'], l, tl['proj'], 1024, "in_proj", trans_w=True)
    o_a, gla_s = gla(u, p['a_up'], p['a_b'], p['gla_g'], p['gla_b'], gla_s0[0], gla_s0[1], batch, l_pad, l_real,
                     min(GLA_CHUNK, l_pad))
    if swa_buf is None:
        o_b = swa_prompt(u, p['sinks'], batch, l_pad)
    else:
        o_b = swa_sample(u, swa_buf[0], swa_buf[1], l, p['sinks'], batch, l_pad, l_real, 4)
    chunked = l_real == l_pad and l_pad % RWKV_HEAD == 0
    o_c, rwkv_s = rwkv(u, rwkv_prev[0], p['rwkv'], rwkv_s0[0], rwkv_s0[1], ones2, batch, l_pad, l_real,
                       chunked, reqs=1 if chunked else 4)
    o_m = mem_attention(u, mem_kv, mem_cache, l, batch, l_pad, min(512, l_pad))
    merged = gated_merge((o_a, o_b, o_c, o_m), u, big['gate_b'], big['w_branch'], l, tl['merge'], 512)
    x_f, x_b = matmul_residual_ln(merged, big['w_out'], l, h_f, p['ln1_g'], p['ln1_b'], tl['ln'])
    act = ffn_up(x_b, big['w_gu'], l, tl['ffn'], 512)
    y_f, y_b = matmul_residual_ln(act, big['w_down'], l, x_f, p['ln2_g'], p['ln2_b'], tl['ln'])
    return y_f, y_b, u, gla_s, rwkv_s


def _split_ru(x):
    return (x[..., :1024], x[..., 1024:2048], x[..., 2048:3072], x[..., 3072:])


def kernel(x_prompt, x_sample, mem_prompt, cache_swa_k, cache_swa_v, cache_mem_k, cache_mem_v, state_gla, state_rwkv, state_rwkv_shift, w_in, gate_b, gla_a_up, gla_a_b, gla_norm_g, gla_norm_b, swa_sinks, rwkv_mu, rwkv_w0, rwkv_w2, rwkv_a0, rwkv_a2, rwkv_g2, rwkv_k_k, rwkv_k_a, rwkv_r_k, rwkv_ln_g, rwkv_ln_b, w_mem_kv, w_branch, w_out, ln1_g, ln1_b, w_gu, w_down, ln2_g, ln2_b):
    weights = {'gla_a_up': gla_a_up, 'gla_a_b': gla_a_b,
               'gla_norm_g': gla_norm_g, 'gla_norm_b': gla_norm_b, 'swa_sinks': swa_sinks,
               'rwkv_mu': rwkv_mu, 'rwkv_w0': rwkv_w0, 'rwkv_w2': rwkv_w2, 'rwkv_a0': rwkv_a0,
               'rwkv_a2': rwkv_a2, 'rwkv_g2': rwkv_g2, 'rwkv_k_k': rwkv_k_k, 'rwkv_k_a': rwkv_k_a,
               'rwkv_r_k': rwkv_r_k, 'rwkv_ln_g': rwkv_ln_g, 'rwkv_ln_b': rwkv_ln_b,
               'ln1_g': ln1_g, 'ln1_b': ln1_b, 'ln2_g': ln2_g, 'ln2_b': ln2_b}
    big = {'w_in_t': _pack_w_in_t(w_in), 'gate_b': gate_b, 'w_branch': w_branch, 'w_gu': w_gu,
           'w_out': w_out.astype(bf16), 'w_down': w_down.astype(bf16)}
    w_mem_kv_b = w_mem_kv.astype(bf16)
    bp, lp, _ = x_prompt.shape
    bs, ls, _ = x_sample.shape
    ls_pad = -(-ls // SUBLANES) * SUBLANES
    mp, ms = bp * lp, bs * ls_pad

    hp_f = x_prompt.reshape(mp, D_MODEL)
    hs_f = jnp.pad(x_sample, ((0, 0), (0, ls_pad - ls), (0, 0))).reshape(ms, D_MODEL)
    hp_b, hs_b = hp_f.astype(bf16), hs_f.astype(bf16)
    mem_b = mem_prompt.reshape(bp * MEM_TOKENS, D_MODEL).astype(bf16)
    half = LANES // 2
    blk = jnp.ones((half, half), f32)
    zero = jnp.zeros((half, half), f32)
    ones2 = jnp.concatenate([jnp.concatenate([blk, zero], 1), jnp.concatenate([zero, blk], 1)], 0).astype(bf16)

    gla0_p = jnp.zeros((1, bp, GLA_HEADS, GLA_DK, GLA_DV), f32)
    rwkv0_p = jnp.zeros((1, bp, RWKV_HEADS, RWKV_HEAD, RWKV_HEAD), f32)
    prev0_p = _split_ru(jnp.zeros((1, bp, 1, RWKV_COLS), f32))
    prev_s = _split_ru(state_rwkv_shift)
    kvw = SWA_KV_HEADS * SWA_HEAD_DIM
    kbuf = cache_swa_k.reshape(DEPTH, bs * WINDOW, kvw)
    vbuf = cache_swa_v.reshape(DEPTH, bs * WINDOW, kvw)
    mem_k2d = cache_mem_k.reshape(DEPTH * bs * MEM_TOKENS, MEM_HEADS * MEM_HEAD_DIM)
    mem_v2d = cache_mem_v.reshape(DEPTH * bs * MEM_TOKENS, MEM_HEADS * MEM_HEAD_DIM)

    outs = {k: [] for k in ('p_swk', 'p_swv', 'p_mk', 'p_mv', 'p_gla', 'p_rw', 'p_rs',
                            's_swk', 's_swv', 's_gla', 's_rw', 's_rs')}
    for l in range(DEPTH):
        p = _layer_params(l, weights)
        kv = matmul(mem_b, w_mem_kv_b, l, bp * MEM_TOKENS, 512, "mem_kv")
        hp_f, hp_b, u, gs, rs = _trunk_layer(hp_f, hp_b, l, p, big, ones2, kv, None, (gla0_p, 0), (rwkv0_p, 0),
                                             (prev0_p, 0), None, bp, lp, lp)
        u3 = u.reshape(bp, lp, N_PACK)
        outs['p_swk'].append(u3[:, lp - WINDOW:, C_SK:C_SK + kvw].reshape(bp, WINDOW, SWA_KV_HEADS, SWA_HEAD_DIM))
        outs['p_swv'].append(u3[:, lp - WINDOW:, C_SV:C_SV + kvw].reshape(bp, WINDOW, SWA_KV_HEADS, SWA_HEAD_DIM))
        outs['p_mk'].append(kv[:, :1024].reshape(bp, MEM_TOKENS, MEM_HEADS, MEM_HEAD_DIM))
        outs['p_mv'].append(kv[:, 1024:].reshape(bp, MEM_TOKENS, MEM_HEADS, MEM_HEAD_DIM))
        outs['p_gla'].append(gs)
        outs['p_rw'].append(rs)
        outs['p_rs'].append(jnp.concatenate([u3[:, lp - 1:, C_R:C_R + 3072],
                                             u3[:, lp - 1:, C_LORA:C_LORA + RWKV_LORA]], axis=-1))
        hs_f, hs_b, u, gs, rs = _trunk_layer(hs_f, hs_b, l, p, big, ones2, None, (mem_k2d, mem_v2d),
                                             (state_gla, l), (state_rwkv, l), (prev_s, l), (kbuf, vbuf),
                                             bs, ls_pad, ls)
        u3 = u.reshape(bs, ls_pad, N_PACK)
        k_new = u3[:, :ls, C_SK:C_SK + kvw].reshape(bs, ls, SWA_KV_HEADS, SWA_HEAD_DIM)
        v_new = u3[:, :ls, C_SV:C_SV + kvw].reshape(bs, ls, SWA_KV_HEADS, SWA_HEAD_DIM)
        outs['s_swk'].append(jnp.concatenate([cache_swa_k[l][:, ls:], k_new], axis=1))
        outs['s_swv'].append(jnp.concatenate([cache_swa_v[l][:, ls:], v_new], axis=1))
        outs['s_gla'].append(gs)
        outs['s_rw'].append(rs)
        outs['s_rs'].append(jnp.concatenate([u3[:, ls - 1:ls, C_R:C_R + 3072],
                                             u3[:, ls - 1:ls, C_LORA:C_LORA + RWKV_LORA]], axis=-1))

    st = {k: jnp.stack(v) for k, v in outs.items()}
    y_prompt = hp_f.reshape(bp, lp, D_MODEL)
    y_sample = hs_f.reshape(bs, ls_pad, D_MODEL)[:, :ls]
    return (y_prompt, y_sample,
            st['p_swk'], st['p_swv'], st['p_mk'], st['p_mv'], st['p_gla'], st['p_rw'], st['p_rs'],
            st['s_swk'], st['s_swv'], st['s_gla'], st['s_rw'], st['s_rs'])
```

```python
import functools

import jax
import jax.numpy as jnp
from jax import lax
from jax.experimental import pallas as pl
from jax.experimental.pallas import tpu as pltpu

f32 = jnp.float32
bf16 = jnp.bfloat16

D_MODEL = 2048
DEPTH = 4
BRANCH_WIDTH = 1024
N_BRANCH = 4
GLA_HEADS = 4
GLA_QK = 512
GLA_V = 1024
GLA_DK = 128
GLA_DV = 256
GLA_GATE_RANK = 16
GLA_TAU = 16.0
GLA_CHUNK = 64
SWA_HEAD_DIM = 64
SWA_Q_HEADS = 16
SWA_KV_HEADS = 2
SWA_GROUP = SWA_Q_HEADS // SWA_KV_HEADS
WINDOW = 128
RWKV_HEAD = 64
RWKV_HEADS = 16
RWKV_LORA = 256
RWKV_COLS = 3 * BRANCH_WIDTH + RWKV_LORA
MEM_TOKENS = 256
MEM_HEADS = 4
MEM_HEAD_DIM = 256
D_FF = 5632
DEEPNORM_ALPHA = (2 * DEPTH) ** 0.25
NEG = -1e30

_O_GQ, _O_GK, _O_GV, _O_GR, _O_GA = 0, 512, 1024, 2048, 3072
_O_SQ, _O_SK, _O_SV = 3088, 4112, 4240
_O_RU = 4368
_O_MQ = 7696
_O_GPRE = 8720
_N_IN = 16912

C_GV, C_GR, C_SQ, C_MQ, C_R, C_K, C_V = 0, 1024, 2048, 3072, 4096, 5120, 6144
C_GPRE = 7168
C_GQ, C_GK = 15360, 15872
C_LORA = 16384
C_SK, C_SV, C_GA = 16640, 16768, 16896
N_PACK = 17408

LANES = 128
SUBLANES = 8
VMEM_BYTES_V7X = 64 * 1024 * 1024
VMEM_LIMIT = VMEM_BYTES_V7X - 8 * 1024 * 1024


def _params(sem):
    return pltpu.CompilerParams(dimension_semantics=sem, vmem_limit_bytes=VMEM_LIMIT)


def _softplus(z):
    return jnp.maximum(z, 0.0) + jnp.log(1.0 + jnp.exp(-jnp.abs(z)))


def _sigmoid(z):
    return 0.5 * jnp.tanh(0.5 * z) + 0.5


def _dot(a, b):
    return jnp.dot(a, b, preferred_element_type=f32)


def _dot_nt(a, b):
    return lax.dot_general(a, b, (((1,), (1,)), ((), ())), preferred_element_type=f32)


def _mm_kernel(x_ref, w_ref, o_ref, *, trans_w):
    mm = _dot_nt if trans_w else _dot
    o_ref[...] = mm(x_ref[...], w_ref[...].astype(bf16))


def matmul(x, w, layer, tm, tn, name, trans_w=False):
    m, k = x.shape
    n = w.shape[1] if trans_w else w.shape[2]
    w_spec = (pl.BlockSpec((None, tn, k), lambda i, j: (layer, j, 0)) if trans_w
              else pl.BlockSpec((None, k, tn), lambda i, j: (layer, 0, j)))
    return pl.pallas_call(
        functools.partial(_mm_kernel, trans_w=trans_w),
        grid=(m // tm, n // tn),
        in_specs=[pl.BlockSpec((tm, k), lambda i, j: (i, 0)), w_spec],
        out_specs=pl.BlockSpec((tm, tn), lambda i, j: (i, j)),
        out_shape=jax.ShapeDtypeStruct((m, n), f32),
        compiler_params=_params(("parallel", "parallel")),
        name=name,
    )(x, w)


def _mm_ln_kernel(x_ref, w_ref, res_ref, g_ref, b_ref, of_ref, ob_ref):
    z = DEEPNORM_ALPHA * res_ref[...] + _dot(x_ref[...], w_ref[...])
    mu = jnp.mean(z, axis=-1, keepdims=True)
    d = z - mu
    var = jnp.mean(d * d, axis=-1, keepdims=True)
    y = d * lax.rsqrt(var + 1e-5) * g_ref[...] + b_ref[...]
    of_ref[...] = y
    ob_ref[...] = y.astype(bf16)


def matmul_residual_ln(x, w, layer, res, g, b, tm):
    m, k = x.shape
    n = w.shape[2]
    return pl.pallas_call(
        _mm_ln_kernel,
        grid=(m // tm,),
        in_specs=[pl.BlockSpec((tm, k), lambda i: (i, 0)),
                  pl.BlockSpec((None, k, n), lambda i: (layer, 0, 0), pipeline_mode=pl.Buffered(1)),
                  pl.BlockSpec((tm, n), lambda i: (i, 0)),
                  pl.BlockSpec((1, n), lambda i: (0, 0)),
                  pl.BlockSpec((1, n), lambda i: (0, 0))],
        out_specs=[pl.BlockSpec((tm, n), lambda i: (i, 0)),
                   pl.BlockSpec((tm, n), lambda i: (i, 0))],
        out_shape=[jax.ShapeDtypeStruct((m, n), f32), jax.ShapeDtypeStruct((m, n), bf16)],
        compiler_params=_params(("parallel",)),
        name="proj_ln",
    )(x, w, res, g, b)


def _merge_kernel(a_ref, b_ref, c_ref, m_ref, g0_ref, g1_ref, g2_ref, g3_ref, gb_ref, w_ref, o_ref, w_bf):
    @pl.when(pl.program_id(1) == 0)
    def _():
        w_bf[...] = w_ref[...].astype(bf16)

    acc = None
    for n, (br, gp) in enumerate(((a_ref, g0_ref), (b_ref, g1_ref), (c_ref, g2_ref), (m_ref, g3_ref))):
        y = _dot(br[...], w_bf[n])
        gate = _sigmoid(gp[...] + gb_ref[n:n + 1, :])
        acc = gate * y if acc is None else acc + gate * y
    o_ref[...] = acc.astype(bf16)


def gated_merge(branches, u, gate_b, w_branch, layer, tm, tn):
    m = u.shape[0]
    gp0 = C_GPRE // tn
    per = D_MODEL // tn
    br_spec = pl.BlockSpec((tm, BRANCH_WIDTH), lambda j, i: (i, 0))
    gp_specs = [pl.BlockSpec((tm, tn), functools.partial(lambda j, i, n: (i, gp0 + n * per + j), n=n))
                for n in range(N_BRANCH)]
    return pl.pallas_call(
        _merge_kernel,
        grid=(D_MODEL // tn, m // tm),
        in_specs=[br_spec] * 4 + gp_specs + [
            pl.BlockSpec((None, N_BRANCH, tn), lambda j, i: (layer, 0, j)),
            pl.BlockSpec((None, N_BRANCH, BRANCH_WIDTH, tn), lambda j, i: (layer, 0, 0, j))],
        out_specs=pl.BlockSpec((tm, tn), lambda j, i: (i, j)),
        out_shape=jax.ShapeDtypeStruct((m, D_MODEL), bf16),
        scratch_shapes=[pltpu.VMEM((N_BRANCH, BRANCH_WIDTH, tn), bf16)],
        compiler_params=_params(("parallel", "arbitrary")),
        name="gated_merge",
    )(*branches, u, u, u, u, gate_b, w_branch)


def _ffn_up_kernel(x_ref, wg_ref, wu_ref, o_ref, wg_bf, wu_bf):
    @pl.when(pl.program_id(1) == 0)
    def _():
        wg_bf[...] = wg_ref[...].astype(bf16)
        wu_bf[...] = wu_ref[...].astype(bf16)

    x = x_ref[...]
    g = _dot(x, wg_bf[...])
    up = _dot(x, wu_bf[...])
    o_ref[...] = (g * _sigmoid(g) * up).astype(bf16)


def ffn_up(x, w_gu, layer, tm, tn):
    m, k = x.shape
    nj = D_FF // tn
    return pl.pallas_call(
        _ffn_up_kernel,
        grid=(nj, m // tm),
        in_specs=[pl.BlockSpec((tm, k), lambda j, i: (i, 0)),
                  pl.BlockSpec((None, k, tn), lambda j, i: (layer, 0, j)),
                  pl.BlockSpec((None, k, tn), lambda j, i: (layer, 0, nj + j))],
        out_specs=pl.BlockSpec((tm, tn), lambda j, i: (i, j)),
        out_shape=jax.ShapeDtypeStruct((m, D_FF), bf16),
        scratch_shapes=[pltpu.VMEM((k, tn), bf16), pltpu.VMEM((k, tn), bf16)],
        compiler_params=_params(("parallel", "arbitrary")),
        name="ffn_up",
    )(x, w_gu, w_gu)


def _gla_kernel(q_ref, k_ref, v_ref, r_ref, a_ref, aup_ref, ab_ref, ng_ref, nb_ref, s0_ref,
                o_ref, s_ref, *, chunk, sub, l_real, l_pad):
    @pl.when(pl.program_id(1) == 0)
    def _():
        s_ref[...] = s0_ref[...]

    for ci in range(sub):
        rows = slice(ci * chunk, (ci + 1) * chunk)
        refs = [x.at[rows, :] for x in (q_ref, k_ref, v_ref, r_ref, a_ref, o_ref)]
        _gla_chunk(*refs[:5], aup_ref, ab_ref, ng_ref, nb_ref, refs[5], s_ref,
                   (pl.program_id(1) * sub + ci) * chunk, chunk, l_real, l_pad)


def _gla_chunk(q_ref, k_ref, v_ref, r_ref, a_ref, aup_ref, ab_ref, ng_ref, nb_ref, o_ref, s_ref,
               t0, chunk, l_real, l_pad):
    la = _dot(a_ref[...].astype(bf16), aup_ref[...]) + ab_ref[...]
    la = (jnp.minimum(la, 0.0) - jnp.log(1.0 + jnp.exp(-jnp.abs(la)))) * (1.0 / GLA_TAU)
    row = lax.broadcasted_iota(jnp.int32, (chunk, GLA_QK), 0)
    kin = k_ref[...]
    if l_real < l_pad:
        real = (t0 + row) < l_real
        la = jnp.where(real, la, 0.0)
        kin = jnp.where(real, kin, 0.0)
    b = la
    d = 1
    while d < chunk:
        b = b + jnp.where(row >= d, pltpu.roll(b, d, 0), 0.0)
        d *= 2
    b_last = b[chunk - 1:chunk, :]
    q_dec = q_ref[...] * (GLA_DK ** -0.5) * jnp.exp(b)
    k_inv = kin * jnp.exp(-b)
    k_dec = kin * jnp.exp(b_last - b)
    tt = lax.broadcasted_iota(jnp.int32, (chunk, chunk), 0)
    ss = lax.broadcasted_iota(jnp.int32, (chunk, chunk), 1)
    causal = tt >= ss
    heads = range(GLA_HEADS)
    kss = [slice(h * GLA_DK, (h + 1) * GLA_DK) for h in heads]
    vss = [slice(h * GLA_DV, (h + 1) * GLA_DV) for h in heads]
    qhs = [q_dec[:, ks].astype(bf16) for ks in kss]
    atts = [jnp.where(causal, _dot_nt(qh, k_inv[:, ks].astype(bf16)), 0.0).astype(bf16) for qh, ks in zip(qhs, kss)]
    vhs = [v_ref[:, vs].astype(bf16) for vs in vss]
    s_olds = [s_ref[0, h] for h in heads]
    os_ = [_dot(att, vh) + _dot(qh, s_old.astype(bf16)) for att, vh, qh, s_old in zip(atts, vhs, qhs, s_olds)]
    decay_cols = [jnp.exp(jnp.sum(la[:, ks].T, axis=1, keepdims=True)) for ks in kss]
    for h, ks, vh, s_old, decay_col in zip(heads, kss, vhs, s_olds, decay_cols):
        s_ref[0, h] = decay_col * s_old + _dot(k_dec[:, ks].T.astype(bf16), vh)
    for vs, o in zip(vss, os_):
        mu = jnp.mean(o, axis=-1, keepdims=True)
        dd = o - mu
        var = jnp.mean(dd * dd, axis=-1, keepdims=True)
        nrm = dd * lax.rsqrt(var + 1e-5) * ng_ref[:, vs] + nb_ref[:, vs]
        gr = r_ref[:, vs]
        o_ref[:, vs] = (nrm * (gr * _sigmoid(gr))).astype(bf16)


def gla(u, a_up, a_b, norm_g, norm_b, s0, layer, batch, l_pad, l_real, chunk):
    m = u.shape[0]
    sub = 2 if l_pad % (2 * chunk) == 0 else 1
    rows = sub * chunk
    nc = l_pad // rows
    row = lambda b, c: b * nc + c
    return pl.pallas_call(
        functools.partial(_gla_kernel, chunk=chunk, sub=sub, l_real=l_real, l_pad=l_pad),
        grid=(batch, nc),
        in_specs=[pl.BlockSpec((rows, GLA_QK), lambda b, c: (row(b, c), C_GQ // GLA_QK)),
                  pl.BlockSpec((rows, GLA_QK), lambda b, c: (row(b, c), C_GK // GLA_QK)),
                  pl.BlockSpec((rows, GLA_V), lambda b, c: (row(b, c), C_GV // GLA_V)),
                  pl.BlockSpec((rows, GLA_V), lambda b, c: (row(b, c), C_GR // GLA_V)),
                  pl.BlockSpec((rows, LANES), lambda b, c: (row(b, c), C_GA // LANES)),
                  pl.BlockSpec((LANES, GLA_QK), lambda b, c: (0, 0)),
                  pl.BlockSpec((1, GLA_QK), lambda b, c: (0, 0)),
                  pl.BlockSpec((1, GLA_V), lambda b, c: (0, 0)),
                  pl.BlockSpec((1, GLA_V), lambda b, c: (0, 0)),
                  pl.BlockSpec((None, 1, GLA_HEADS, GLA_DK, GLA_DV), lambda b, c: (layer, b, 0, 0, 0))],
        out_specs=[pl.BlockSpec((rows, GLA_V), lambda b, c: (row(b, c), 0)),
                   pl.BlockSpec((1, GLA_HEADS, GLA_DK, GLA_DV), lambda b, c: (b, 0, 0, 0))],
        out_shape=[jax.ShapeDtypeStruct((m, GLA_V), bf16),
                   jax.ShapeDtypeStruct((batch, GLA_HEADS, GLA_DK, GLA_DV), f32)],
        compiler_params=_params(("parallel", "arbitrary")),
        name="gla",
    )(u, u, u, u, u, a_up, a_b, norm_g, norm_b, s0)


def _alibi_slopes():
    return 2.0 ** (-8.0 * jnp.arange(1, SWA_Q_HEADS + 1, dtype=f32) / SWA_Q_HEADS)


def _swa_bias(t_pos, key_pos, key_ok):
    dist = (t_pos[:, None] - key_pos[None, :]).astype(f32)
    valid = (dist >= 0) & (dist <= WINDOW) & key_ok[None, :]
    slopes = _alibi_slopes().reshape(SWA_KV_HEADS, SWA_GROUP, 1, 1)
    bias = jnp.where(valid[None, None], -slopes * dist[None, None], NEG)
    return bias.reshape(SWA_KV_HEADS, SWA_GROUP * t_pos.shape[0], key_pos.shape[0])


def _swa_groups(q_all, keys, vals, biases, sink_ref, rows):
    kvs = range(SWA_KV_HEADS)
    d = SWA_HEAD_DIM
    pairs = SWA_GROUP // 2
    heads = [range(kv * SWA_GROUP, (kv + 1) * SWA_GROUP) for kv in kvs]
    low = lax.broadcasted_iota(jnp.int32, (rows, LANES), 1) < d

    def both_halves(x, kv):
        swapped = pltpu.roll(x, d, 1)
        first = lax.broadcasted_iota(jnp.int32, x.shape, 1) < d
        return jnp.where(first, x, swapped) if kv == 0 else jnp.where(first, swapped, x)

    def stacked_queries(kv):
        parts = []
        for p in range(kv * pairs, (kv + 1) * pairs):
            q_pair = q_all[:, p * LANES:(p + 1) * LANES]
            parts += [jnp.where(low, q_pair, 0.0), jnp.where(low, 0.0, q_pair)]
        return jnp.concatenate(parts, axis=0).astype(bf16)

    qs = [stacked_queries(kv) for kv in kvs]
    k2s = [jnp.concatenate([both_halves(kb, kv) for kb in keys], axis=0).astype(bf16) for kv in kvs]
    v2s = [jnp.concatenate([both_halves(vb, kv) for vb in vals], axis=0).astype(bf16) for kv in kvs]
    ss = [_dot_nt(q, k2) * (d ** -0.5) + bias for q, k2, bias in zip(qs, k2s, biases)]
    sinks = [jnp.concatenate([jnp.full((rows, 1), sink_ref[h], f32) for h in hh], axis=0) for hh in heads]
    ms = [jnp.maximum(jnp.max(s, axis=-1, keepdims=True), sink) for s, sink in zip(ss, sinks)]
    ps = [jnp.exp(s - m) for s, m in zip(ss, ms)]
    dens = [jnp.sum(p, axis=-1, keepdims=True) + jnp.exp(sink - m) for p, sink, m in zip(ps, sinks, ms)]
    os_ = [_dot(p.astype(bf16), v2) / den for p, v2, den in zip(ps, v2s, dens)]
    return [jnp.where(low, o[(2 * p) * rows:(2 * p + 1) * rows, :], o[(2 * p + 1) * rows:(2 * p + 2) * rows, :])
            for o in os_ for p in range(pairs)]


def _swa_prompt_kernel(sink_ref, bias_ref, q_ref, kc_ref, kp_ref, vc_ref, vp_ref, o_ref):
    keys = (kp_ref[...], kc_ref[...])
    vals = (vp_ref[...], vc_ref[...])
    biases = [bias_ref[0, kv] for kv in range(SWA_KV_HEADS)]
    outs = _swa_groups(q_ref[...], keys, vals, biases, sink_ref, WINDOW)
    o_ref[...] = jnp.concatenate(outs, axis=-1).astype(bf16)


def swa_prompt(u, sinks, batch, seq):
    m = u.shape[0]
    nb = seq // WINDOW
    cur = lambda b, i: b * nb + i
    prev = lambda b, i: b * nb + jnp.maximum(i - 1, 0)
    kcol, vcol = C_SK // LANES, C_SV // LANES
    t_pos = jnp.arange(WINDOW)
    key_pos = jnp.arange(2 * WINDOW) - WINDOW
    bias = jnp.stack([_swa_bias(t_pos, key_pos, key_pos >= 0), _swa_bias(t_pos, key_pos, key_pos >= -WINDOW)])
    return pl.pallas_call(
        _swa_prompt_kernel,
        grid=(batch, nb),
        in_specs=[pl.BlockSpec(memory_space=pltpu.SMEM),
                  pl.BlockSpec((1,) + bias.shape[1:], lambda b, i: (jnp.minimum(i, 1), 0, 0, 0)),
                  pl.BlockSpec((WINDOW, BRANCH_WIDTH), lambda b, i: (cur(b, i), C_SQ // BRANCH_WIDTH)),
                  pl.BlockSpec((WINDOW, LANES), lambda b, i: (cur(b, i), kcol)),
                  pl.BlockSpec((WINDOW, LANES), lambda b, i: (prev(b, i), kcol)),
                  pl.BlockSpec((WINDOW, LANES), lambda b, i: (cur(b, i), vcol)),
                  pl.BlockSpec((WINDOW, LANES), lambda b, i: (prev(b, i), vcol))],
        out_specs=pl.BlockSpec((WINDOW, BRANCH_WIDTH), lambda b, i: (cur(b, i), 0)),
        out_shape=jax.ShapeDtypeStruct((m, BRANCH_WIDTH), bf16),
        compiler_params=_params(("parallel", "parallel")),
        name="swa_prompt",
    )(sinks, bias, u, u, u, u, u)


def _swa_sample_kernel(sink_ref, bias_ref, q_ref, kn_ref, vn_ref, kb_ref, vb_ref, o_ref, *, l_pad, reqs):
    for r in range(reqs):
        rs = slice(r * l_pad, (r + 1) * l_pad)
        bs = slice(r * WINDOW, (r + 1) * WINDOW)
        keys = (kb_ref[bs, :], kn_ref[rs, :])
        vals = (vb_ref[bs, :], vn_ref[rs, :])
        biases = [bias_ref[kv] for kv in range(SWA_KV_HEADS)]
        outs = _swa_groups(q_ref[rs, :], keys, vals, biases, sink_ref, l_pad)
        o_ref[rs, :] = jnp.concatenate(outs, axis=-1).astype(bf16)


def swa_sample(u, kbuf, vbuf, layer, sinks, batch, l_pad, l_real, reqs):
    m = u.shape[0]
    kcol, vcol = C_SK // LANES, C_SV // LANES
    key_pos = jnp.concatenate([jnp.arange(WINDOW) - WINDOW, jnp.arange(l_pad)])
    key_ok = jnp.concatenate([jnp.ones((WINDOW,), bool), jnp.arange(l_pad) < l_real])
    bias = _swa_bias(jnp.arange(l_pad), key_pos, key_ok)
    return pl.pallas_call(
        functools.partial(_swa_sample_kernel, l_pad=l_pad, reqs=reqs),
        grid=(batch // reqs,),
        in_specs=[pl.BlockSpec(memory_space=pltpu.SMEM),
                  pl.BlockSpec(bias.shape, lambda b: (0, 0, 0)),
                  pl.BlockSpec((reqs * l_pad, BRANCH_WIDTH), lambda b: (b, C_SQ // BRANCH_WIDTH)),
                  pl.BlockSpec((reqs * l_pad, LANES), lambda b: (b, kcol)),
                  pl.BlockSpec((reqs * l_pad, LANES), lambda b: (b, vcol)),
                  pl.BlockSpec((None, reqs * WINDOW, LANES), lambda b: (layer, b, 0)),
                  pl.BlockSpec((None, reqs * WINDOW, LANES), lambda b: (layer, b, 0))],
        out_specs=pl.BlockSpec((reqs * l_pad, BRANCH_WIDTH), lambda b: (b, 0)),
        out_shape=jax.ShapeDtypeStruct((m, BRANCH_WIDTH), bf16),
        compiler_params=_params(("parallel",)),
        name="swa_sample",
    )(sinks, bias, u, u, u, kbuf, vbuf)


def _mem_kernel(q_ref, k_ref, v_ref, o_ref):
    for h in range(MEM_HEADS):
        hs = slice(h * MEM_HEAD_DIM, (h + 1) * MEM_HEAD_DIM)
        s = _dot_nt(q_ref[:, hs].astype(bf16), k_ref[:, hs].astype(bf16)) * (MEM_HEAD_DIM ** -0.5)
        p = jnp.exp(s - jnp.max(s, axis=-1, keepdims=True))
        o = _dot(p.astype(bf16), v_ref[:, hs].astype(bf16)) / jnp.sum(p, axis=-1, keepdims=True)
        o_ref[:, hs] = o.astype(bf16)


def mem_attention(u, kv, cache, layer, batch, l_pad, tl):
    m = u.shape[0]
    nl = l_pad // tl
    width = MEM_HEADS * MEM_HEAD_DIM
    if cache is None:
        kv_args = (kv, kv)
        kv_specs = [pl.BlockSpec((MEM_TOKENS, width), lambda b, i: (b, 0)),
                    pl.BlockSpec((MEM_TOKENS, width), lambda b, i: (b, 1))]
    else:
        kv_args = cache
        kv_specs = [pl.BlockSpec((MEM_TOKENS, width), lambda b, i: (layer * batch + b, 0))] * 2
    return pl.pallas_call(
        _mem_kernel,
        grid=(batch, nl),
        in_specs=[pl.BlockSpec((tl, width), lambda b, i: (b * nl + i, C_MQ // width))] + kv_specs,
        out_specs=pl.BlockSpec((tl, width), lambda b, i: (b * nl + i, 0)),
        out_shape=jax.ShapeDtypeStruct((m, width), bf16),
        compiler_params=_params(("parallel", "parallel")),
        name="mem_attention",
    )(u, *kv_args)


def _seg64_sum(x, ones_ref):
    hi = x.astype(bf16)
    lo = (x - hi.astype(f32)).astype(bf16)
    cols = []
    for j in range(x.shape[1] // LANES):
        js = slice(j * LANES, (j + 1) * LANES)
        cols.append(_dot(hi[:, js], ones_ref[...]) + _dot(lo[:, js], ones_ref[...]))
    return jnp.concatenate(cols, axis=-1)


def _rwkv_features(xs, prevs, w_refs, ones_ref):
    mur_ref, muk_ref, muv_ref, mulo_ref, w0_ref, a0_ref, wl_ref, kk_ref, ka_ref, rk_ref = w_refs

    def token_shift(x, prev, mu_ref):
        row = lax.broadcasted_iota(jnp.int32, x.shape, 0)
        shifted = jnp.where(row == 0, prev, pltpu.roll(x, 1, 0))
        return x + (shifted - x) * mu_ref[...]

    r, k0, v, lo = (token_shift(x, pv, mu) for x, pv, mu in zip(xs, prevs, (mur_ref, muk_ref, muv_ref, mulo_ref)))
    col = lax.broadcasted_iota(jnp.int32, lo.shape, 1)
    act = jnp.where(col < 64, jnp.tanh(lo), jnp.where(col < 128, lo, _sigmoid(lo)))
    proj = _dot(act.astype(bf16), wl_ref[...])
    log_w = -_softplus(-(w0_ref[...] + proj[:, :BRANCH_WIDTH])) - 0.5
    log_decay = -jnp.exp(log_w)
    a = _sigmoid(a0_ref[...] + proj[:, BRANCH_WIDTH:2 * BRANCH_WIDTH])
    g = proj[:, 2 * BRANCH_WIDTH:]
    kk = k0 * kk_ref[...]
    k = k0 * (1.0 + (a - 1.0) * ka_ref[...])
    tb = kk.shape[0]
    head_sums = _seg64_sum(jnp.concatenate([kk * kk, r * k * rk_ref[...]], axis=0), ones_ref)
    kk = kk / jnp.maximum(jnp.sqrt(head_sums[:tb]), 1e-12)
    bonus = head_sums[tb:] * v
    return r, k, v, kk, a, log_decay, g, bonus


def _rwkv_output(y, bonus, g, lg_ref, lb_ref, ones_ref):
    mu = _seg64_sum(y, ones_ref) * (1.0 / RWKV_HEAD)
    d = y - mu
    var = _seg64_sum(d * d, ones_ref) * (1.0 / RWKV_HEAD)
    yn = d * lax.rsqrt(var + 64e-5) * lg_ref[...] + lb_ref[...]
    return ((yn + bonus) * g).astype(bf16)


def _rwkv_seq_kernel(*refs, tb, steps, reqs):
    x_refs, p_refs, w_refs = refs[0:4], refs[4:8], refs[8:18]
    lg_ref, lb_ref, ones_ref, s0_ref, o_ref, s_ref = refs[18:24]
    w3, kk3, kka3, k3, r3, v3, y3, bonus_s, g_s = refs[24:]
    s_ref[...] = s0_ref[...]
    y3[...] = jnp.zeros_like(y3)

    for q in range(reqs):
        rows = slice(q * tb, (q + 1) * tb)
        xs = [x[rows, :] for x in x_refs]
        prevs = [p[q] for p in p_refs]
        r, k, v, kk, a, log_decay, g, bonus = _rwkv_features(xs, prevs, w_refs, ones_ref)
        decay = jnp.exp(log_decay)
        kka = kk * a
        g_s[rows, :] = g
        bonus_s[rows, :] = bonus
        for h in range(RWKV_HEADS):
            hs = slice(h * RWKV_HEAD, (h + 1) * RWKV_HEAD)
            j = q * RWKV_HEADS + h
            w3[j] = decay[:, hs]
            kk3[j] = kk[:, hs]
            kka3[j] = kka[:, hs]
            k3[j] = k[:, hs]
            r3[j] = r[:, hs]
            v3[j] = v[:, hs]

    eye = (lax.broadcasted_iota(jnp.int32, (RWKV_HEAD, RWKV_HEAD), 0)
           == lax.broadcasted_iota(jnp.int32, (RWKV_HEAD, RWKV_HEAD), 1)).astype(f32)

    def step(t, carry):
        ts = pl.ds(t, 1)
        chains = [(q, h, q * RWKV_HEADS + h) for q in range(reqs) for h in range(RWKV_HEADS)]
        s_olds = [s_ref[q, h] for q, h, _ in chains]
        sas = [jnp.sum(s * kk3[j, ts, :], axis=1, keepdims=True) for s, (_, _, j) in zip(s_olds, chains)]
        v_cols = [jnp.sum(eye * v3[j, ts, :], axis=1, keepdims=True) for _, _, j in chains]
        s_news = [s * w3[j, ts, :] - sa * kka3[j, ts, :] + vc * k3[j, ts, :]
                  for s, sa, vc, (_, _, j) in zip(s_olds, sas, v_cols, chains)]
        for s_new, (q, h, _) in zip(s_news, chains):
            s_ref[q, h] = s_new
        y_cols = [jnp.sum(s * r3[j, ts, :], axis=1, keepdims=True) for s, (_, _, j) in zip(s_news, chains)]
        for y_col, (_, _, j) in zip(y_cols, chains):
            y3[j, ts, :] = jnp.sum(eye * y_col, axis=0, keepdims=True)
        return carry

    lax.fori_loop(0, steps, step, 0)

    for q in range(reqs):
        rows = slice(q * tb, (q + 1) * tb)
        y = jnp.concatenate([y3[q * RWKV_HEADS + h] for h in range(RWKV_HEADS)], axis=-1)
        o_ref[rows, :] = _rwkv_output(y, bonus_s[rows, :], g_s[rows, :], lg_ref, lb_ref, ones_ref)


def _unit_lower_inverse_minus_identity(ns):
    size = ns[0].shape[0]
    t = lax.broadcasted_iota(jnp.int32, (size, size), 0)
    s = lax.broadcasted_iota(jnp.int32, (size, size), 1)
    first = ((t >> 1) == (s >> 1)) & (t > s)
    es = [-jnp.where(first, n, 0.0) for n in ns]
    blk, shift = 4, 2
    while blk <= RWKV_HEAD:
        half = blk // 2
        sel = ((t >> shift) == (s >> shift)) & ((t & (blk - 1)) >= half) & ((s & (blk - 1)) < half)
        cs = [jnp.where(sel, n, 0.0) for n in ns]
        zs = [c + _dot(c.astype(bf16), e.astype(bf16)) for c, e in zip(cs, es)]
        es = [e - z - _dot(e.astype(bf16), z.astype(bf16)) for e, z in zip(es, zs)]
        blk, shift = blk * 2, shift + 1
    return es


def _rwkv_chunk_kernel(*refs, tb):
    x_refs, p_refs, w_refs = refs[0:4], refs[4:8], refs[8:18]
    lg_ref, lb_ref, ones_ref, s0_ref, o_ref, s_ref = refs[18:24]
    carries = refs[24:28]
    sp = refs[28]
    i = pl.program_id(1)
    n_pairs = RWKV_HEADS // 2
    hd = RWKV_HEAD

    @pl.when(i == 0)
    def _():
        zero = jnp.zeros((hd, hd), f32)
        for p in range(n_pairs):
            top = jnp.concatenate([s0_ref[0, 2 * p], zero], axis=1)
            bot = jnp.concatenate([zero, s0_ref[0, 2 * p + 1]], axis=1)
            sp[p] = jnp.concatenate([top, bot], axis=0)
        for carry, p_ref in zip(carries, p_refs):
            carry[...] = p_ref[0]

    xs = [x[...] for x in x_refs]
    prevs = [carry[...] for carry in carries]
    for carry, x in zip(carries, xs):
        carry[...] = x[tb - 1:tb, :]
    r, k, v, kk, a, lw, g, bonus = _rwkv_features(xs, prevs, w_refs, ones_ref)
    beta = kk * a
    row = lax.broadcasted_iota(jnp.int32, lw.shape, 0)
    cum = lw
    d = 1
    while d < tb:
        cum = cum + jnp.where(row >= d, pltpu.roll(cum, d, 0), 0.0)
        d *= 2
    c_last = cum[tb - 1:tb, :]
    a_t = kk * jnp.exp(cum - lw)
    r_t = r * jnp.exp(cum)
    e_neg = jnp.exp(-cum)
    k_t = k * e_neg
    b_t = beta * e_neg
    e_hat = jnp.exp(c_last - cum)
    k_h = k * e_hat
    b_h = beta * e_hat
    gamma = jnp.exp(c_last)

    lane = lax.broadcasted_iota(jnp.int32, (tb, LANES), 1)
    head0 = lane < hd
    split = lambda x: (jnp.where(head0, x, 0.0), jnp.where(head0, 0.0, x))
    tt = lax.broadcasted_iota(jnp.int32, (LANES, LANES), 0)
    ss = lax.broadcasted_iota(jnp.int32, (LANES, LANES), 1)
    strict = tt > ss
    incl = tt >= ss
    same_head = (tt >= hd) == (ss >= hd)
    fold = lambda m: m[:tb, :] + m[tb:, :]

    pairs = range(n_pairs)
    cols = [slice(p * LANES, (p + 1) * LANES) for p in pairs]
    stack = lambda *xs: jnp.concatenate(xs, axis=0)
    scs = [_dot_nt(stack(*split(a_t[:, ps]), *split(r_t[:, ps])).astype(bf16),
                   stack(*split(k_t[:, ps]), *split(b_t[:, ps])).astype(bf16)) for ps in cols]
    es = _unit_lower_inverse_minus_identity([jnp.where(strict, sc[:2 * tb, 2 * tb:], 0.0) for sc in scs])
    eye = jnp.where(tt == ss, 1.0, 0.0)
    t_cats = [fold(eye + e).astype(bf16) for e in es]
    ak_cats = [fold(jnp.where(strict, sc[:2 * tb, :2 * tb], 0.0)).astype(bf16) for sc in scs]
    r_cats = [jnp.concatenate([fold(jnp.where(incl, sc[2 * tb:, :2 * tb], 0.0)),
                               -fold(jnp.where(incl, sc[2 * tb:, 2 * tb:], 0.0))], axis=1).astype(bf16)
              for sc in scs]
    s_olds = [sp[p] for p in pairs]
    grs = [_dot_nt(stack(a_t[:, ps], r_t[:, ps]).astype(bf16), s_old.astype(bf16))
           for ps, s_old in zip(cols, s_olds)]
    v_sts = [stack(*split(v[:, ps])).astype(bf16) for ps in cols]
    u_rhss = [gr[:tb] + _dot(ak, v_st) for gr, ak, v_st in zip(grs, ak_cats, v_sts)]
    us = [_dot(t_cat, stack(*split(u_rhs)).astype(bf16)) for t_cat, u_rhs in zip(t_cats, u_rhss)]
    ys = [gr[tb:] + _dot(r_cat, stack(v_st, stack(*split(u)).astype(bf16)))
          for gr, r_cat, v_st, u in zip(grs, r_cats, v_sts, us)]
    for p, ps, s_old, u in zip(pairs, cols, s_olds, us):
        vu_t = stack(v[:, ps], -u).T.astype(bf16)
        kb = stack(k_h[:, ps], b_h[:, ps]).astype(bf16)
        sp[p] = s_old * gamma[:, ps] + jnp.where(same_head, _dot(vu_t, kb), 0.0)

    o_ref[...] = _rwkv_output(jnp.concatenate(ys, axis=-1), bonus, g, lg_ref, lb_ref, ones_ref)

    @pl.when(i == pl.num_programs(1) - 1)
    def _():
        for p in range(n_pairs):
            full = sp[p]
            s_ref[0, 2 * p] = full[:hd, :hd]
            s_ref[0, 2 * p + 1] = full[hd:, hd:]


def rwkv(u, prev, p, s0, layer, ones2, batch, l_pad, l_real, chunked, reqs=1):
    m = u.shape[0]
    bw = BRANCH_WIDTH
    full = lambda w: pl.BlockSpec((1, w), lambda b, i: (0, 0))
    if chunked:
        tb, nb, nblk = RWKV_HEAD, 1, l_pad // RWKV_HEAD
        assert l_real == l_pad and l_pad % tb == 0
        body = functools.partial(_rwkv_chunk_kernel, tb=tb)
        scratch = [pltpu.VMEM((1, bw), f32), pltpu.VMEM((1, bw), f32), pltpu.VMEM((1, bw), f32),
                   pltpu.VMEM((1, RWKV_LORA), f32), pltpu.VMEM((RWKV_HEADS // 2, LANES, LANES), f32)]
    else:
        tb, nb, nblk = l_pad, reqs, 1
        assert batch % reqs == 0
        body = functools.partial(_rwkv_seq_kernel, tb=tb, steps=l_real, reqs=reqs)
        head_buf = pltpu.VMEM((reqs * RWKV_HEADS, tb, RWKV_HEAD), f32)
        scratch = [head_buf] * 7 + [pltpu.VMEM((reqs * tb, bw), f32), pltpu.VMEM((reqs * tb, bw), f32)]
    rows = nb * tb
    row = lambda b, i: b * nblk + i
    prev_spec = lambda w: pl.BlockSpec((None, nb, 1, w), lambda b, i: (layer, b, 0, 0))
    return pl.pallas_call(
        body,
        grid=(batch // nb, nblk),
        in_specs=[pl.BlockSpec((rows, bw), lambda b, i: (row(b, i), C_R // bw)),
                  pl.BlockSpec((rows, bw), lambda b, i: (row(b, i), C_K // bw)),
                  pl.BlockSpec((rows, bw), lambda b, i: (row(b, i), C_V // bw)),
                  pl.BlockSpec((rows, RWKV_LORA), lambda b, i: (row(b, i), C_LORA // RWKV_LORA)),
                  prev_spec(bw), prev_spec(bw), prev_spec(bw), prev_spec(RWKV_LORA),
                  full(bw), full(bw), full(bw), full(RWKV_LORA),
                  full(bw), full(bw),
                  pl.BlockSpec((RWKV_LORA, 3 * bw), lambda b, i: (0, 0)),
                  full(bw), full(bw), full(bw), full(bw), full(bw),
                  pl.BlockSpec((LANES, LANES), lambda b, i: (0, 0)),
                  pl.BlockSpec((None, nb, RWKV_HEADS, RWKV_HEAD, RWKV_HEAD), lambda b, i: (layer, b, 0, 0, 0))],
        out_specs=[pl.BlockSpec((rows, bw), lambda b, i: (row(b, i), 0)),
                   pl.BlockSpec((nb, RWKV_HEADS, RWKV_HEAD, RWKV_HEAD), lambda b, i: (b, 0, 0, 0))],
        out_shape=[jax.ShapeDtypeStruct((m, bw), bf16),
                   jax.ShapeDtypeStruct((batch, RWKV_HEADS, RWKV_HEAD, RWKV_HEAD), f32)],
        scratch_shapes=scratch,
        compiler_params=_params(("parallel", "arbitrary")),
        name="rwkv7_chunked" if chunked else "rwkv7_seq",
    )(u, u, u, u, *prev, p['mu_r'], p['mu_k'], p['mu_v'], p['mu_lo'], p['w0'], p['a0'], p['w_lora'],
      p['k_k'], p['k_a'], p['r_k'], p['ln_g'], p['ln_b'], ones2, s0)


def _pack_w_in_t(w_in):
    wt = jnp.swapaxes(w_in, 1, 2)
    seg = lambda o, n: wt[:, o:o + n]
    parts = [seg(_O_GV, 1024), seg(_O_GR, 1024), seg(_O_SQ, 1024), seg(_O_MQ, 1024),
             seg(_O_RU, 3072), seg(_O_GPRE, 8192), seg(_O_GQ, 512), seg(_O_GK, 512),
             seg(_O_RU + 3072, RWKV_LORA), seg(_O_SK, 128), seg(_O_SV, 128), seg(_O_GA, GLA_GATE_RANK),
             jnp.zeros((wt.shape[0], N_PACK - C_GA - GLA_GATE_RANK, wt.shape[2]), wt.dtype)]
    return jnp.concatenate(parts, axis=1).astype(bf16)


def _layer_params(l, w):
    row = lambda x: x[l].reshape(1, -1)
    mu = w['rwkv_mu'][l]
    z = lambda r, c: jnp.zeros((r, c), f32)
    w_lora = jnp.concatenate([
        jnp.concatenate([w['rwkv_w2'][l], z(64, 1024), z(64, 1024)], axis=1),
        jnp.concatenate([z(64, 1024), w['rwkv_a2'][l], z(64, 1024)], axis=1),
        jnp.concatenate([z(128, 1024), z(128, 1024), w['rwkv_g2'][l]], axis=1)], axis=0).astype(bf16)
    a_up = jnp.concatenate([w['gla_a_up'][l], z(LANES - GLA_GATE_RANK, GLA_QK)], axis=0).astype(bf16)
    return {
        'a_up': a_up, 'a_b': row(w['gla_a_b']),
        'gla_g': row(w['gla_norm_g']), 'gla_b': row(w['gla_norm_b']),
        'sinks': w['swa_sinks'][l],
        'rwkv': {'mu_r': mu[:1024].reshape(1, -1), 'mu_k': mu[1024:2048].reshape(1, -1),
                 'mu_v': mu[2048:3072].reshape(1, -1), 'mu_lo': mu[3072:].reshape(1, -1),
                 'w0': row(w['rwkv_w0']), 'a0': row(w['rwkv_a0']), 'w_lora': w_lora,
                 'k_k': row(w['rwkv_k_k']), 'k_a': row(w['rwkv_k_a']), 'r_k': row(w['rwkv_r_k']),
                 'ln_g': row(w['rwkv_ln_g']), 'ln_b': row(w['rwkv_ln_b'])},
        'ln1_g': row(w['ln1_g']), 'ln1_b': row(w['ln1_b']),
        'ln2_g': row(w['ln2_g']), 'ln2_b': row(w['ln2_b']),
    }


def _tiles(m):
    if m >= 1024:
        return {'proj': 2048, 'merge': 512, 'ffn': 2048, 'ln': 256}
    return {'proj': m, 'merge': m, 'ffn': m, 'ln': m}


def _trunk_layer(h_f, h_b, l, p, big, ones2, mem_kv, mem_cache, gla_s0, rwkv_s0, rwkv_prev, swa_buf,
                 batch, l_pad, l_real):
    m = h_f.shape[0]
    tl = _tiles(m)
    u = matmul(h_b, big['w_in_t'], l, tl['proj'], 1024, "in_proj", trans_w=True)
    o_a, gla_s = gla(u, p['a_up'], p['a_b'], p['gla_g'], p['gla_b'], gla_s0[0], gla_s0[1], batch, l_pad, l_real,
                     min(GLA_CHUNK, l_pad))
    if swa_buf is None:
        o_b = swa_prompt(u, p['sinks'], batch, l_pad)
    else:
        o_b = swa_sample(u, swa_buf[0], swa_buf[1], l, p['sinks'], batch, l_pad, l_real, 4)
    chunked = l_real == l_pad and l_pad % RWKV_HEAD == 0
    o_c, rwkv_s = rwkv(u, rwkv_prev[0], p['rwkv'], rwkv_s0[0], rwkv_s0[1], ones2, batch, l_pad, l_real,
                       chunked, reqs=1 if chunked else 4)
    o_m = mem_attention(u, mem_kv, mem_cache, l, batch, l_pad, min(512, l_pad))
    merged = gated_merge((o_a, o_b, o_c, o_m), u, big['gate_b'], big['w_branch'], l, tl['merge'], 512)
    x_f, x_b = matmul_residual_ln(merged, big['w_out'], l, h_f, p['ln1_g'], p['ln1_b'], tl['ln'])
    act = ffn_up(x_b, big['w_gu'], l, tl['ffn'], 512)
    y_f, y_b = matmul_residual_ln(act, big['w_down'], l, x_f, p['ln2_g'], p['ln2_b'], tl['ln'])
    return y_f, y_b, u, gla_s, rwkv_s


def _split_ru(x):
    return (x[..., :1024], x[..., 1024:2048], x[..., 2048:3072], x[..., 3072:])


def kernel(x_prompt, x_sample, mem_prompt, cache_swa_k, cache_swa_v, cache_mem_k, cache_mem_v, state_gla, state_rwkv, state_rwkv_shift, w_in, gate_b, gla_a_up, gla_a_b, gla_norm_g, gla_norm_b, swa_sinks, rwkv_mu, rwkv_w0, rwkv_w2, rwkv_a0, rwkv_a2, rwkv_g2, rwkv_k_k, rwkv_k_a, rwkv_r_k, rwkv_ln_g, rwkv_ln_b, w_mem_kv, w_branch, w_out, ln1_g, ln1_b, w_gu, w_down, ln2_g, ln2_b):
    weights = {'gla_a_up': gla_a_up, 'gla_a_b': gla_a_b,
               'gla_norm_g': gla_norm_g, 'gla_norm_b': gla_norm_b, 'swa_sinks': swa_sinks,
               'rwkv_mu': rwkv_mu, 'rwkv_w0': rwkv_w0, 'rwkv_w2': rwkv_w2, 'rwkv_a0': rwkv_a0,
               'rwkv_a2': rwkv_a2, 'rwkv_g2': rwkv_g2, 'rwkv_k_k': rwkv_k_k, 'rwkv_k_a': rwkv_k_a,
               'rwkv_r_k': rwkv_r_k, 'rwkv_ln_g': rwkv_ln_g, 'rwkv_ln_b': rwkv_ln_b,
               'ln1_g': ln1_g, 'ln1_b': ln1_b, 'ln2_g': ln2_g, 'ln2_b': ln2_b}
    big = {'w_in_t': _pack_w_in_t(w_in), 'gate_b': gate_b, 'w_branch': w_branch, 'w_gu': w_gu,
           'w_out': w_out.astype(bf16), 'w_down': w_down.astype(bf16)}
    bp, lp, _ = x_prompt.shape
    bs, ls, _ = x_sample.shape
    ls_pad = -(-ls // SUBLANES) * SUBLANES
    mp, ms = bp * lp, bs * ls_pad

    hp_f = x_prompt.reshape(mp, D_MODEL)
    hs_f = jnp.pad(x_sample, ((0, 0), (0, ls_pad - ls), (0, 0))).reshape(ms, D_MODEL)
    hp_b, hs_b = hp_f.astype(bf16), hs_f.astype(bf16)
    mem_b = mem_prompt.reshape(bp * MEM_TOKENS, D_MODEL).astype(bf16)
    half = LANES // 2
    blk = jnp.ones((half, half), f32)
    zero = jnp.zeros((half, half), f32)
    ones2 = jnp.concatenate([jnp.concatenate([blk, zero], 1), jnp.concatenate([zero, blk], 1)], 0).astype(bf16)

    gla0_p = jnp.zeros((1, bp, GLA_HEADS, GLA_DK, GLA_DV), f32)
    rwkv0_p = jnp.zeros((1, bp, RWKV_HEADS, RWKV_HEAD, RWKV_HEAD), f32)
    prev0_p = _split_ru(jnp.zeros((1, bp, 1, RWKV_COLS), f32))
    prev_s = _split_ru(state_rwkv_shift)
    kvw = SWA_KV_HEADS * SWA_HEAD_DIM
    kbuf = cache_swa_k.reshape(DEPTH, bs * WINDOW, kvw)
    vbuf = cache_swa_v.reshape(DEPTH, bs * WINDOW, kvw)
    mem_k2d = cache_mem_k.reshape(DEPTH * bs * MEM_TOKENS, MEM_HEADS * MEM_HEAD_DIM)
    mem_v2d = cache_mem_v.reshape(DEPTH * bs * MEM_TOKENS, MEM_HEADS * MEM_HEAD_DIM)

    outs = {k: [] for k in ('p_swk', 'p_swv', 'p_mk', 'p_mv', 'p_gla', 'p_rw', 'p_rs',
                            's_swk', 's_swv', 's_gla', 's_rw', 's_rs')}
    for l in range(DEPTH):
        p = _layer_params(l, weights)
        kv = matmul(mem_b, w_mem_kv, l, bp * MEM_TOKENS, 512, "mem_kv")
        hp_f, hp_b, u, gs, rs = _trunk_layer(hp_f, hp_b, l, p, big, ones2, kv, None, (gla0_p, 0), (rwkv0_p, 0),
                                             (prev0_p, 0), None, bp, lp, lp)
        u3 = u.reshape(bp, lp, N_PACK)
        outs['p_swk'].append(u3[:, lp - WINDOW:, C_SK:C_SK + kvw].reshape(bp, WINDOW, SWA_KV_HEADS, SWA_HEAD_DIM))
        outs['p_swv'].append(u3[:, lp - WINDOW:, C_SV:C_SV + kvw].reshape(bp, WINDOW, SWA_KV_HEADS, SWA_HEAD_DIM))
        outs['p_mk'].append(kv[:, :1024].reshape(bp, MEM_TOKENS, MEM_HEADS, MEM_HEAD_DIM))
        outs['p_mv'].append(kv[:, 1024:].reshape(bp, MEM_TOKENS, MEM_HEADS, MEM_HEAD_DIM))
        outs['p_gla'].append(gs)
        outs['p_rw'].append(rs)
        outs['p_rs'].append(jnp.concatenate([u3[:, lp - 1:, C_R:C_R + 3072],
                                             u3[:, lp - 1:, C_LORA:C_LORA + RWKV_LORA]], axis=-1))
        hs_f, hs_b, u, gs, rs = _trunk_layer(hs_f, hs_b, l, p, big, ones2, None, (mem_k2d, mem_v2d),
                                             (state_gla, l), (state_rwkv, l), (prev_s, l), (kbuf, vbuf),
                                             bs, ls_pad, ls)
        u3 = u.reshape(bs, ls_pad, N_PACK)
        k_new = u3[:, :ls, C_SK:C_SK + kvw].reshape(bs, ls, SWA_KV_HEADS, SWA_HEAD_DIM)
        v_new = u3[:, :ls, C_SV:C_SV + kvw].reshape(bs, ls, SWA_KV_HEADS, SWA_HEAD_DIM)
        outs['s_swk'].append(jnp.concatenate([cache_swa_k[l][:, ls:], k_new], axis=1))
        outs['s_swv'].append(jnp.concatenate([cache_swa_v[l][:, ls:], v_new], axis=1))
        outs['s_gla'].append(gs)
        outs['s_rw'].append(rs)
        outs['s_rs'].append(jnp.concatenate([u3[:, ls - 1:ls, C_R:C_R + 3072],
                                             u3[:, ls - 1:ls, C_LORA:C_LORA + RWKV_LORA]], axis=-1))

    st = {k: jnp.stack(v) for k, v in outs.items()}
    y_prompt = hp_f.reshape(bp, lp, D_MODEL)
    y_sample = hs_f.reshape(bs, ls_pad, D_MODEL)[:, :ls]
    return (y_prompt, y_sample,
            st['p_swk'], st['p_swv'], st['p_mk'], st['p_mv'], st['p_gla'], st['p_rw'], st['p_rs'],
            st['s_swk'], st['s_swv'], st['s_gla'], st['s_rw'], st['s_rs'])
```

```python
import functools

import jax
import jax.numpy as jnp
from jax import lax
from jax.experimental import pallas as pl
from jax.experimental.pallas import tpu as pltpu

f32 = jnp.float32
bf16 = jnp.bfloat16

D_MODEL = 2048
DEPTH = 4
BRANCH_WIDTH = 1024
N_BRANCH = 4
GLA_HEADS = 4
GLA_QK = 512
GLA_V = 1024
GLA_DK = 128
GLA_DV = 256
GLA_GATE_RANK = 16
GLA_TAU = 16.0
GLA_CHUNK = 64
SWA_HEAD_DIM = 64
SWA_Q_HEADS = 16
SWA_KV_HEADS = 2
SWA_GROUP = SWA_Q_HEADS // SWA_KV_HEADS
WINDOW = 128
RWKV_HEAD = 64
RWKV_HEADS = 16
RWKV_LORA = 256
RWKV_COLS = 3 * BRANCH_WIDTH + RWKV_LORA
MEM_TOKENS = 256
MEM_HEADS = 4
MEM_HEAD_DIM = 256
D_FF = 5632
DEEPNORM_ALPHA = (2 * DEPTH) ** 0.25
NEG = -1e30

_O_GQ, _O_GK, _O_GV, _O_GR, _O_GA = 0, 512, 1024, 2048, 3072
_O_SQ, _O_SK, _O_SV = 3088, 4112, 4240
_O_RU = 4368
_O_MQ = 7696
_O_GPRE = 8720
_N_IN = 16912

C_GV, C_GR, C_SQ, C_MQ, C_R, C_K, C_V = 0, 1024, 2048, 3072, 4096, 5120, 6144
C_GPRE = 7168
C_GQ, C_GK = 15360, 15872
C_LORA = 16384
C_SK, C_SV, C_GA = 16640, 16768, 16896
N_PACK = 17408

LANES = 128
SUBLANES = 8
VMEM_BYTES_V7X = 64 * 1024 * 1024
VMEM_LIMIT = VMEM_BYTES_V7X - 8 * 1024 * 1024


def _params(sem):
    return pltpu.CompilerParams(dimension_semantics=sem, vmem_limit_bytes=VMEM_LIMIT)


def _softplus(z):
    return jnp.maximum(z, 0.0) + jnp.log(1.0 + jnp.exp(-jnp.abs(z)))


def _sigmoid(z):
    return 0.5 * jnp.tanh(0.5 * z) + 0.5


def _dot(a, b):
    return jnp.dot(a, b, preferred_element_type=f32)


def _dot_nt(a, b):
    return lax.dot_general(a, b, (((1,), (1,)), ((), ())), preferred_element_type=f32)


def _mm_kernel(x_ref, w_ref, o_ref, *, trans_w):
    mm = _dot_nt if trans_w else _dot
    o_ref[...] = mm(x_ref[...], w_ref[...].astype(bf16))


def matmul(x, w, layer, tm, tn, name, trans_w=False):
    m, k = x.shape
    n = w.shape[1] if trans_w else w.shape[2]
    w_spec = (pl.BlockSpec((None, tn, k), lambda i, j: (layer, j, 0)) if trans_w
              else pl.BlockSpec((None, k, tn), lambda i, j: (layer, 0, j)))
    return pl.pallas_call(
        functools.partial(_mm_kernel, trans_w=trans_w),
        grid=(m // tm, n // tn),
        in_specs=[pl.BlockSpec((tm, k), lambda i, j: (i, 0)), w_spec],
        out_specs=pl.BlockSpec((tm, tn), lambda i, j: (i, j)),
        out_shape=jax.ShapeDtypeStruct((m, n), f32),
        compiler_params=_params(("parallel", "parallel")),
        name=name,
    )(x, w)


def _mm_ln_kernel(x_ref, w_ref, res_ref, g_ref, b_ref, of_ref, ob_ref):
    z = DEEPNORM_ALPHA * res_ref[...] + _dot(x_ref[...], w_ref[...])
    mu = jnp.mean(z, axis=-1, keepdims=True)
    d = z - mu
    var = jnp.mean(d * d, axis=-1, keepdims=True)
    y = d * lax.rsqrt(var + 1e-5) * g_ref[...] + b_ref[...]
    of_ref[...] = y
    ob_ref[...] = y.astype(bf16)


def matmul_residual_ln(x, w, layer, res, g, b, tm):
    m, k = x.shape
    n = w.shape[2]
    return pl.pallas_call(
        _mm_ln_kernel,
        grid=(m // tm,),
        in_specs=[pl.BlockSpec((tm, k), lambda i: (i, 0)),
                  pl.BlockSpec((None, k, n), lambda i: (layer, 0, 0), pipeline_mode=pl.Buffered(1)),
                  pl.BlockSpec((tm, n), lambda i: (i, 0)),
                  pl.BlockSpec((1, n), lambda i: (0, 0)),
                  pl.BlockSpec((1, n), lambda i: (0, 0))],
        out_specs=[pl.BlockSpec((tm, n), lambda i: (i, 0)),
                   pl.BlockSpec((tm, n), lambda i: (i, 0))],
        out_shape=[jax.ShapeDtypeStruct((m, n), f32), jax.ShapeDtypeStruct((m, n), bf16)],
        compiler_params=_params(("parallel",)),
        name="proj_ln",
    )(x, w, res, g, b)


def _merge_kernel(a_ref, b_ref, c_ref, m_ref, g0_ref, g1_ref, g2_ref, g3_ref, gb_ref, w_ref, o_ref, w_bf):
    @pl.when(pl.program_id(1) == 0)
    def _():
        w_bf[...] = w_ref[...].astype(bf16)

    acc = None
    for n, (br, gp) in enumerate(((a_ref, g0_ref), (b_ref, g1_ref), (c_ref, g2_ref), (m_ref, g3_ref))):
        y = _dot(br[...], w_bf[n])
        gate = _sigmoid(gp[...] + gb_ref[n:n + 1, :])
        acc = gate * y if acc is None else acc + gate * y
    o_ref[...] = acc.astype(bf16)


def gated_merge(branches, u, gate_b, w_branch, layer, tm, tn):
    m = u.shape[0]
    gp0 = C_GPRE // tn
    per = D_MODEL // tn
    br_spec = pl.BlockSpec((tm, BRANCH_WIDTH), lambda j, i: (i, 0))
    gp_specs = [pl.BlockSpec((tm, tn), functools.partial(lambda j, i, n: (i, gp0 + n * per + j), n=n))
                for n in range(N_BRANCH)]
    return pl.pallas_call(
        _merge_kernel,
        grid=(D_MODEL // tn, m // tm),
        in_specs=[br_spec] * 4 + gp_specs + [
            pl.BlockSpec((None, N_BRANCH, tn), lambda j, i: (layer, 0, j)),
            pl.BlockSpec((None, N_BRANCH, BRANCH_WIDTH, tn), lambda j, i: (layer, 0, 0, j))],
        out_specs=pl.BlockSpec((tm, tn), lambda j, i: (i, j)),
        out_shape=jax.ShapeDtypeStruct((m, D_MODEL), bf16),
        scratch_shapes=[pltpu.VMEM((N_BRANCH, BRANCH_WIDTH, tn), bf16)],
        compiler_params=_params(("parallel", "arbitrary")),
        name="gated_merge",
    )(*branches, u, u, u, u, gate_b, w_branch)


def _ffn_up_kernel(x_ref, wg_ref, wu_ref, o_ref, wg_bf, wu_bf):
    @pl.when(pl.program_id(1) == 0)
    def _():
        wg_bf[...] = wg_ref[...].astype(bf16)
        wu_bf[...] = wu_ref[...].astype(bf16)

    x = x_ref[...]
    g = _dot(x, wg_bf[...])
    up = _dot(x, wu_bf[...])
    o_ref[...] = (g * _sigmoid(g) * up).astype(bf16)


def ffn_up(x, w_gu, layer, tm, tn):
    m, k = x.shape
    nj = D_FF // tn
    return pl.pallas_call(
        _ffn_up_kernel,
        grid=(nj, m // tm),
        in_specs=[pl.BlockSpec((tm, k), lambda j, i: (i, 0)),
                  pl.BlockSpec((None, k, tn), lambda j, i: (layer, 0, j)),
                  pl.BlockSpec((None, k, tn), lambda j, i: (layer, 0, nj + j))],
        out_specs=pl.BlockSpec((tm, tn), lambda j, i: (i, j)),
        out_shape=jax.ShapeDtypeStruct((m, D_FF), bf16),
        scratch_shapes=[pltpu.VMEM((k, tn), bf16), pltpu.VMEM((k, tn), bf16)],
        compiler_params=_params(("parallel", "arbitrary")),
        name="ffn_up",
    )(x, w_gu, w_gu)


def _gla_kernel(q_ref, k_ref, v_ref, r_ref, a_ref, aup_ref, ab_ref, ng_ref, nb_ref, s0_ref,
                o_ref, s_ref, *, chunk, sub, l_real, l_pad):
    @pl.when(pl.program_id(1) == 0)
    def _():
        s_ref[...] = s0_ref[...]

    for ci in range(sub):
        rows = slice(ci * chunk, (ci + 1) * chunk)
        refs = [x.at[rows, :] for x in (q_ref, k_ref, v_ref, r_ref, a_ref, o_ref)]
        _gla_chunk(*refs[:5], aup_ref, ab_ref, ng_ref, nb_ref, refs[5], s_ref,
                   (pl.program_id(1) * sub + ci) * chunk, chunk, l_real, l_pad)


def _gla_chunk(q_ref, k_ref, v_ref, r_ref, a_ref, aup_ref, ab_ref, ng_ref, nb_ref, o_ref, s_ref,
               t0, chunk, l_real, l_pad):
    la = _dot(a_ref[...].astype(bf16), aup_ref[...]) + ab_ref[...]
    la = (jnp.minimum(la, 0.0) - jnp.log(1.0 + jnp.exp(-jnp.abs(la)))) * (1.0 / GLA_TAU)
    row = lax.broadcasted_iota(jnp.int32, (chunk, GLA_QK), 0)
    kin = k_ref[...]
    if l_real < l_pad:
        real = (t0 + row) < l_real
        la = jnp.where(real, la, 0.0)
        kin = jnp.where(real, kin, 0.0)
    b = la
    d = 1
    while d < chunk:
        b = b + jnp.where(row >= d, pltpu.roll(b, d, 0), 0.0)
        d *= 2
    b_last = b[chunk - 1:chunk, :]
    q_dec = q_ref[...] * (GLA_DK ** -0.5) * jnp.exp(b)
    k_inv = kin * jnp.exp(-b)
    k_dec = kin * jnp.exp(b_last - b)
    tt = lax.broadcasted_iota(jnp.int32, (chunk, chunk), 0)
    ss = lax.broadcasted_iota(jnp.int32, (chunk, chunk), 1)
    causal = tt >= ss
    heads = range(GLA_HEADS)
    kss = [slice(h * GLA_DK, (h + 1) * GLA_DK) for h in heads]
    vss = [slice(h * GLA_DV, (h + 1) * GLA_DV) for h in heads]
    qhs = [q_dec[:, ks].astype(bf16) for ks in kss]
    atts = [jnp.where(causal, _dot_nt(qh, k_inv[:, ks].astype(bf16)), 0.0).astype(bf16) for qh, ks in zip(qhs, kss)]
    vhs = [v_ref[:, vs].astype(bf16) for vs in vss]
    s_olds = [s_ref[0, h] for h in heads]
    os_ = [_dot(att, vh) + _dot(qh, s_old.astype(bf16)) for att, vh, qh, s_old in zip(atts, vhs, qhs, s_olds)]
    decay_cols = [jnp.exp(jnp.sum(la[:, ks].T, axis=1, keepdims=True)) for ks in kss]
    for h, ks, vh, s_old, decay_col in zip(heads, kss, vhs, s_olds, decay_cols):
        s_ref[0, h] = decay_col * s_old + _dot(k_dec[:, ks].T.astype(bf16), vh)
    for vs, o in zip(vss, os_):
        mu = jnp.mean(o, axis=-1, keepdims=True)
        dd = o - mu
        var = jnp.mean(dd * dd, axis=-1, keepdims=True)
        nrm = dd * lax.rsqrt(var + 1e-5) * ng_ref[:, vs] + nb_ref[:, vs]
        gr = r_ref[:, vs]
        o_ref[:, vs] = (nrm * (gr * _sigmoid(gr))).astype(bf16)


def gla(u, a_up, a_b, norm_g, norm_b, s0, layer, batch, l_pad, l_real, chunk):
    m = u.shape[0]
    sub = next(s for s in (4, 2, 1) if l_pad % (s * chunk) == 0)
    rows = sub * chunk
    nc = l_pad // rows
    row = lambda b, c: b * nc + c
    return pl.pallas_call(
        functools.partial(_gla_kernel, chunk=chunk, sub=sub, l_real=l_real, l_pad=l_pad),
        grid=(batch, nc),
        in_specs=[pl.BlockSpec((rows, GLA_QK), lambda b, c: (row(b, c), C_GQ // GLA_QK)),
                  pl.BlockSpec((rows, GLA_QK), lambda b, c: (row(b, c), C_GK // GLA_QK)),
                  pl.BlockSpec((rows, GLA_V), lambda b, c: (row(b, c), C_GV // GLA_V)),
                  pl.BlockSpec((rows, GLA_V), lambda b, c: (row(b, c), C_GR // GLA_V)),
                  pl.BlockSpec((rows, LANES), lambda b, c: (row(b, c), C_GA // LANES)),
                  pl.BlockSpec((LANES, GLA_QK), lambda b, c: (0, 0)),
                  pl.BlockSpec((1, GLA_QK), lambda b, c: (0, 0)),
                  pl.BlockSpec((1, GLA_V), lambda b, c: (0, 0)),
                  pl.BlockSpec((1, GLA_V), lambda b, c: (0, 0)),
                  pl.BlockSpec((None, 1, GLA_HEADS, GLA_DK, GLA_DV), lambda b, c: (layer, b, 0, 0, 0))],
        out_specs=[pl.BlockSpec((rows, GLA_V), lambda b, c: (row(b, c), 0)),
                   pl.BlockSpec((1, GLA_HEADS, GLA_DK, GLA_DV), lambda b, c: (b, 0, 0, 0))],
        out_shape=[jax.ShapeDtypeStruct((m, GLA_V), bf16),
                   jax.ShapeDtypeStruct((batch, GLA_HEADS, GLA_DK, GLA_DV), f32)],
        compiler_params=_params(("parallel", "arbitrary")),
        name="gla",
    )(u, u, u, u, u, a_up, a_b, norm_g, norm_b, s0)


def _alibi_slopes():
    return 2.0 ** (-8.0 * jnp.arange(1, SWA_Q_HEADS + 1, dtype=f32) / SWA_Q_HEADS)


def _swa_bias(t_pos, key_pos, key_ok):
    dist = (t_pos[:, None] - key_pos[None, :]).astype(f32)
    valid = (dist >= 0) & (dist <= WINDOW) & key_ok[None, :]
    slopes = _alibi_slopes().reshape(SWA_KV_HEADS, SWA_GROUP, 1, 1)
    bias = jnp.where(valid[None, None], -slopes * dist[None, None], NEG)
    return bias.reshape(SWA_KV_HEADS, SWA_GROUP * t_pos.shape[0], key_pos.shape[0])


def _swa_groups(q_all, keys, vals, biases, sink_ref, rows):
    kvs = range(SWA_KV_HEADS)
    d = SWA_HEAD_DIM
    pairs = SWA_GROUP // 2
    heads = [range(kv * SWA_GROUP, (kv + 1) * SWA_GROUP) for kv in kvs]
    low = lax.broadcasted_iota(jnp.int32, (rows, LANES), 1) < d

    def both_halves(x, kv):
        swapped = pltpu.roll(x, d, 1)
        first = lax.broadcasted_iota(jnp.int32, x.shape, 1) < d
        return jnp.where(first, x, swapped) if kv == 0 else jnp.where(first, swapped, x)

    def stacked_queries(kv):
        parts = []
        for p in range(kv * pairs, (kv + 1) * pairs):
            q_pair = q_all[:, p * LANES:(p + 1) * LANES]
            parts += [jnp.where(low, q_pair, 0.0), jnp.where(low, 0.0, q_pair)]
        return jnp.concatenate(parts, axis=0).astype(bf16)

    qs = [stacked_queries(kv) for kv in kvs]
    k2s = [jnp.concatenate([both_halves(kb, kv) for kb in keys], axis=0).astype(bf16) for kv in kvs]
    v2s = [jnp.concatenate([both_halves(vb, kv) for vb in vals], axis=0).astype(bf16) for kv in kvs]
    ss = [_dot_nt(q, k2) * (d ** -0.5) + bias for q, k2, bias in zip(qs, k2s, biases)]
    sinks = [jnp.concatenate([jnp.full((rows, 1), sink_ref[h], f32) for h in hh], axis=0) for hh in heads]
    ms = [jnp.maximum(jnp.max(s, axis=-1, keepdims=True), sink) for s, sink in zip(ss, sinks)]
    ps = [jnp.exp(s - m) for s, m in zip(ss, ms)]
    dens = [jnp.sum(p, axis=-1, keepdims=True) + jnp.exp(sink - m) for p, sink, m in zip(ps, sinks, ms)]
    os_ = [_dot(p.astype(bf16), v2) / den for p, v2, den in zip(ps, v2s, dens)]
    return [jnp.where(low, o[(2 * p) * rows:(2 * p + 1) * rows, :], o[(2 * p + 1) * rows:(2 * p + 2) * rows, :])
            for o in os_ for p in range(pairs)]


def _swa_prompt_kernel(sink_ref, bias_ref, q_ref, kc_ref, kp_ref, vc_ref, vp_ref, o_ref):
    keys = (kp_ref[...], kc_ref[...])
    vals = (vp_ref[...], vc_ref[...])
    biases = [bias_ref[0, kv] for kv in range(SWA_KV_HEADS)]
    outs = _swa_groups(q_ref[...], keys, vals, biases, sink_ref, WINDOW)
    o_ref[...] = jnp.concatenate(outs, axis=-1).astype(bf16)


def swa_prompt(u, sinks, batch, seq):
    m = u.shape[0]
    nb = seq // WINDOW
    cur = lambda b, i: b * nb + i
    prev = lambda b, i: b * nb + jnp.maximum(i - 1, 0)
    kcol, vcol = C_SK // LANES, C_SV // LANES
    t_pos = jnp.arange(WINDOW)
    key_pos = jnp.arange(2 * WINDOW) - WINDOW
    bias = jnp.stack([_swa_bias(t_pos, key_pos, key_pos >= 0), _swa_bias(t_pos, key_pos, key_pos >= -WINDOW)])
    return pl.pallas_call(
        _swa_prompt_kernel,
        grid=(batch, nb),
        in_specs=[pl.BlockSpec(memory_space=pltpu.SMEM),
                  pl.BlockSpec((1,) + bias.shape[1:], lambda b, i: (jnp.minimum(i, 1), 0, 0, 0)),
                  pl.BlockSpec((WINDOW, BRANCH_WIDTH), lambda b, i: (cur(b, i), C_SQ // BRANCH_WIDTH)),
                  pl.BlockSpec((WINDOW, LANES), lambda b, i: (cur(b, i), kcol)),
                  pl.BlockSpec((WINDOW, LANES), lambda b, i: (prev(b, i), kcol)),
                  pl.BlockSpec((WINDOW, LANES), lambda b, i: (cur(b, i), vcol)),
                  pl.BlockSpec((WINDOW, LANES), lambda b, i: (prev(b, i), vcol))],
        out_specs=pl.BlockSpec((WINDOW, BRANCH_WIDTH), lambda b, i: (cur(b, i), 0)),
        out_shape=jax.ShapeDtypeStruct((m, BRANCH_WIDTH), bf16),
        compiler_params=_params(("parallel", "parallel")),
        name="swa_prompt",
    )(sinks, bias, u, u, u, u, u)


def _swa_sample_kernel(sink_ref, bias_ref, q_ref, kn_ref, vn_ref, kb_ref, vb_ref, o_ref, *, l_pad, reqs):
    for r in range(reqs):
        rs = slice(r * l_pad, (r + 1) * l_pad)
        bs = slice(r * WINDOW, (r + 1) * WINDOW)
        keys = (kb_ref[bs, :], kn_ref[rs, :])
        vals = (vb_ref[bs, :], vn_ref[rs, :])
        biases = [bias_ref[kv] for kv in range(SWA_KV_HEADS)]
        outs = _swa_groups(q_ref[rs, :], keys, vals, biases, sink_ref, l_pad)
        o_ref[rs, :] = jnp.concatenate(outs, axis=-1).astype(bf16)


def swa_sample(u, kbuf, vbuf, layer, sinks, batch, l_pad, l_real, reqs):
    m = u.shape[0]
    kcol, vcol = C_SK // LANES, C_SV // LANES
    key_pos = jnp.concatenate([jnp.arange(WINDOW) - WINDOW, jnp.arange(l_pad)])
    key_ok = jnp.concatenate([jnp.ones((WINDOW,), bool), jnp.arange(l_pad) < l_real])
    bias = _swa_bias(jnp.arange(l_pad), key_pos, key_ok)
    return pl.pallas_call(
        functools.partial(_swa_sample_kernel, l_pad=l_pad, reqs=reqs),
        grid=(batch // reqs,),
        in_specs=[pl.BlockSpec(memory_space=pltpu.SMEM),
                  pl.BlockSpec(bias.shape, lambda b: (0, 0, 0)),
                  pl.BlockSpec((reqs * l_pad, BRANCH_WIDTH), lambda b: (b, C_SQ // BRANCH_WIDTH)),
                  pl.BlockSpec((reqs * l_pad, LANES), lambda b: (b, kcol)),
                  pl.BlockSpec((reqs * l_pad, LANES), lambda b: (b, vcol)),
                  pl.BlockSpec((None, reqs * WINDOW, LANES), lambda b: (layer, b, 0)),
                  pl.BlockSpec((None, reqs * WINDOW, LANES), lambda b: (layer, b, 0))],
        out_specs=pl.BlockSpec((reqs * l_pad, BRANCH_WIDTH), lambda b: (b, 0)),
        out_shape=jax.ShapeDtypeStruct((m, BRANCH_WIDTH), bf16),
        compiler_params=_params(("parallel",)),
        name="swa_sample",
    )(sinks, bias, u, u, u, kbuf, vbuf)


def _mem_kernel(q_ref, k_ref, v_ref, o_ref):
    for h in range(MEM_HEADS):
        hs = slice(h * MEM_HEAD_DIM, (h + 1) * MEM_HEAD_DIM)
        s = _dot_nt(q_ref[:, hs].astype(bf16), k_ref[:, hs].astype(bf16)) * (MEM_HEAD_DIM ** -0.5)
        p = jnp.exp(s - jnp.max(s, axis=-1, keepdims=True))
        o = _dot(p.astype(bf16), v_ref[:, hs].astype(bf16)) / jnp.sum(p, axis=-1, keepdims=True)
        o_ref[:, hs] = o.astype(bf16)


def mem_attention(u, kv, cache, layer, batch, l_pad, tl):
    m = u.shape[0]
    nl = l_pad // tl
    width = MEM_HEADS * MEM_HEAD_DIM
    if cache is None:
        kv_args = (kv, kv)
        kv_specs = [pl.BlockSpec((MEM_TOKENS, width), lambda b, i: (b, 0)),
                    pl.BlockSpec((MEM_TOKENS, width), lambda b, i: (b, 1))]
    else:
        kv_args = cache
        kv_specs = [pl.BlockSpec((MEM_TOKENS, width), lambda b, i: (layer * batch + b, 0))] * 2
    return pl.pallas_call(
        _mem_kernel,
        grid=(batch, nl),
        in_specs=[pl.BlockSpec((tl, width), lambda b, i: (b * nl + i, C_MQ // width))] + kv_specs,
        out_specs=pl.BlockSpec((tl, width), lambda b, i: (b * nl + i, 0)),
        out_shape=jax.ShapeDtypeStruct((m, width), bf16),
        compiler_params=_params(("parallel", "parallel")),
        name="mem_attention",
    )(u, *kv_args)


def _seg64_sum(x, ones_ref):
    hi = x.astype(bf16)
    lo = (x - hi.astype(f32)).astype(bf16)
    cols = []
    for j in range(x.shape[1] // LANES):
        js = slice(j * LANES, (j + 1) * LANES)
        cols.append(_dot(hi[:, js], ones_ref[...]) + _dot(lo[:, js], ones_ref[...]))
    return jnp.concatenate(cols, axis=-1)


def _rwkv_features(xs, prevs, w_refs, ones_ref):
    mur_ref, muk_ref, muv_ref, mulo_ref, w0_ref, a0_ref, wl_ref, kk_ref, ka_ref, rk_ref = w_refs

    def token_shift(x, prev, mu_ref):
        row = lax.broadcasted_iota(jnp.int32, x.shape, 0)
        shifted = jnp.where(row == 0, prev, pltpu.roll(x, 1, 0))
        return x + (shifted - x) * mu_ref[...]

    r, k0, v, lo = (token_shift(x, pv, mu) for x, pv, mu in zip(xs, prevs, (mur_ref, muk_ref, muv_ref, mulo_ref)))
    col = lax.broadcasted_iota(jnp.int32, lo.shape, 1)
    act = jnp.where(col < 64, jnp.tanh(lo), jnp.where(col < 128, lo, _sigmoid(lo)))
    proj = _dot(act.astype(bf16), wl_ref[...])
    log_w = -_softplus(-(w0_ref[...] + proj[:, :BRANCH_WIDTH])) - 0.5
    log_decay = -jnp.exp(log_w)
    a = _sigmoid(a0_ref[...] + proj[:, BRANCH_WIDTH:2 * BRANCH_WIDTH])
    g = proj[:, 2 * BRANCH_WIDTH:]
    kk = k0 * kk_ref[...]
    k = k0 * (1.0 + (a - 1.0) * ka_ref[...])
    tb = kk.shape[0]
    head_sums = _seg64_sum(jnp.concatenate([kk * kk, r * k * rk_ref[...]], axis=0), ones_ref)
    kk = kk / jnp.maximum(jnp.sqrt(head_sums[:tb]), 1e-12)
    bonus = head_sums[tb:] * v
    return r, k, v, kk, a, log_decay, g, bonus


def _rwkv_output(y, bonus, g, lg_ref, lb_ref, ones_ref):
    mu = _seg64_sum(y, ones_ref) * (1.0 / RWKV_HEAD)
    d = y - mu
    var = _seg64_sum(d * d, ones_ref) * (1.0 / RWKV_HEAD)
    yn = d * lax.rsqrt(var + 64e-5) * lg_ref[...] + lb_ref[...]
    return ((yn + bonus) * g).astype(bf16)


def _rwkv_seq_kernel(*refs, tb, steps, reqs):
    x_refs, p_refs, w_refs = refs[0:4], refs[4:8], refs[8:18]
    lg_ref, lb_ref, ones_ref, s0_ref, o_ref, s_ref = refs[18:24]
    w3, kk3, kka3, k3, r3, v3, y3, bonus_s, g_s = refs[24:]
    s_ref[...] = s0_ref[...]
    y3[...] = jnp.zeros_like(y3)

    for q in range(reqs):
        rows = slice(q * tb, (q + 1) * tb)
        xs = [x[rows, :] for x in x_refs]
        prevs = [p[q] for p in p_refs]
        r, k, v, kk, a, log_decay, g, bonus = _rwkv_features(xs, prevs, w_refs, ones_ref)
        decay = jnp.exp(log_decay)
        kka = kk * a
        g_s[rows, :] = g
        bonus_s[rows, :] = bonus
        for h in range(RWKV_HEADS):
            hs = slice(h * RWKV_HEAD, (h + 1) * RWKV_HEAD)
            j = q * RWKV_HEADS + h
            w3[j] = decay[:, hs]
            kk3[j] = kk[:, hs]
            kka3[j] = kka[:, hs]
            k3[j] = k[:, hs]
            r3[j] = r[:, hs]
            v3[j] = v[:, hs]

    eye = (lax.broadcasted_iota(jnp.int32, (RWKV_HEAD, RWKV_HEAD), 0)
           == lax.broadcasted_iota(jnp.int32, (RWKV_HEAD, RWKV_HEAD), 1)).astype(f32)

    def step(t, carry):
        ts = pl.ds(t, 1)
        chains = [(q, h, q * RWKV_HEADS + h) for q in range(reqs) for h in range(RWKV_HEADS)]
        s_olds = [s_ref[q, h] for q, h, _ in chains]
        sas = [jnp.sum(s * kk3[j, ts, :], axis=1, keepdims=True) for s, (_, _, j) in zip(s_olds, chains)]
        v_cols = [jnp.sum(eye * v3[j, ts, :], axis=1, keepdims=True) for _, _, j in chains]
        s_news = [s * w3[j, ts, :] - sa * kka3[j, ts, :] + vc * k3[j, ts, :]
                  for s, sa, vc, (_, _, j) in zip(s_olds, sas, v_cols, chains)]
        for s_new, (q, h, _) in zip(s_news, chains):
            s_ref[q, h] = s_new
        y_cols = [jnp.sum(s * r3[j, ts, :], axis=1, keepdims=True) for s, (_, _, j) in zip(s_news, chains)]
        for y_col, (_, _, j) in zip(y_cols, chains):
            y3[j, ts, :] = jnp.sum(eye * y_col, axis=0, keepdims=True)
        return carry

    lax.fori_loop(0, steps, step, 0)

    for q in range(reqs):
        rows = slice(q * tb, (q + 1) * tb)
        y = jnp.concatenate([y3[q * RWKV_HEADS + h] for h in range(RWKV_HEADS)], axis=-1)
        o_ref[rows, :] = _rwkv_output(y, bonus_s[rows, :], g_s[rows, :], lg_ref, lb_ref, ones_ref)


def _unit_lower_inverse_minus_identity(ns):
    size = ns[0].shape[0]
    t = lax.broadcasted_iota(jnp.int32, (size, size), 0)
    s = lax.broadcasted_iota(jnp.int32, (size, size), 1)
    first = ((t >> 1) == (s >> 1)) & (t > s)
    es = [-jnp.where(first, n, 0.0) for n in ns]
    blk, shift = 4, 2
    while blk <= RWKV_HEAD:
        half = blk // 2
        sel = ((t >> shift) == (s >> shift)) & ((t & (blk - 1)) >= half) & ((s & (blk - 1)) < half)
        cs = [jnp.where(sel, n, 0.0) for n in ns]
        zs = [c + _dot(c.astype(bf16), e.astype(bf16)) for c, e in zip(cs, es)]
        es = [e - z - _dot(e.astype(bf16), z.astype(bf16)) for e, z in zip(es, zs)]
        blk, shift = blk * 2, shift + 1
    return es


def _rwkv_chunk_kernel(*refs, tb):
    x_refs, p_refs, w_refs = refs[0:4], refs[4:8], refs[8:18]
    lg_ref, lb_ref, ones_ref, s0_ref, o_ref, s_ref = refs[18:24]
    carries = refs[24:28]
    sp = refs[28]
    i = pl.program_id(1)
    n_pairs = RWKV_HEADS // 2
    hd = RWKV_HEAD

    @pl.when(i == 0)
    def _():
        zero = jnp.zeros((hd, hd), f32)
        for p in range(n_pairs):
            top = jnp.concatenate([s0_ref[0, 2 * p], zero], axis=1)
            bot = jnp.concatenate([zero, s0_ref[0, 2 * p + 1]], axis=1)
            sp[p] = jnp.concatenate([top, bot], axis=0)
        for carry, p_ref in zip(carries, p_refs):
            carry[...] = p_ref[0]

    xs = [x[...] for x in x_refs]
    prevs = [carry[...] for carry in carries]
    for carry, x in zip(carries, xs):
        carry[...] = x[tb - 1:tb, :]
    r, k, v, kk, a, lw, g, bonus = _rwkv_features(xs, prevs, w_refs, ones_ref)
    beta = kk * a
    row = lax.broadcasted_iota(jnp.int32, lw.shape, 0)
    cum = lw
    d = 1
    while d < tb:
        cum = cum + jnp.where(row >= d, pltpu.roll(cum, d, 0), 0.0)
        d *= 2
    c_last = cum[tb - 1:tb, :]
    a_t = kk * jnp.exp(cum - lw)
    r_t = r * jnp.exp(cum)
    e_neg = jnp.exp(-cum)
    k_t = k * e_neg
    b_t = beta * e_neg
    e_hat = jnp.exp(c_last - cum)
    k_h = k * e_hat
    b_h = beta * e_hat
    gamma = jnp.exp(c_last)

    lane = lax.broadcasted_iota(jnp.int32, (tb, LANES), 1)
    head0 = lane < hd
    split = lambda x: (jnp.where(head0, x, 0.0), jnp.where(head0, 0.0, x))
    tt = lax.broadcasted_iota(jnp.int32, (LANES, LANES), 0)
    ss = lax.broadcasted_iota(jnp.int32, (LANES, LANES), 1)
    strict = tt > ss
    incl = tt >= ss
    same_head = (tt >= hd) == (ss >= hd)
    fold = lambda m: m[:tb, :] + m[tb:, :]

    pairs = range(n_pairs)
    cols = [slice(p * LANES, (p + 1) * LANES) for p in pairs]
    stack = lambda *xs: jnp.concatenate(xs, axis=0)
    scs = [_dot_nt(stack(*split(a_t[:, ps]), *split(r_t[:, ps])).astype(bf16),
                   stack(*split(k_t[:, ps]), *split(b_t[:, ps])).astype(bf16)) for ps in cols]
    es = _unit_lower_inverse_minus_identity([jnp.where(strict, sc[:2 * tb, 2 * tb:], 0.0) for sc in scs])
    eye = jnp.where(tt == ss, 1.0, 0.0)
    t_cats = [fold(eye + e).astype(bf16) for e in es]
    ak_cats = [fold(jnp.where(strict, sc[:2 * tb, :2 * tb], 0.0)).astype(bf16) for sc in scs]
    r_cats = [jnp.concatenate([fold(jnp.where(incl, sc[2 * tb:, :2 * tb], 0.0)),
                               -fold(jnp.where(incl, sc[2 * tb:, 2 * tb:], 0.0))], axis=1).astype(bf16)
              for sc in scs]
    s_olds = [sp[p] for p in pairs]
    grs = [_dot_nt(stack(a_t[:, ps], r_t[:, ps]).astype(bf16), s_old.astype(bf16))
           for ps, s_old in zip(cols, s_olds)]
    v_sts = [stack(*split(v[:, ps])).astype(bf16) for ps in cols]
    u_rhss = [gr[:tb] + _dot(ak, v_st) for gr, ak, v_st in zip(grs, ak_cats, v_sts)]
    us = [_dot(t_cat, stack(*split(u_rhs)).astype(bf16)) for t_cat, u_rhs in zip(t_cats, u_rhss)]
    ys = [gr[tb:] + _dot(r_cat, stack(v_st, stack(*split(u)).astype(bf16)))
          for gr, r_cat, v_st, u in zip(grs, r_cats, v_sts, us)]
    for p, ps, s_old, u in zip(pairs, cols, s_olds, us):
        vu_t = stack(v[:, ps], -u).T.astype(bf16)
        kb = stack(k_h[:, ps], b_h[:, ps]).astype(bf16)
        sp[p] = s_old * gamma[:, ps] + jnp.where(same_head, _dot(vu_t, kb), 0.0)

    o_ref[...] = _rwkv_output(jnp.concatenate(ys, axis=-1), bonus, g, lg_ref, lb_ref, ones_ref)

    @pl.when(i == pl.num_programs(1) - 1)
    def _():
        for p in range(n_pairs):
            full = sp[p]
            s_ref[0, 2 * p] = full[:hd, :hd]
            s_ref[0, 2 * p + 1] = full[hd:, hd:]


def rwkv(u, prev, p, s0, layer, ones2, batch, l_pad, l_real, chunked, reqs=1):
    m = u.shape[0]
    bw = BRANCH_WIDTH
    full = lambda w: pl.BlockSpec((1, w), lambda b, i: (0, 0))
    if chunked:
        tb, nb, nblk = RWKV_HEAD, 1, l_pad // RWKV_HEAD
        assert l_real == l_pad and l_pad % tb == 0
        body = functools.partial(_rwkv_chunk_kernel, tb=tb)
        scratch = [pltpu.VMEM((1, bw), f32), pltpu.VMEM((1, bw), f32), pltpu.VMEM((1, bw), f32),
                   pltpu.VMEM((1, RWKV_LORA), f32), pltpu.VMEM((RWKV_HEADS // 2, LANES, LANES), f32)]
    else:
        tb, nb, nblk = l_pad, reqs, 1
        assert batch % reqs == 0
        body = functools.partial(_rwkv_seq_kernel, tb=tb, steps=l_real, reqs=reqs)
        head_buf = pltpu.VMEM((reqs * RWKV_HEADS, tb, RWKV_HEAD), f32)
        scratch = [head_buf] * 7 + [pltpu.VMEM((reqs * tb, bw), f32), pltpu.VMEM((reqs * tb, bw), f32)]
    rows = nb * tb
    row = lambda b, i: b * nblk + i
    prev_spec = lambda w: pl.BlockSpec((None, nb, 1, w), lambda b, i: (layer, b, 0, 0))
    return pl.pallas_call(
        body,
        grid=(batch // nb, nblk),
        in_specs=[pl.BlockSpec((rows, bw), lambda b, i: (row(b, i), C_R // bw)),
                  pl.BlockSpec((rows, bw), lambda b, i: (row(b, i), C_K // bw)),
                  pl.BlockSpec((rows, bw), lambda b, i: (row(b, i), C_V // bw)),
                  pl.BlockSpec((rows, RWKV_LORA), lambda b, i: (row(b, i), C_LORA // RWKV_LORA)),
                  prev_spec(bw), prev_spec(bw), prev_spec(bw), prev_spec(RWKV_LORA),
                  full(bw), full(bw), full(bw), full(RWKV_LORA),
                  full(bw), full(bw),
                  pl.BlockSpec((RWKV_LORA, 3 * bw), lambda b, i: (0, 0)),
                  full(bw), full(bw), full(bw), full(bw), full(bw),
                  pl.BlockSpec((LANES, LANES), lambda b, i: (0, 0)),
                  pl.BlockSpec((None, nb, RWKV_HEADS, RWKV_HEAD, RWKV_HEAD), lambda b, i: (layer, b, 0, 0, 0))],
        out_specs=[pl.BlockSpec((rows, bw), lambda b, i: (row(b, i), 0)),
                   pl.BlockSpec((nb, RWKV_HEADS, RWKV_HEAD, RWKV_HEAD), lambda b, i: (b, 0, 0, 0))],
        out_shape=[jax.ShapeDtypeStruct((m, bw), bf16),
                   jax.ShapeDtypeStruct((batch, RWKV_HEADS, RWKV_HEAD, RWKV_HEAD), f32)],
        scratch_shapes=scratch,
        compiler_params=_params(("parallel", "arbitrary")),
        name="rwkv7_chunked" if chunked else "rwkv7_seq",
    )(u, u, u, u, *prev, p['mu_r'], p['mu_k'], p['mu_v'], p['mu_lo'], p['w0'], p['a0'], p['w_lora'],
      p['k_k'], p['k_a'], p['r_k'], p['ln_g'], p['ln_b'], ones2, s0)


IN_TILE = 1024
IN_TAIL_TILE = C_LORA // IN_TILE


def _in_proj_weights(w_in):
    wt = jnp.swapaxes(w_in, 1, 2).astype(bf16)
    seg = lambda o, n: wt[:, o:o + n]
    tail = jnp.concatenate([seg(_O_RU + 3072, RWKV_LORA), seg(_O_SK, 128), seg(_O_SV, 128),
                            seg(_O_GA, GLA_GATE_RANK),
                            jnp.zeros((wt.shape[0], N_PACK - C_GA - GLA_GATE_RANK, wt.shape[2]), bf16)], axis=1)
    return wt, tail


def _in_tile_source(j):
    return jnp.where(j < 2, _O_GV + IN_TILE * j,
           jnp.where(j == 2, _O_SQ,
           jnp.where(j == 3, _O_MQ,
           jnp.where(j < 7, _O_RU + IN_TILE * (j - 4),
           jnp.where(j < 15, _O_GPRE + IN_TILE * (j - 7), _O_GQ)))))


def _in_proj_kernel(x_ref, w_ref, tail_ref, o_ref):
    j = pl.program_id(1)

    @pl.when(j != IN_TAIL_TILE)
    def _():
        o_ref[...] = _dot_nt(x_ref[...], w_ref[0])

    @pl.when(j == IN_TAIL_TILE)
    def _():
        o_ref[...] = _dot_nt(x_ref[...], tail_ref[...])


def in_projection(x, wt, tail, layer, tm):
    m, k = x.shape
    return pl.pallas_call(
        _in_proj_kernel,
        grid=(m // tm, N_PACK // IN_TILE),
        in_specs=[pl.BlockSpec((tm, k), lambda i, j: (i, 0)),
                  pl.BlockSpec((pl.Element(1), pl.Element(IN_TILE), pl.Element(k)),
                               lambda i, j: (layer, pl.multiple_of(_in_tile_source(j), 16), 0)),
                  pl.BlockSpec((None, IN_TILE, k), lambda i, j: (layer, 0, 0))],
        out_specs=pl.BlockSpec((tm, IN_TILE), lambda i, j: (i, j)),
        out_shape=jax.ShapeDtypeStruct((m, N_PACK), f32),
        compiler_params=_params(("parallel", "parallel")),
        name="in_proj",
    )(x, wt, tail)


def _layer_params(l, w):
    row = lambda x: x[l].reshape(1, -1)
    mu = w['rwkv_mu'][l]
    z = lambda r, c: jnp.zeros((r, c), f32)
    w_lora = jnp.concatenate([
        jnp.concatenate([w['rwkv_w2'][l], z(64, 1024), z(64, 1024)], axis=1),
        jnp.concatenate([z(64, 1024), w['rwkv_a2'][l], z(64, 1024)], axis=1),
        jnp.concatenate([z(128, 1024), z(128, 1024), w['rwkv_g2'][l]], axis=1)], axis=0).astype(bf16)
    a_up = jnp.concatenate([w['gla_a_up'][l], z(LANES - GLA_GATE_RANK, GLA_QK)], axis=0).astype(bf16)
    return {
        'a_up': a_up, 'a_b': row(w['gla_a_b']),
        'gla_g': row(w['gla_norm_g']), 'gla_b': row(w['gla_norm_b']),
        'sinks': w['swa_sinks'][l],
        'rwkv': {'mu_r': mu[:1024].reshape(1, -1), 'mu_k': mu[1024:2048].reshape(1, -1),
                 'mu_v': mu[2048:3072].reshape(1, -1), 'mu_lo': mu[3072:].reshape(1, -1),
                 'w0': row(w['rwkv_w0']), 'a0': row(w['rwkv_a0']), 'w_lora': w_lora,
                 'k_k': row(w['rwkv_k_k']), 'k_a': row(w['rwkv_k_a']), 'r_k': row(w['rwkv_r_k']),
                 'ln_g': row(w['rwkv_ln_g']), 'ln_b': row(w['rwkv_ln_b'])},
        'ln1_g': row(w['ln1_g']), 'ln1_b': row(w['ln1_b']),
        'ln2_g': row(w['ln2_g']), 'ln2_b': row(w['ln2_b']),
    }


def _tiles(m):
    if m >= 1024:
        return {'proj': 2048, 'merge': 512, 'ffn': 2048, 'ln_out': 512, 'ln_down': 256}
    return {'proj': m, 'merge': m, 'ffn': m, 'ln_out': m, 'ln_down': m}


def _trunk_layer(h_f, h_b, l, p, big, ones2, mem_kv, mem_cache, gla_s0, rwkv_s0, rwkv_prev, swa_buf,
                 batch, l_pad, l_real):
    m = h_f.shape[0]
    tl = _tiles(m)
    u = in_projection(h_b, big['w_in_t'], big['w_in_tail'], l, tl['proj'])
    o_a, gla_s = gla(u, p['a_up'], p['a_b'], p['gla_g'], p['gla_b'], gla_s0[0], gla_s0[1], batch, l_pad, l_real,
                     min(GLA_CHUNK, l_pad))
    if swa_buf is None:
        o_b = swa_prompt(u, p['sinks'], batch, l_pad)
    else:
        o_b = swa_sample(u, swa_buf[0], swa_buf[1], l, p['sinks'], batch, l_pad, l_real, 4)
    chunked = l_real == l_pad and l_pad % RWKV_HEAD == 0
    o_c, rwkv_s = rwkv(u, rwkv_prev[0], p['rwkv'], rwkv_s0[0], rwkv_s0[1], ones2, batch, l_pad, l_real,
                       chunked, reqs=1 if chunked else 4)
    o_m = mem_attention(u, mem_kv, mem_cache, l, batch, l_pad, min(512, l_pad))
    merged = gated_merge((o_a, o_b, o_c, o_m), u, big['gate_b'], big['w_branch'], l, tl['merge'], 512)
    x_f, x_b = matmul_residual_ln(merged, big['w_out'], l, h_f, p['ln1_g'], p['ln1_b'], tl['ln_out'])
    act = ffn_up(x_b, big['w_gu'], l, tl['ffn'], 512)
    y_f, y_b = matmul_residual_ln(act, big['w_down'], l, x_f, p['ln2_g'], p['ln2_b'], tl['ln_down'])
    return y_f, y_b, u, gla_s, rwkv_s


def _split_ru(x):
    return (x[..., :1024], x[..., 1024:2048], x[..., 2048:3072], x[..., 3072:])


def kernel(x_prompt, x_sample, mem_prompt, cache_swa_k, cache_swa_v, cache_mem_k, cache_mem_v, state_gla, state_rwkv, state_rwkv_shift, w_in, gate_b, gla_a_up, gla_a_b, gla_norm_g, gla_norm_b, swa_sinks, rwkv_mu, rwkv_w0, rwkv_w2, rwkv_a0, rwkv_a2, rwkv_g2, rwkv_k_k, rwkv_k_a, rwkv_r_k, rwkv_ln_g, rwkv_ln_b, w_mem_kv, w_branch, w_out, ln1_g, ln1_b, w_gu, w_down, ln2_g, ln2_b):
    weights = {'gla_a_up': gla_a_up, 'gla_a_b': gla_a_b,
               'gla_norm_g': gla_norm_g, 'gla_norm_b': gla_norm_b, 'swa_sinks': swa_sinks,
               'rwkv_mu': rwkv_mu, 'rwkv_w0': rwkv_w0, 'rwkv_w2': rwkv_w2, 'rwkv_a0': rwkv_a0,
               'rwkv_a2': rwkv_a2, 'rwkv_g2': rwkv_g2, 'rwkv_k_k': rwkv_k_k, 'rwkv_k_a': rwkv_k_a,
               'rwkv_r_k': rwkv_r_k, 'rwkv_ln_g': rwkv_ln_g, 'rwkv_ln_b': rwkv_ln_b,
               'ln1_g': ln1_g, 'ln1_b': ln1_b, 'ln2_g': ln2_g, 'ln2_b': ln2_b}
    w_in_t, w_in_tail = _in_proj_weights(w_in)
    big = {'w_in_t': w_in_t, 'w_in_tail': w_in_tail, 'gate_b': gate_b, 'w_branch': w_branch, 'w_gu': w_gu,
           'w_out': w_out.astype(bf16), 'w_down': w_down.astype(bf16)}
    bp, lp, _ = x_prompt.shape
    bs, ls, _ = x_sample.shape
    ls_pad = -(-ls // SUBLANES) * SUBLANES
    mp, ms = bp * lp, bs * ls_pad

    hp_f = x_prompt.reshape(mp, D_MODEL)
    hs_f = jnp.pad(x_sample, ((0, 0), (0, ls_pad - ls), (0, 0))).reshape(ms, D_MODEL)
    hp_b, hs_b = hp_f.astype(bf16), hs_f.astype(bf16)
    mem_b = mem_prompt.reshape(bp * MEM_TOKENS, D_MODEL).astype(bf16)
    half = LANES // 2
    blk = jnp.ones((half, half), f32)
    zero = jnp.zeros((half, half), f32)
    ones2 = jnp.concatenate([jnp.concatenate([blk, zero], 1), jnp.concatenate([zero, blk], 1)], 0).astype(bf16)

    gla0_p = jnp.zeros((1, bp, GLA_HEADS, GLA_DK, GLA_DV), f32)
    rwkv0_p = jnp.zeros((1, bp, RWKV_HEADS, RWKV_HEAD, RWKV_HEAD), f32)
    prev0_p = _split_ru(jnp.zeros((1, bp, 1, RWKV_COLS), f32))
    prev_s = _split_ru(state_rwkv_shift)
    kvw = SWA_KV_HEADS * SWA_HEAD_DIM
    kbuf = cache_swa_k.reshape(DEPTH, bs * WINDOW, kvw)
    vbuf = cache_swa_v.reshape(DEPTH, bs * WINDOW, kvw)
    mem_k2d = cache_mem_k.reshape(DEPTH * bs * MEM_TOKENS, MEM_HEADS * MEM_HEAD_DIM)
    mem_v2d = cache_mem_v.reshape(DEPTH * bs * MEM_TOKENS, MEM_HEADS * MEM_HEAD_DIM)

    outs = {k: [] for k in ('p_swk', 'p_swv', 'p_mk', 'p_mv', 'p_gla', 'p_rw', 'p_rs',
                            's_swk', 's_swv', 's_gla', 's_rw', 's_rs')}
    for l in range(DEPTH):
        p = _layer_params(l, weights)
        kv = matmul(mem_b, w_mem_kv, l, bp * MEM_TOKENS, 512, "mem_kv")
        hp_f, hp_b, u, gs, rs = _trunk_layer(hp_f, hp_b, l, p, big, ones2, kv, None, (gla0_p, 0), (rwkv0_p, 0),
                                             (prev0_p, 0), None, bp, lp, lp)
        u3 = u.reshape(bp, lp, N_PACK)
        outs['p_swk'].append(u3[:, lp - WINDOW:, C_SK:C_SK + kvw].reshape(bp, WINDOW, SWA_KV_HEADS, SWA_HEAD_DIM))
        outs['p_swv'].append(u3[:, lp - WINDOW:, C_SV:C_SV + kvw].reshape(bp, WINDOW, SWA_KV_HEADS, SWA_HEAD_DIM))
        outs['p_mk'].append(kv[:, :1024].reshape(bp, MEM_TOKENS, MEM_HEADS, MEM_HEAD_DIM))
        outs['p_mv'].append(kv[:, 1024:].reshape(bp, MEM_TOKENS, MEM_HEADS, MEM_HEAD_DIM))
        outs['p_gla'].append(gs)
        outs['p_rw'].append(rs)
        outs['p_rs'].append(jnp.concatenate([u3[:, lp - 1:, C_R:C_R + 3072],
                                             u3[:, lp - 1:, C_LORA:C_LORA + RWKV_LORA]], axis=-1))
        hs_f, hs_b, u, gs, rs = _trunk_layer(hs_f, hs_b, l, p, big, ones2, None, (mem_k2d, mem_v2d),
                                             (state_gla, l), (state_rwkv, l), (prev_s, l), (kbuf, vbuf),
                                             bs, ls_pad, ls)
        u3 = u.reshape(bs, ls_pad, N_PACK)
        k_new = u3[:, :ls, C_SK:C_SK + kvw].reshape(bs, ls, SWA_KV_HEADS, SWA_HEAD_DIM)
        v_new = u3[:, :ls, C_SV:C_SV + kvw].reshape(bs, ls, SWA_KV_HEADS, SWA_HEAD_DIM)
        outs['s_swk'].append(jnp.concatenate([cache_swa_k[l][:, ls:], k_new], axis=1))
        outs['s_swv'].append(jnp.concatenate([cache_swa_v[l][:, ls:], v_new], axis=1))
        outs['s_gla'].append(gs)
        outs['s_rw'].append(rs)
        outs['s_rs'].append(jnp.concatenate([u3[:, ls - 1:ls, C_R:C_R + 3072],
                                             u3[:, ls - 1:ls, C_LORA:C_LORA + RWKV_LORA]], axis=-1))

    st = {k: jnp.stack(v) for k, v in outs.items()}
    y_prompt = hp_f.reshape(bp, lp, D_MODEL)
    y_sample = hs_f.reshape(bs, ls_pad, D_MODEL)[:, :ls]
    return (y_prompt, y_sample,
            st['p_swk'], st['p_swv'], st['p_mk'], st['p_mv'], st['p_gla'], st['p_rw'], st['p_rs'],
            st['s_swk'], st['s_swv'], st['s_gla'], st['s_rw'], st['s_rs'])
```

```python
import functools

import jax
import jax.numpy as jnp
from jax import lax
from jax.experimental import pallas as pl
from jax.experimental.pallas import tpu as pltpu

f32 = jnp.float32
bf16 = jnp.bfloat16

D_MODEL = 2048
DEPTH = 4
BRANCH_WIDTH = 1024
N_BRANCH = 4
GLA_HEADS = 4
GLA_QK = 512
GLA_V = 1024
GLA_DK = 128
GLA_DV = 256
GLA_GATE_RANK = 16
GLA_TAU = 16.0
GLA_CHUNK = 64
SWA_HEAD_DIM = 64
SWA_Q_HEADS = 16
SWA_KV_HEADS = 2
SWA_GROUP = SWA_Q_HEADS // SWA_KV_HEADS
WINDOW = 128
RWKV_HEAD = 64
RWKV_HEADS = 16
RWKV_LORA = 256
RWKV_COLS = 3 * BRANCH_WIDTH + RWKV_LORA
MEM_TOKENS = 256
MEM_HEADS = 4
MEM_HEAD_DIM = 256
D_FF = 5632
DEEPNORM_ALPHA = (2 * DEPTH) ** 0.25
NEG = -1e30

_O_GQ, _O_GK, _O_GV, _O_GR, _O_GA = 0, 512, 1024, 2048, 3072
_O_SQ, _O_SK, _O_SV = 3088, 4112, 4240
_O_RU = 4368
_O_MQ = 7696
_O_GPRE = 8720
_N_IN = 16912

C_GV, C_GR, C_SQ, C_MQ, C_R, C_K, C_V = 0, 1024, 2048, 3072, 4096, 5120, 6144
C_GPRE = 7168
C_GQ, C_GK = 15360, 15872
C_LORA = 16384
C_SK, C_SV, C_GA = 16640, 16768, 16896
N_PACK = 17408

LANES = 128
SUBLANES = 8
VMEM_BYTES_V7X = 64 * 1024 * 1024
VMEM_LIMIT = VMEM_BYTES_V7X - 8 * 1024 * 1024


def _params(sem):
    return pltpu.CompilerParams(dimension_semantics=sem, vmem_limit_bytes=VMEM_LIMIT)


def _softplus(z):
    return jnp.maximum(z, 0.0) + jnp.log(1.0 + jnp.exp(-jnp.abs(z)))


def _sigmoid(z):
    return 0.5 * jnp.tanh(0.5 * z) + 0.5


def _dot(a, b):
    return jnp.dot(a, b, preferred_element_type=f32)


def _dot_nt(a, b):
    return lax.dot_general(a, b, (((1,), (1,)), ((), ())), preferred_element_type=f32)


def _mm_kernel(x_ref, w_ref, o_ref, *, trans_w):
    mm = _dot_nt if trans_w else _dot
    o_ref[...] = mm(x_ref[...], w_ref[...].astype(bf16))


def matmul(x, w, layer, tm, tn, name, trans_w=False):
    m, k = x.shape
    n = w.shape[1] if trans_w else w.shape[2]
    w_spec = (pl.BlockSpec((None, tn, k), lambda i, j: (layer, j, 0)) if trans_w
              else pl.BlockSpec((None, k, tn), lambda i, j: (layer, 0, j)))
    return pl.pallas_call(
        functools.partial(_mm_kernel, trans_w=trans_w),
        grid=(m // tm, n // tn),
        in_specs=[pl.BlockSpec((tm, k), lambda i, j: (i, 0)), w_spec],
        out_specs=pl.BlockSpec((tm, tn), lambda i, j: (i, j)),
        out_shape=jax.ShapeDtypeStruct((m, n), f32),
        compiler_params=_params(("parallel", "parallel")),
        name=name,
    )(x, w)


def _mm_ln_kernel(x_ref, w_ref, res_ref, g_ref, b_ref, of_ref, ob_ref):
    z = DEEPNORM_ALPHA * res_ref[...] + _dot(x_ref[...], w_ref[...])
    mu = jnp.mean(z, axis=-1, keepdims=True)
    d = z - mu
    var = jnp.mean(d * d, axis=-1, keepdims=True)
    y = d * lax.rsqrt(var + 1e-5) * g_ref[...] + b_ref[...]
    of_ref[...] = y
    ob_ref[...] = y.astype(bf16)


def matmul_residual_ln(x, w, layer, res, g, b, tm):
    m, k = x.shape
    n = w.shape[2]
    return pl.pallas_call(
        _mm_ln_kernel,
        grid=(m // tm,),
        in_specs=[pl.BlockSpec((tm, k), lambda i: (i, 0)),
                  pl.BlockSpec((None, k, n), lambda i: (layer, 0, 0), pipeline_mode=pl.Buffered(1)),
                  pl.BlockSpec((tm, n), lambda i: (i, 0)),
                  pl.BlockSpec((1, n), lambda i: (0, 0)),
                  pl.BlockSpec((1, n), lambda i: (0, 0))],
        out_specs=[pl.BlockSpec((tm, n), lambda i: (i, 0)),
                   pl.BlockSpec((tm, n), lambda i: (i, 0))],
        out_shape=[jax.ShapeDtypeStruct((m, n), f32), jax.ShapeDtypeStruct((m, n), bf16)],
        compiler_params=_params(("parallel",)),
        name="proj_ln",
    )(x, w, res, g, b)


def _merge_kernel(a_ref, b_ref, c_ref, m_ref, g0_ref, g1_ref, g2_ref, g3_ref, gb_ref, w_ref, o_ref, w_bf):
    @pl.when(pl.program_id(1) == 0)
    def _():
        w_bf[...] = w_ref[...].astype(bf16)

    acc = None
    for n, (br, gp) in enumerate(((a_ref, g0_ref), (b_ref, g1_ref), (c_ref, g2_ref), (m_ref, g3_ref))):
        y = _dot(br[...], w_bf[n])
        gate = _sigmoid(gp[...] + gb_ref[n:n + 1, :])
        acc = gate * y if acc is None else acc + gate * y
    o_ref[...] = acc.astype(bf16)


def gated_merge(branches, u, gate_b, w_branch, layer, tm, tn):
    m = u.shape[0]
    gp0 = C_GPRE // tn
    per = D_MODEL // tn
    br_spec = pl.BlockSpec((tm, BRANCH_WIDTH), lambda j, i: (i, 0))
    gp_specs = [pl.BlockSpec((tm, tn), functools.partial(lambda j, i, n: (i, gp0 + n * per + j), n=n))
                for n in range(N_BRANCH)]
    return pl.pallas_call(
        _merge_kernel,
        grid=(D_MODEL // tn, m // tm),
        in_specs=[br_spec] * 4 + gp_specs + [
            pl.BlockSpec((None, N_BRANCH, tn), lambda j, i: (layer, 0, j)),
            pl.BlockSpec((None, N_BRANCH, BRANCH_WIDTH, tn), lambda j, i: (layer, 0, 0, j))],
        out_specs=pl.BlockSpec((tm, tn), lambda j, i: (i, j)),
        out_shape=jax.ShapeDtypeStruct((m, D_MODEL), bf16),
        scratch_shapes=[pltpu.VMEM((N_BRANCH, BRANCH_WIDTH, tn), bf16)],
        compiler_params=_params(("parallel", "arbitrary")),
        name="gated_merge",
    )(*branches, u, u, u, u, gate_b, w_branch)


def _ffn_up_kernel(x_ref, wg_ref, wu_ref, o_ref, wg_bf, wu_bf):
    @pl.when(pl.program_id(1) == 0)
    def _():
        wg_bf[...] = wg_ref[...].astype(bf16)
        wu_bf[...] = wu_ref[...].astype(bf16)

    x = x_ref[...]
    g = _dot(x, wg_bf[...])
    up = _dot(x, wu_bf[...])
    o_ref[...] = (g * _sigmoid(g) * up).astype(bf16)


def ffn_up(x, w_gu, layer, tm, tn):
    m, k = x.shape
    nj = D_FF // tn
    return pl.pallas_call(
        _ffn_up_kernel,
        grid=(nj, m // tm),
        in_specs=[pl.BlockSpec((tm, k), lambda j, i: (i, 0)),
                  pl.BlockSpec((None, k, tn), lambda j, i: (layer, 0, j)),
                  pl.BlockSpec((None, k, tn), lambda j, i: (layer, 0, nj + j))],
        out_specs=pl.BlockSpec((tm, tn), lambda j, i: (i, j)),
        out_shape=jax.ShapeDtypeStruct((m, D_FF), bf16),
        scratch_shapes=[pltpu.VMEM((k, tn), bf16), pltpu.VMEM((k, tn), bf16)],
        compiler_params=_params(("parallel", "arbitrary")),
        name="ffn_up",
    )(x, w_gu, w_gu)


def _gla_kernel(q_ref, k_ref, v_ref, r_ref, a_ref, aup_ref, ab_ref, ng_ref, nb_ref, s0_ref,
                o_ref, s_ref, *, chunk, sub, l_real, l_pad):
    @pl.when(pl.program_id(1) == 0)
    def _():
        s_ref[...] = s0_ref[...]

    for ci in range(sub):
        rows = slice(ci * chunk, (ci + 1) * chunk)
        refs = [x.at[rows, :] for x in (q_ref, k_ref, v_ref, r_ref, a_ref, o_ref)]
        _gla_chunk(*refs[:5], aup_ref, ab_ref, ng_ref, nb_ref, refs[5], s_ref,
                   (pl.program_id(1) * sub + ci) * chunk, chunk, l_real, l_pad)


def _gla_chunk(q_ref, k_ref, v_ref, r_ref, a_ref, aup_ref, ab_ref, ng_ref, nb_ref, o_ref, s_ref,
               t0, chunk, l_real, l_pad):
    la = _dot(a_ref[...].astype(bf16), aup_ref[...]) + ab_ref[...]
    la = (jnp.minimum(la, 0.0) - jnp.log(1.0 + jnp.exp(-jnp.abs(la)))) * (1.0 / GLA_TAU)
    row = lax.broadcasted_iota(jnp.int32, (chunk, GLA_QK), 0)
    kin = k_ref[...]
    if l_real < l_pad:
        real = (t0 + row) < l_real
        la = jnp.where(real, la, 0.0)
        kin = jnp.where(real, kin, 0.0)
    b = la
    d = 1
    while d < chunk:
        b = b + jnp.where(row >= d, pltpu.roll(b, d, 0), 0.0)
        d *= 2
    b_last = b[chunk - 1:chunk, :]
    q_dec = q_ref[...] * (GLA_DK ** -0.5) * jnp.exp(b)
    k_inv = kin * jnp.exp(-b)
    k_dec = kin * jnp.exp(b_last - b)
    tt = lax.broadcasted_iota(jnp.int32, (chunk, chunk), 0)
    ss = lax.broadcasted_iota(jnp.int32, (chunk, chunk), 1)
    causal = tt >= ss
    heads = range(GLA_HEADS)
    kss = [slice(h * GLA_DK, (h + 1) * GLA_DK) for h in heads]
    vss = [slice(h * GLA_DV, (h + 1) * GLA_DV) for h in heads]
    qhs = [q_dec[:, ks].astype(bf16) for ks in kss]
    atts = [jnp.where(causal, _dot_nt(qh, k_inv[:, ks].astype(bf16)), 0.0).astype(bf16) for qh, ks in zip(qhs, kss)]
    vhs = [v_ref[:, vs].astype(bf16) for vs in vss]
    s_olds = [s_ref[0, h] for h in heads]
    os_ = [_dot(att, vh) + _dot(qh, s_old.astype(bf16)) for att, vh, qh, s_old in zip(atts, vhs, qhs, s_olds)]
    decay_cols = [jnp.exp(jnp.sum(la[:, ks].T, axis=1, keepdims=True)) for ks in kss]
    for h, ks, vh, s_old, decay_col in zip(heads, kss, vhs, s_olds, decay_cols):
        s_ref[0, h] = decay_col * s_old + _dot(k_dec[:, ks].T.astype(bf16), vh)
    for vs, o in zip(vss, os_):
        mu = jnp.mean(o, axis=-1, keepdims=True)
        dd = o - mu
        var = jnp.mean(dd * dd, axis=-1, keepdims=True)
        nrm = dd * lax.rsqrt(var + 1e-5) * ng_ref[:, vs] + nb_ref[:, vs]
        gr = r_ref[:, vs]
        o_ref[:, vs] = (nrm * (gr * _sigmoid(gr))).astype(bf16)


def gla(u, a_up, a_b, norm_g, norm_b, s0, layer, batch, l_pad, l_real, chunk):
    m = u.shape[0]
    sub = next(s for s in (4, 2, 1) if l_pad % (s * chunk) == 0)
    rows = sub * chunk
    nc = l_pad // rows
    row = lambda b, c: b * nc + c
    return pl.pallas_call(
        functools.partial(_gla_kernel, chunk=chunk, sub=sub, l_real=l_real, l_pad=l_pad),
        grid=(batch, nc),
        in_specs=[pl.BlockSpec((rows, GLA_QK), lambda b, c: (row(b, c), C_GQ // GLA_QK)),
                  pl.BlockSpec((rows, GLA_QK), lambda b, c: (row(b, c), C_GK // GLA_QK)),
                  pl.BlockSpec((rows, GLA_V), lambda b, c: (row(b, c), C_GV // GLA_V)),
                  pl.BlockSpec((rows, GLA_V), lambda b, c: (row(b, c), C_GR // GLA_V)),
                  pl.BlockSpec((rows, LANES), lambda b, c: (row(b, c), C_GA // LANES)),
                  pl.BlockSpec((LANES, GLA_QK), lambda b, c: (0, 0)),
                  pl.BlockSpec((1, GLA_QK), lambda b, c: (0, 0)),
                  pl.BlockSpec((1, GLA_V), lambda b, c: (0, 0)),
                  pl.BlockSpec((1, GLA_V), lambda b, c: (0, 0)),
                  pl.BlockSpec((None, 1, GLA_HEADS, GLA_DK, GLA_DV), lambda b, c: (layer, b, 0, 0, 0))],
        out_specs=[pl.BlockSpec((rows, GLA_V), lambda b, c: (row(b, c), 0)),
                   pl.BlockSpec((1, GLA_HEADS, GLA_DK, GLA_DV), lambda b, c: (b, 0, 0, 0))],
        out_shape=[jax.ShapeDtypeStruct((m, GLA_V), bf16),
                   jax.ShapeDtypeStruct((batch, GLA_HEADS, GLA_DK, GLA_DV), f32)],
        compiler_params=_params(("parallel", "arbitrary")),
        name="gla",
    )(u, u, u, u, u, a_up, a_b, norm_g, norm_b, s0)


def _alibi_slopes():
    return 2.0 ** (-8.0 * jnp.arange(1, SWA_Q_HEADS + 1, dtype=f32) / SWA_Q_HEADS)


def _swa_bias(t_pos, key_pos, key_ok):
    dist = (t_pos[:, None] - key_pos[None, :]).astype(f32)
    valid = (dist >= 0) & (dist <= WINDOW) & key_ok[None, :]
    slopes = _alibi_slopes().reshape(SWA_KV_HEADS, SWA_GROUP, 1, 1)
    bias = jnp.where(valid[None, None], -slopes * dist[None, None], NEG)
    return bias.reshape(SWA_KV_HEADS, SWA_GROUP * t_pos.shape[0], key_pos.shape[0])


def _swa_groups(q_all, keys, vals, biases, sink_ref, rows):
    kvs = range(SWA_KV_HEADS)
    d = SWA_HEAD_DIM
    pairs = SWA_GROUP // 2
    heads = [range(kv * SWA_GROUP, (kv + 1) * SWA_GROUP) for kv in kvs]
    low = lax.broadcasted_iota(jnp.int32, (rows, LANES), 1) < d

    def both_halves(x, kv):
        swapped = pltpu.roll(x, d, 1)
        first = lax.broadcasted_iota(jnp.int32, x.shape, 1) < d
        return jnp.where(first, x, swapped) if kv == 0 else jnp.where(first, swapped, x)

    def stacked_queries(kv):
        parts = []
        for p in range(kv * pairs, (kv + 1) * pairs):
            q_pair = q_all[:, p * LANES:(p + 1) * LANES]
            parts += [jnp.where(low, q_pair, 0.0), jnp.where(low, 0.0, q_pair)]
        return jnp.concatenate(parts, axis=0).astype(bf16)

    qs = [stacked_queries(kv) for kv in kvs]
    k2s = [jnp.concatenate([both_halves(kb, kv) for kb in keys], axis=0).astype(bf16) for kv in kvs]
    v2s = [jnp.concatenate([both_halves(vb, kv) for vb in vals], axis=0).astype(bf16) for kv in kvs]
    ss = [_dot_nt(q, k2) * (d ** -0.5) + bias for q, k2, bias in zip(qs, k2s, biases)]
    sinks = [jnp.concatenate([jnp.full((rows, 1), sink_ref[h], f32) for h in hh], axis=0) for hh in heads]
    ms = [jnp.maximum(jnp.max(s, axis=-1, keepdims=True), sink) for s, sink in zip(ss, sinks)]
    ps = [jnp.exp(s - m) for s, m in zip(ss, ms)]
    dens = [jnp.sum(p, axis=-1, keepdims=True) + jnp.exp(sink - m) for p, sink, m in zip(ps, sinks, ms)]
    os_ = [_dot(p.astype(bf16), v2) / den for p, v2, den in zip(ps, v2s, dens)]
    return [jnp.where(low, o[(2 * p) * rows:(2 * p + 1) * rows, :], o[(2 * p + 1) * rows:(2 * p + 2) * rows, :])
            for o in os_ for p in range(pairs)]


def _swa_prompt_kernel(sink_ref, bias_ref, q_ref, kc_ref, kp_ref, vc_ref, vp_ref, o_ref):
    keys = (kp_ref[...], kc_ref[...])
    vals = (vp_ref[...], vc_ref[...])
    biases = [bias_ref[0, kv] for kv in range(SWA_KV_HEADS)]
    outs = _swa_groups(q_ref[...], keys, vals, biases, sink_ref, WINDOW)
    o_ref[...] = jnp.concatenate(outs, axis=-1).astype(bf16)


def swa_prompt(u, sinks, batch, seq):
    m = u.shape[0]
    nb = seq // WINDOW
    cur = lambda b, i: b * nb + i
    prev = lambda b, i: b * nb + jnp.maximum(i - 1, 0)
    kcol, vcol = C_SK // LANES, C_SV // LANES
    t_pos = jnp.arange(WINDOW)
    key_pos = jnp.arange(2 * WINDOW) - WINDOW
    bias = jnp.stack([_swa_bias(t_pos, key_pos, key_pos >= 0), _swa_bias(t_pos, key_pos, key_pos >= -WINDOW)])
    return pl.pallas_call(
        _swa_prompt_kernel,
        grid=(batch, nb),
        in_specs=[pl.BlockSpec(memory_space=pltpu.SMEM),
                  pl.BlockSpec((1,) + bias.shape[1:], lambda b, i: (jnp.minimum(i, 1), 0, 0, 0)),
                  pl.BlockSpec((WINDOW, BRANCH_WIDTH), lambda b, i: (cur(b, i), C_SQ // BRANCH_WIDTH)),
                  pl.BlockSpec((WINDOW, LANES), lambda b, i: (cur(b, i), kcol)),
                  pl.BlockSpec((WINDOW, LANES), lambda b, i: (prev(b, i), kcol)),
                  pl.BlockSpec((WINDOW, LANES), lambda b, i: (cur(b, i), vcol)),
                  pl.BlockSpec((WINDOW, LANES), lambda b, i: (prev(b, i), vcol))],
        out_specs=pl.BlockSpec((WINDOW, BRANCH_WIDTH), lambda b, i: (cur(b, i), 0)),
        out_shape=jax.ShapeDtypeStruct((m, BRANCH_WIDTH), bf16),
        compiler_params=_params(("parallel", "parallel")),
        name="swa_prompt",
    )(sinks, bias, u, u, u, u, u)


def _swa_sample_kernel(sink_ref, bias_ref, q_ref, kn_ref, vn_ref, kb_ref, vb_ref, o_ref, *, l_pad, reqs):
    for r in range(reqs):
        rs = slice(r * l_pad, (r + 1) * l_pad)
        bs = slice(r * WINDOW, (r + 1) * WINDOW)
        keys = (kb_ref[bs, :], kn_ref[rs, :])
        vals = (vb_ref[bs, :], vn_ref[rs, :])
        biases = [bias_ref[kv] for kv in range(SWA_KV_HEADS)]
        outs = _swa_groups(q_ref[rs, :], keys, vals, biases, sink_ref, l_pad)
        o_ref[rs, :] = jnp.concatenate(outs, axis=-1).astype(bf16)


def swa_sample(u, kbuf, vbuf, layer, sinks, batch, l_pad, l_real, reqs):
    m = u.shape[0]
    kcol, vcol = C_SK // LANES, C_SV // LANES
    key_pos = jnp.concatenate([jnp.arange(WINDOW) - WINDOW, jnp.arange(l_pad)])
    key_ok = jnp.concatenate([jnp.ones((WINDOW,), bool), jnp.arange(l_pad) < l_real])
    bias = _swa_bias(jnp.arange(l_pad), key_pos, key_ok)
    return pl.pallas_call(
        functools.partial(_swa_sample_kernel, l_pad=l_pad, reqs=reqs),
        grid=(batch // reqs,),
        in_specs=[pl.BlockSpec(memory_space=pltpu.SMEM),
                  pl.BlockSpec(bias.shape, lambda b: (0, 0, 0)),
                  pl.BlockSpec((reqs * l_pad, BRANCH_WIDTH), lambda b: (b, C_SQ // BRANCH_WIDTH)),
                  pl.BlockSpec((reqs * l_pad, LANES), lambda b: (b, kcol)),
                  pl.BlockSpec((reqs * l_pad, LANES), lambda b: (b, vcol)),
                  pl.BlockSpec((None, reqs * WINDOW, LANES), lambda b: (layer, b, 0)),
                  pl.BlockSpec((None, reqs * WINDOW, LANES), lambda b: (layer, b, 0))],
        out_specs=pl.BlockSpec((reqs * l_pad, BRANCH_WIDTH), lambda b: (b, 0)),
        out_shape=jax.ShapeDtypeStruct((m, BRANCH_WIDTH), bf16),
        compiler_params=_params(("parallel",)),
        name="swa_sample",
    )(sinks, bias, u, u, u, kbuf, vbuf)


def _mem_kernel(q_ref, k_ref, v_ref, o_ref):
    for h in range(MEM_HEADS):
        hs = slice(h * MEM_HEAD_DIM, (h + 1) * MEM_HEAD_DIM)
        s = _dot_nt(q_ref[:, hs].astype(bf16), k_ref[:, hs].astype(bf16)) * (MEM_HEAD_DIM ** -0.5)
        p = jnp.exp(s - jnp.max(s, axis=-1, keepdims=True))
        o = _dot(p.astype(bf16), v_ref[:, hs].astype(bf16)) / jnp.sum(p, axis=-1, keepdims=True)
        o_ref[:, hs] = o.astype(bf16)


def mem_attention(u, kv, cache, layer, batch, l_pad, tl):
    m = u.shape[0]
    nl = l_pad // tl
    width = MEM_HEADS * MEM_HEAD_DIM
    if cache is None:
        kv_args = (kv, kv)
        kv_specs = [pl.BlockSpec((MEM_TOKENS, width), lambda b, i: (b, 0)),
                    pl.BlockSpec((MEM_TOKENS, width), lambda b, i: (b, 1))]
    else:
        kv_args = cache
        kv_specs = [pl.BlockSpec((MEM_TOKENS, width), lambda b, i: (layer * batch + b, 0))] * 2
    return pl.pallas_call(
        _mem_kernel,
        grid=(batch, nl),
        in_specs=[pl.BlockSpec((tl, width), lambda b, i: (b * nl + i, C_MQ // width))] + kv_specs,
        out_specs=pl.BlockSpec((tl, width), lambda b, i: (b * nl + i, 0)),
        out_shape=jax.ShapeDtypeStruct((m, width), bf16),
        compiler_params=_params(("parallel", "parallel")),
        name="mem_attention",
    )(u, *kv_args)


def _seg64_sum(x, ones_ref):
    hi = x.astype(bf16)
    lo = (x - hi.astype(f32)).astype(bf16)
    cols = []
    for j in range(x.shape[1] // LANES):
        js = slice(j * LANES, (j + 1) * LANES)
        cols.append(_dot(hi[:, js], ones_ref[...]) + _dot(lo[:, js], ones_ref[...]))
    return jnp.concatenate(cols, axis=-1)


def _rwkv_features(xs, prevs, w_refs, ones_ref):
    mur_ref, muk_ref, muv_ref, mulo_ref, w0_ref, a0_ref, wl_ref, kk_ref, ka_ref, rk_ref = w_refs

    def token_shift(x, prev, mu_ref):
        row = lax.broadcasted_iota(jnp.int32, x.shape, 0)
        shifted = jnp.where(row == 0, prev, pltpu.roll(x, 1, 0))
        return x + (shifted - x) * mu_ref[...]

    r, k0, v, lo = (token_shift(x, pv, mu) for x, pv, mu in zip(xs, prevs, (mur_ref, muk_ref, muv_ref, mulo_ref)))
    col = lax.broadcasted_iota(jnp.int32, lo.shape, 1)
    act = jnp.where(col < 64, jnp.tanh(lo), jnp.where(col < 128, lo, _sigmoid(lo)))
    proj = _dot(act.astype(bf16), wl_ref[...])
    log_w = -_softplus(-(w0_ref[...] + proj[:, :BRANCH_WIDTH])) - 0.5
    log_decay = -jnp.exp(log_w)
    a = _sigmoid(a0_ref[...] + proj[:, BRANCH_WIDTH:2 * BRANCH_WIDTH])
    g = proj[:, 2 * BRANCH_WIDTH:]
    kk = k0 * kk_ref[...]
    k = k0 * (1.0 + (a - 1.0) * ka_ref[...])
    tb = kk.shape[0]
    head_sums = _seg64_sum(jnp.concatenate([kk * kk, r * k * rk_ref[...]], axis=0), ones_ref)
    kk = kk / jnp.maximum(jnp.sqrt(head_sums[:tb]), 1e-12)
    bonus = head_sums[tb:] * v
    return r, k, v, kk, a, log_decay, g, bonus


def _rwkv_output(y, bonus, g, lg_ref, lb_ref, ones_ref):
    mu = _seg64_sum(y, ones_ref) * (1.0 / RWKV_HEAD)
    d = y - mu
    var = _seg64_sum(d * d, ones_ref) * (1.0 / RWKV_HEAD)
    yn = d * lax.rsqrt(var + 64e-5) * lg_ref[...] + lb_ref[...]
    return ((yn + bonus) * g).astype(bf16)


def _rwkv_seq_kernel(*refs, tb, steps, reqs):
    x_refs, p_refs, w_refs = refs[0:4], refs[4:8], refs[8:18]
    lg_ref, lb_ref, ones_ref, s0_ref, o_ref, s_ref = refs[18:24]
    w3, kk3, kka3, k3, r3, v3, y3, bonus_s, g_s = refs[24:]
    s_ref[...] = s0_ref[...]
    y3[...] = jnp.zeros_like(y3)

    for q in range(reqs):
        rows = slice(q * tb, (q + 1) * tb)
        xs = [x[rows, :] for x in x_refs]
        prevs = [p[q] for p in p_refs]
        r, k, v, kk, a, log_decay, g, bonus = _rwkv_features(xs, prevs, w_refs, ones_ref)
        decay = jnp.exp(log_decay)
        kka = kk * a
        g_s[rows, :] = g
        bonus_s[rows, :] = bonus
        for h in range(RWKV_HEADS):
            hs = slice(h * RWKV_HEAD, (h + 1) * RWKV_HEAD)
            j = q * RWKV_HEADS + h
            w3[j] = decay[:, hs]
            kk3[j] = kk[:, hs]
            kka3[j] = kka[:, hs]
            k3[j] = k[:, hs]
            r3[j] = r[:, hs]
            v3[j] = v[:, hs]

    eye = (lax.broadcasted_iota(jnp.int32, (RWKV_HEAD, RWKV_HEAD), 0)
           == lax.broadcasted_iota(jnp.int32, (RWKV_HEAD, RWKV_HEAD), 1)).astype(f32)

    def step(t, carry):
        ts = pl.ds(t, 1)
        chains = [(q, h, q * RWKV_HEADS + h) for q in range(reqs) for h in range(RWKV_HEADS)]
        s_olds = [s_ref[q, h] for q, h, _ in chains]
        sas = [jnp.sum(s * kk3[j, ts, :], axis=1, keepdims=True) for s, (_, _, j) in zip(s_olds, chains)]
        v_cols = [jnp.sum(eye * v3[j, ts, :], axis=1, keepdims=True) for _, _, j in chains]
        s_news = [s * w3[j, ts, :] - sa * kka3[j, ts, :] + vc * k3[j, ts, :]
                  for s, sa, vc, (_, _, j) in zip(s_olds, sas, v_cols, chains)]
        for s_new, (q, h, _) in zip(s_news, chains):
            s_ref[q, h] = s_new
        y_cols = [jnp.sum(s * r3[j, ts, :], axis=1, keepdims=True) for s, (_, _, j) in zip(s_news, chains)]
        for y_col, (_, _, j) in zip(y_cols, chains):
            y3[j, ts, :] = jnp.sum(eye * y_col, axis=0, keepdims=True)
        return carry

    lax.fori_loop(0, steps, step, 0)

    for q in range(reqs):
        rows = slice(q * tb, (q + 1) * tb)
        y = jnp.concatenate([y3[q * RWKV_HEADS + h] for h in range(RWKV_HEADS)], axis=-1)
        o_ref[rows, :] = _rwkv_output(y, bonus_s[rows, :], g_s[rows, :], lg_ref, lb_ref, ones_ref)


def _unit_lower_inverse_minus_identity(ns):
    size = ns[0].shape[0]
    t = lax.broadcasted_iota(jnp.int32, (size, size), 0)
    s = lax.broadcasted_iota(jnp.int32, (size, size), 1)
    first = ((t >> 1) == (s >> 1)) & (t > s)
    es = [-jnp.where(first, n, 0.0) for n in ns]
    blk, shift = 4, 2
    while blk <= RWKV_HEAD:
        half = blk // 2
        sel = ((t >> shift) == (s >> shift)) & ((t & (blk - 1)) >= half) & ((s & (blk - 1)) < half)
        cs = [jnp.where(sel, n, 0.0) for n in ns]
        zs = [c + _dot(c.astype(bf16), e.astype(bf16)) for c, e in zip(cs, es)]
        es = [e - z - _dot(e.astype(bf16), z.astype(bf16)) for e, z in zip(es, zs)]
        blk, shift = blk * 2, shift + 1
    return es


def _rwkv_chunk_kernel(*refs, tb):
    x_refs, p_refs, w_refs = refs[0:4], refs[4:8], refs[8:18]
    lg_ref, lb_ref, ones_ref, s0_ref, o_ref, s_ref = refs[18:24]
    carries = refs[24:28]
    sp = refs[28]
    i = pl.program_id(1)
    n_pairs = RWKV_HEADS // 2
    hd = RWKV_HEAD

    @pl.when(i == 0)
    def _():
        zero = jnp.zeros((hd, hd), f32)
        for p in range(n_pairs):
            top = jnp.concatenate([s0_ref[0, 2 * p], zero], axis=1)
            bot = jnp.concatenate([zero, s0_ref[0, 2 * p + 1]], axis=1)
            sp[p] = jnp.concatenate([top, bot], axis=0)
        for carry, p_ref in zip(carries, p_refs):
            carry[...] = p_ref[0]

    xs = [x[...] for x in x_refs]
    prevs = [carry[...] for carry in carries]
    for carry, x in zip(carries, xs):
        carry[...] = x[tb - 1:tb, :]
    r, k, v, kk, a, lw, g, bonus = _rwkv_features(xs, prevs, w_refs, ones_ref)
    beta = kk * a
    row = lax.broadcasted_iota(jnp.int32, lw.shape, 0)
    cum = lw
    d = 1
    while d < tb:
        cum = cum + jnp.where(row >= d, pltpu.roll(cum, d, 0), 0.0)
        d *= 2
    c_last = cum[tb - 1:tb, :]
    a_t = kk * jnp.exp(cum - lw)
    r_t = r * jnp.exp(cum)
    e_neg = jnp.exp(-cum)
    k_t = k * e_neg
    b_t = beta * e_neg
    e_hat = jnp.exp(c_last - cum)
    k_h = k * e_hat
    b_h = beta * e_hat
    gamma = jnp.exp(c_last)

    lane = lax.broadcasted_iota(jnp.int32, (tb, LANES), 1)
    head0 = lane < hd
    split = lambda x: (jnp.where(head0, x, 0.0), jnp.where(head0, 0.0, x))
    tt = lax.broadcasted_iota(jnp.int32, (LANES, LANES), 0)
    ss = lax.broadcasted_iota(jnp.int32, (LANES, LANES), 1)
    strict = tt > ss
    incl = tt >= ss
    same_head = (tt >= hd) == (ss >= hd)
    fold = lambda m: m[:tb, :] + m[tb:, :]

    pairs = range(n_pairs)
    cols = [slice(p * LANES, (p + 1) * LANES) for p in pairs]
    stack = lambda *xs: jnp.concatenate(xs, axis=0)
    scs = [_dot_nt(stack(*split(a_t[:, ps]), *split(r_t[:, ps])).astype(bf16),
                   stack(*split(k_t[:, ps]), *split(b_t[:, ps])).astype(bf16)) for ps in cols]
    es = _unit_lower_inverse_minus_identity([jnp.where(strict, sc[:2 * tb, 2 * tb:], 0.0) for sc in scs])
    eye = jnp.where(tt == ss, 1.0, 0.0)
    t_cats = [fold(eye + e).astype(bf16) for e in es]
    ak_cats = [fold(jnp.where(strict, sc[:2 * tb, :2 * tb], 0.0)).astype(bf16) for sc in scs]
    r_cats = [jnp.concatenate([fold(jnp.where(incl, sc[2 * tb:, :2 * tb], 0.0)),
                               -fold(jnp.where(incl, sc[2 * tb:, 2 * tb:], 0.0))], axis=1).astype(bf16)
              for sc in scs]
    s_olds = [sp[p] for p in pairs]
    grs = [_dot_nt(stack(a_t[:, ps], r_t[:, ps]).astype(bf16), s_old.astype(bf16))
           for ps, s_old in zip(cols, s_olds)]
    v_sts = [stack(*split(v[:, ps])).astype(bf16) for ps in cols]
    u_rhss = [gr[:tb] + _dot(ak, v_st) for gr, ak, v_st in zip(grs, ak_cats, v_sts)]
    us = [_dot(t_cat, stack(*split(u_rhs)).astype(bf16)) for t_cat, u_rhs in zip(t_cats, u_rhss)]
    ys = [gr[tb:] + _dot(r_cat, stack(v_st, stack(*split(u)).astype(bf16)))
          for gr, r_cat, v_st, u in zip(grs, r_cats, v_sts, us)]
    for p, ps, s_old, u in zip(pairs, cols, s_olds, us):
        vu_t = stack(v[:, ps], -u).T.astype(bf16)
        kb = stack(k_h[:, ps], b_h[:, ps]).astype(bf16)
        sp[p] = s_old * gamma[:, ps] + jnp.where(same_head, _dot(vu_t, kb), 0.0)

    o_ref[...] = _rwkv_output(jnp.concatenate(ys, axis=-1), bonus, g, lg_ref, lb_ref, ones_ref)

    @pl.when(i == pl.num_programs(1) - 1)
    def _():
        for p in range(n_pairs):
            full = sp[p]
            s_ref[0, 2 * p] = full[:hd, :hd]
            s_ref[0, 2 * p + 1] = full[hd:, hd:]


def rwkv(u, prev, p, s0, layer, ones2, batch, l_pad, l_real, chunked, reqs=1):
    m = u.shape[0]
    bw = BRANCH_WIDTH
    full = lambda w: pl.BlockSpec((1, w), lambda b, i: (0, 0))
    if chunked:
        tb, nb, nblk = RWKV_HEAD, 1, l_pad // RWKV_HEAD
        assert l_real == l_pad and l_pad % tb == 0
        body = functools.partial(_rwkv_chunk_kernel, tb=tb)
        scratch = [pltpu.VMEM((1, bw), f32), pltpu.VMEM((1, bw), f32), pltpu.VMEM((1, bw), f32),
                   pltpu.VMEM((1, RWKV_LORA), f32), pltpu.VMEM((RWKV_HEADS // 2, LANES, LANES), f32)]
    else:
        tb, nb, nblk = l_pad, reqs, 1
        assert batch % reqs == 0
        body = functools.partial(_rwkv_seq_kernel, tb=tb, steps=l_real, reqs=reqs)
        head_buf = pltpu.VMEM((reqs * RWKV_HEADS, tb, RWKV_HEAD), f32)
        scratch = [head_buf] * 7 + [pltpu.VMEM((reqs * tb, bw), f32), pltpu.VMEM((reqs * tb, bw), f32)]
    rows = nb * tb
    row = lambda b, i: b * nblk + i
    prev_spec = lambda w: pl.BlockSpec((None, nb, 1, w), lambda b, i: (layer, b, 0, 0))
    return pl.pallas_call(
        body,
        grid=(batch // nb, nblk),
        in_specs=[pl.BlockSpec((rows, bw), lambda b, i: (row(b, i), C_R // bw)),
                  pl.BlockSpec((rows, bw), lambda b, i: (row(b, i), C_K // bw)),
                  pl.BlockSpec((rows, bw), lambda b, i: (row(b, i), C_V // bw)),
                  pl.BlockSpec((rows, RWKV_LORA), lambda b, i: (row(b, i), C_LORA // RWKV_LORA)),
                  prev_spec(bw), prev_spec(bw), prev_spec(bw), prev_spec(RWKV_LORA),
                  full(bw), full(bw), full(bw), full(RWKV_LORA),
                  full(bw), full(bw),
                  pl.BlockSpec((RWKV_LORA, 3 * bw), lambda b, i: (0, 0)),
                  full(bw), full(bw), full(bw), full(bw), full(bw),
                  pl.BlockSpec((LANES, LANES), lambda b, i: (0, 0)),
                  pl.BlockSpec((None, nb, RWKV_HEADS, RWKV_HEAD, RWKV_HEAD), lambda b, i: (layer, b, 0, 0, 0))],
        out_specs=[pl.BlockSpec((rows, bw), lambda b, i: (row(b, i), 0)),
                   pl.BlockSpec((nb, RWKV_HEADS, RWKV_HEAD, RWKV_HEAD), lambda b, i: (b, 0, 0, 0))],
        out_shape=[jax.ShapeDtypeStruct((m, bw), bf16),
                   jax.ShapeDtypeStruct((batch, RWKV_HEADS, RWKV_HEAD, RWKV_HEAD), f32)],
        scratch_shapes=scratch,
        compiler_params=_params(("parallel", "arbitrary")),
        name="rwkv7_chunked" if chunked else "rwkv7_seq",
    )(u, u, u, u, *prev, p['mu_r'], p['mu_k'], p['mu_v'], p['mu_lo'], p['w0'], p['a0'], p['w_lora'],
      p['k_k'], p['k_a'], p['r_k'], p['ln_g'], p['ln_b'], ones2, s0)


IN_TILE = 1024
IN_TAIL_TILE = C_LORA // IN_TILE


def _in_proj_weights(w_in):
    wt = jnp.swapaxes(w_in, 1, 2)
    seg = lambda o, n: wt[:, o:o + n]
    tail = jnp.concatenate([seg(_O_RU + 3072, RWKV_LORA), seg(_O_SK, 128), seg(_O_SV, 128),
                            seg(_O_GA, GLA_GATE_RANK),
                            jnp.zeros((wt.shape[0], N_PACK - C_GA - GLA_GATE_RANK, wt.shape[2]), f32)], axis=1)
    return wt, tail.astype(bf16)


def _in_tile_source(j):
    return jnp.where(j < 2, _O_GV + IN_TILE * j,
           jnp.where(j == 2, _O_SQ,
           jnp.where(j == 3, _O_MQ,
           jnp.where(j < 7, _O_RU + IN_TILE * (j - 4),
           jnp.where(j < 15, _O_GPRE + IN_TILE * (j - 7), _O_GQ)))))


def _in_proj_kernel(x_ref, w_ref, tail_ref, o_ref, w_bf):
    j = pl.program_id(0)
    first_row_block = pl.program_id(1) == 0

    @pl.when(first_row_block & (j != IN_TAIL_TILE))
    def _():
        w_bf[...] = w_ref[0].astype(bf16)

    @pl.when(first_row_block & (j == IN_TAIL_TILE))
    def _():
        w_bf[...] = tail_ref[...]

    o_ref[...] = _dot_nt(x_ref[...], w_bf[...])


def in_projection(x, wt, tail, layer, tm):
    m, k = x.shape
    return pl.pallas_call(
        _in_proj_kernel,
        grid=(N_PACK // IN_TILE, m // tm),
        in_specs=[pl.BlockSpec((tm, k), lambda j, i: (i, 0)),
                  pl.BlockSpec((pl.Element(1), pl.Element(IN_TILE), pl.Element(k)),
                               lambda j, i: (layer, pl.multiple_of(_in_tile_source(j), 16), 0)),
                  pl.BlockSpec((None, IN_TILE, k), lambda j, i: (layer, 0, 0))],
        out_specs=pl.BlockSpec((tm, IN_TILE), lambda j, i: (i, j)),
        out_shape=jax.ShapeDtypeStruct((m, N_PACK), f32),
        scratch_shapes=[pltpu.VMEM((IN_TILE, k), bf16)],
        compiler_params=_params(("parallel", "arbitrary")),
        name="in_proj",
    )(x, wt, tail)


def _layer_params(l, w):
    row = lambda x: x[l].reshape(1, -1)
    mu = w['rwkv_mu'][l]
    z = lambda r, c: jnp.zeros((r, c), f32)
    w_lora = jnp.concatenate([
        jnp.concatenate([w['rwkv_w2'][l], z(64, 1024), z(64, 1024)], axis=1),
        jnp.concatenate([z(64, 1024), w['rwkv_a2'][l], z(64, 1024)], axis=1),
        jnp.concatenate([z(128, 1024), z(128, 1024), w['rwkv_g2'][l]], axis=1)], axis=0).astype(bf16)
    a_up = jnp.concatenate([w['gla_a_up'][l], z(LANES - GLA_GATE_RANK, GLA_QK)], axis=0).astype(bf16)
    return {
        'a_up': a_up, 'a_b': row(w['gla_a_b']),
        'gla_g': row(w['gla_norm_g']), 'gla_b': row(w['gla_norm_b']),
        'sinks': w['swa_sinks'][l],
        'rwkv': {'mu_r': mu[:1024].reshape(1, -1), 'mu_k': mu[1024:2048].reshape(1, -1),
                 'mu_v': mu[2048:3072].reshape(1, -1), 'mu_lo': mu[3072:].reshape(1, -1),
                 'w0': row(w['rwkv_w0']), 'a0': row(w['rwkv_a0']), 'w_lora': w_lora,
                 'k_k': row(w['rwkv_k_k']), 'k_a': row(w['rwkv_k_a']), 'r_k': row(w['rwkv_r_k']),
                 'ln_g': row(w['rwkv_ln_g']), 'ln_b': row(w['rwkv_ln_b'])},
        'ln1_g': row(w['ln1_g']), 'ln1_b': row(w['ln1_b']),
        'ln2_g': row(w['ln2_g']), 'ln2_b': row(w['ln2_b']),
    }


def _tiles(m):
    if m >= 1024:
        return {'proj': 1024, 'merge': 512, 'ffn': 2048, 'ln_out': 512, 'ln_down': 256}
    return {'proj': m, 'merge': m, 'ffn': m, 'ln_out': m, 'ln_down': m}


def _trunk_layer(h_f, h_b, l, p, big, ones2, mem_kv, mem_cache, gla_s0, rwkv_s0, rwkv_prev, swa_buf,
                 batch, l_pad, l_real):
    m = h_f.shape[0]
    tl = _tiles(m)
    u = in_projection(h_b, big['w_in_t'], big['w_in_tail'], l, tl['proj'])
    o_a, gla_s = gla(u, p['a_up'], p['a_b'], p['gla_g'], p['gla_b'], gla_s0[0], gla_s0[1], batch, l_pad, l_real,
                     min(GLA_CHUNK, l_pad))
    if swa_buf is None:
        o_b = swa_prompt(u, p['sinks'], batch, l_pad)
    else:
        o_b = swa_sample(u, swa_buf[0], swa_buf[1], l, p['sinks'], batch, l_pad, l_real, 4)
    chunked = l_real == l_pad and l_pad % RWKV_HEAD == 0
    o_c, rwkv_s = rwkv(u, rwkv_prev[0], p['rwkv'], rwkv_s0[0], rwkv_s0[1], ones2, batch, l_pad, l_real,
                       chunked, reqs=1 if chunked else 4)
    o_m = mem_attention(u, mem_kv, mem_cache, l, batch, l_pad, min(512, l_pad))
    merged = gated_merge((o_a, o_b, o_c, o_m), u, big['gate_b'], big['w_branch'], l, tl['merge'], 512)
    x_f, x_b = matmul_residual_ln(merged, big['w_out'], l, h_f, p['ln1_g'], p['ln1_b'], tl['ln_out'])
    act = ffn_up(x_b, big['w_gu'], l, tl['ffn'], 512)
    y_f, y_b = matmul_residual_ln(act, big['w_down'], l, x_f, p['ln2_g'], p['ln2_b'], tl['ln_down'])
    return y_f, y_b, u, gla_s, rwkv_s


def _split_ru(x):
    return (x[..., :1024], x[..., 1024:2048], x[..., 2048:3072], x[..., 3072:])


def kernel(x_prompt, x_sample, mem_prompt, cache_swa_k, cache_swa_v, cache_mem_k, cache_mem_v, state_gla, state_rwkv, state_rwkv_shift, w_in, gate_b, gla_a_up, gla_a_b, gla_norm_g, gla_norm_b, swa_sinks, rwkv_mu, rwkv_w0, rwkv_w2, rwkv_a0, rwkv_a2, rwkv_g2, rwkv_k_k, rwkv_k_a, rwkv_r_k, rwkv_ln_g, rwkv_ln_b, w_mem_kv, w_branch, w_out, ln1_g, ln1_b, w_gu, w_down, ln2_g, ln2_b):
    weights = {'gla_a_up': gla_a_up, 'gla_a_b': gla_a_b,
               'gla_norm_g': gla_norm_g, 'gla_norm_b': gla_norm_b, 'swa_sinks': swa_sinks,
               'rwkv_mu': rwkv_mu, 'rwkv_w0': rwkv_w0, 'rwkv_w2': rwkv_w2, 'rwkv_a0': rwkv_a0,
               'rwkv_a2': rwkv_a2, 'rwkv_g2': rwkv_g2, 'rwkv_k_k': rwkv_k_k, 'rwkv_k_a': rwkv_k_a,
               'rwkv_r_k': rwkv_r_k, 'rwkv_ln_g': rwkv_ln_g, 'rwkv_ln_b': rwkv_ln_b,
               'ln1_g': ln1_g, 'ln1_b': ln1_b, 'ln2_g': ln2_g, 'ln2_b': ln2_b}
    w_in_t, w_in_tail = _in_proj_weights(w_in)
    big = {'w_in_t': w_in_t, 'w_in_tail': w_in_tail, 'gate_b': gate_b, 'w_branch': w_branch, 'w_gu': w_gu,
           'w_out': w_out.astype(bf16), 'w_down': w_down.astype(bf16)}
    bp, lp, _ = x_prompt.shape
    bs, ls, _ = x_sample.shape
    ls_pad = -(-ls // SUBLANES) * SUBLANES
    mp, ms = bp * lp, bs * ls_pad

    hp_f = x_prompt.reshape(mp, D_MODEL)
    hs_f = jnp.pad(x_sample, ((0, 0), (0, ls_pad - ls), (0, 0))).reshape(ms, D_MODEL)
    hp_b, hs_b = hp_f.astype(bf16), hs_f.astype(bf16)
    mem_b = mem_prompt.reshape(bp * MEM_TOKENS, D_MODEL).astype(bf16)
    half = LANES // 2
    blk = jnp.ones((half, half), f32)
    zero = jnp.zeros((half, half), f32)
    ones2 = jnp.concatenate([jnp.concatenate([blk, zero], 1), jnp.concatenate([zero, blk], 1)], 0).astype(bf16)

    gla0_p = jnp.zeros((1, bp, GLA_HEADS, GLA_DK, GLA_DV), f32)
    rwkv0_p = jnp.zeros((1, bp, RWKV_HEADS, RWKV_HEAD, RWKV_HEAD), f32)
    prev0_p = _split_ru(jnp.zeros((1, bp, 1, RWKV_COLS), f32))
    prev_s = _split_ru(state_rwkv_shift)
    kvw = SWA_KV_HEADS * SWA_HEAD_DIM
    kbuf = cache_swa_k.reshape(DEPTH, bs * WINDOW, kvw)
    vbuf = cache_swa_v.reshape(DEPTH, bs * WINDOW, kvw)
    mem_k2d = cache_mem_k.reshape(DEPTH * bs * MEM_TOKENS, MEM_HEADS * MEM_HEAD_DIM)
    mem_v2d = cache_mem_v.reshape(DEPTH * bs * MEM_TOKENS, MEM_HEADS * MEM_HEAD_DIM)

    outs = {k: [] for k in ('p_swk', 'p_swv', 'p_mk', 'p_mv', 'p_gla', 'p_rw', 'p_rs',
                            's_swk', 's_swv', 's_gla', 's_rw', 's_rs')}
    for l in range(DEPTH):
        p = _layer_params(l, weights)
        kv = matmul(mem_b, w_mem_kv, l, bp * MEM_TOKENS, 512, "mem_kv")
        hp_f, hp_b, u, gs, rs = _trunk_layer(hp_f, hp_b, l, p, big, ones2, kv, None, (gla0_p, 0), (rwkv0_p, 0),
                                             (prev0_p, 0), None, bp, lp, lp)
        u3 = u.reshape(bp, lp, N_PACK)
        outs['p_swk'].append(u3[:, lp - WINDOW:, C_SK:C_SK + kvw].reshape(bp, WINDOW, SWA_KV_HEADS, SWA_HEAD_DIM))
        outs['p_swv'].append(u3[:, lp - WINDOW:, C_SV:C_SV + kvw].reshape(bp, WINDOW, SWA_KV_HEADS, SWA_HEAD_DIM))
        outs['p_mk'].append(kv[:, :1024].reshape(bp, MEM_TOKENS, MEM_HEADS, MEM_HEAD_DIM))
        outs['p_mv'].append(kv[:, 1024:].reshape(bp, MEM_TOKENS, MEM_HEADS, MEM_HEAD_DIM))
        outs['p_gla'].append(gs)
        outs['p_rw'].append(rs)
        outs['p_rs'].append(jnp.concatenate([u3[:, lp - 1:, C_R:C_R + 3072],
                                             u3[:, lp - 1:, C_LORA:C_LORA + RWKV_LORA]], axis=-1))
        hs_f, hs_b, u, gs, rs = _trunk_layer(hs_f, hs_b, l, p, big, ones2, None, (mem_k2d, mem_v2d),
                                             (state_gla, l), (state_rwkv, l), (prev_s, l), (kbuf, vbuf),
                                             bs, ls_pad, ls)
        u3 = u.reshape(bs, ls_pad, N_PACK)
        k_new = u3[:, :ls, C_SK:C_SK + kvw].reshape(bs, ls, SWA_KV_HEADS, SWA_HEAD_DIM)
        v_new = u3[:, :ls, C_SV:C_SV + kvw].reshape(bs, ls, SWA_KV_HEADS, SWA_HEAD_DIM)
        outs['s_swk'].append(jnp.concatenate([cache_swa_k[l][:, ls:], k_new], axis=1))
        outs['s_swv'].append(jnp.concatenate([cache_swa_v[l][:, ls:], v_new], axis=1))
        outs['s_gla'].append(gs)
        outs['s_rw'].append(rs)
        outs['s_rs'].append(jnp.concatenate([u3[:, ls - 1:ls, C_R:C_R + 3072],
                                             u3[:, ls - 1:ls, C_LORA:C_LORA + RWKV_LORA]], axis=-1))

    st = {k: jnp.stack(v) for k, v in outs.items()}
    y_prompt = hp_f.reshape(bp, lp, D_MODEL)
    y_sample = hs_f.reshape(bs, ls_pad, D_MODEL)[:, :ls]
    return (y_prompt, y_sample,
            st['p_swk'], st['p_swv'], st['p_mk'], st['p_mv'], st['p_gla'], st['p_rw'], st['p_rs'],
            st['s_swk'], st['s_swv'], st['s_gla'], st['s_rw'], st['s_rs'])
```

```python
import functools

import jax
import jax.numpy as jnp
from jax import lax
from jax.experimental import pallas as pl
from jax.experimental.pallas import tpu as pltpu

f32 = jnp.float32
bf16 = jnp.bfloat16

D_MODEL = 2048
DEPTH = 4
BRANCH_WIDTH = 1024
N_BRANCH = 4
GLA_HEADS = 4
GLA_QK = 512
GLA_V = 1024
GLA_DK = 128
GLA_DV = 256
GLA_GATE_RANK = 16
GLA_TAU = 16.0
GLA_CHUNK = 64
SWA_HEAD_DIM = 64
SWA_Q_HEADS = 16
SWA_KV_HEADS = 2
SWA_GROUP = SWA_Q_HEADS // SWA_KV_HEADS
WINDOW = 128
RWKV_HEAD = 64
RWKV_HEADS = 16
RWKV_LORA = 256
RWKV_COLS = 3 * BRANCH_WIDTH + RWKV_LORA
MEM_TOKENS = 256
MEM_HEADS = 4
MEM_HEAD_DIM = 256
D_FF = 5632
DEEPNORM_ALPHA = (2 * DEPTH) ** 0.25
NEG = -1e30

_O_GQ, _O_GK, _O_GV, _O_GR, _O_GA = 0, 512, 1024, 2048, 3072
_O_SQ, _O_SK, _O_SV = 3088, 4112, 4240
_O_RU = 4368
_O_MQ = 7696
_O_GPRE = 8720
_N_IN = 16912

C_GV, C_GR, C_SQ, C_MQ, C_R, C_K, C_V = 0, 1024, 2048, 3072, 4096, 5120, 6144
C_GPRE = 7168
C_GQ, C_GK = 15360, 15872
C_LORA = 16384
C_SK, C_SV, C_GA = 16640, 16768, 16896
N_PACK = 17408

LANES = 128
SUBLANES = 8
VMEM_BYTES_V7X = 64 * 1024 * 1024
VMEM_LIMIT = VMEM_BYTES_V7X - 8 * 1024 * 1024


def _params(sem):
    return pltpu.CompilerParams(dimension_semantics=sem, vmem_limit_bytes=VMEM_LIMIT)


def _softplus(z):
    return jnp.maximum(z, 0.0) + jnp.log(1.0 + jnp.exp(-jnp.abs(z)))


def _sigmoid(z):
    return 0.5 * jnp.tanh(0.5 * z) + 0.5


def _dot(a, b):
    return jnp.dot(a, b, preferred_element_type=f32)


def _dot_nt(a, b):
    return lax.dot_general(a, b, (((1,), (1,)), ((), ())), preferred_element_type=f32)


def _mm_kernel(x_ref, w_ref, o_ref, *, trans_w):
    mm = _dot_nt if trans_w else _dot
    o_ref[...] = mm(x_ref[...], w_ref[...].astype(bf16))


def matmul(x, w, layer, tm, tn, name, trans_w=False):
    m, k = x.shape
    n = w.shape[1] if trans_w else w.shape[2]
    w_spec = (pl.BlockSpec((None, tn, k), lambda i, j: (layer, j, 0)) if trans_w
              else pl.BlockSpec((None, k, tn), lambda i, j: (layer, 0, j)))
    return pl.pallas_call(
        functools.partial(_mm_kernel, trans_w=trans_w),
        grid=(m // tm, n // tn),
        in_specs=[pl.BlockSpec((tm, k), lambda i, j: (i, 0)), w_spec],
        out_specs=pl.BlockSpec((tm, tn), lambda i, j: (i, j)),
        out_shape=jax.ShapeDtypeStruct((m, n), f32),
        compiler_params=_params(("parallel", "parallel")),
        name=name,
    )(x, w)


def _mm_ln_kernel(x_ref, w_ref, res_ref, g_ref, b_ref, of_ref, ob_ref):
    z = DEEPNORM_ALPHA * res_ref[...] + _dot(x_ref[...], w_ref[...])
    mu = jnp.mean(z, axis=-1, keepdims=True)
    d = z - mu
    var = jnp.mean(d * d, axis=-1, keepdims=True)
    y = d * lax.rsqrt(var + 1e-5) * g_ref[...] + b_ref[...]
    of_ref[...] = y
    ob_ref[...] = y.astype(bf16)


def matmul_residual_ln(x, w, layer, res, g, b, tm):
    m, k = x.shape
    n = w.shape[2]
    return pl.pallas_call(
        _mm_ln_kernel,
        grid=(m // tm,),
        in_specs=[pl.BlockSpec((tm, k), lambda i: (i, 0)),
                  pl.BlockSpec((None, k, n), lambda i: (layer, 0, 0), pipeline_mode=pl.Buffered(1)),
                  pl.BlockSpec((tm, n), lambda i: (i, 0)),
                  pl.BlockSpec((1, n), lambda i: (0, 0)),
                  pl.BlockSpec((1, n), lambda i: (0, 0))],
        out_specs=[pl.BlockSpec((tm, n), lambda i: (i, 0)),
                   pl.BlockSpec((tm, n), lambda i: (i, 0))],
        out_shape=[jax.ShapeDtypeStruct((m, n), f32), jax.ShapeDtypeStruct((m, n), bf16)],
        compiler_params=_params(("parallel",)),
        name="proj_ln",
    )(x, w, res, g, b)


def _merge_kernel(a_ref, b_ref, c_ref, m_ref, g0_ref, g1_ref, g2_ref, g3_ref, gb_ref, w_ref, o_ref, w_bf):
    @pl.when(pl.program_id(1) == 0)
    def _():
        w_bf[...] = w_ref[...].astype(bf16)

    acc = None
    for n, (br, gp) in enumerate(((a_ref, g0_ref), (b_ref, g1_ref), (c_ref, g2_ref), (m_ref, g3_ref))):
        y = _dot(br[...], w_bf[n])
        gate = _sigmoid(gp[...] + gb_ref[n:n + 1, :])
        acc = gate * y if acc is None else acc + gate * y
    o_ref[...] = acc.astype(bf16)


def gated_merge(branches, u, gate_b, w_branch, layer, tm, tn):
    m = u.shape[0]
    gp0 = C_GPRE // tn
    per = D_MODEL // tn
    br_spec = pl.BlockSpec((tm, BRANCH_WIDTH), lambda j, i: (i, 0))
    gp_specs = [pl.BlockSpec((tm, tn), functools.partial(lambda j, i, n: (i, gp0 + n * per + j), n=n))
                for n in range(N_BRANCH)]
    return pl.pallas_call(
        _merge_kernel,
        grid=(D_MODEL // tn, m // tm),
        in_specs=[br_spec] * 4 + gp_specs + [
            pl.BlockSpec((None, N_BRANCH, tn), lambda j, i: (layer, 0, j)),
            pl.BlockSpec((None, N_BRANCH, BRANCH_WIDTH, tn), lambda j, i: (layer, 0, 0, j))],
        out_specs=pl.BlockSpec((tm, tn), lambda j, i: (i, j)),
        out_shape=jax.ShapeDtypeStruct((m, D_MODEL), bf16),
        scratch_shapes=[pltpu.VMEM((N_BRANCH, BRANCH_WIDTH, tn), bf16)],
        compiler_params=_params(("parallel", "arbitrary")),
        name="gated_merge",
    )(*branches, u, u, u, u, gate_b, w_branch)


def _ffn_up_kernel(x_ref, wg_ref, wu_ref, o_ref, wg_bf, wu_bf):
    @pl.when(pl.program_id(1) == 0)
    def _():
        wg_bf[...] = wg_ref[...].astype(bf16)
        wu_bf[...] = wu_ref[...].astype(bf16)

    x = x_ref[...]
    g = _dot(x, wg_bf[...])
    up = _dot(x, wu_bf[...])
    o_ref[...] = (g * _sigmoid(g) * up).astype(bf16)


def ffn_up(x, w_gu, layer, tm, tn):
    m, k = x.shape
    nj = D_FF // tn
    return pl.pallas_call(
        _ffn_up_kernel,
        grid=(nj, m // tm),
        in_specs=[pl.BlockSpec((tm, k), lambda j, i: (i, 0)),
                  pl.BlockSpec((None, k, tn), lambda j, i: (layer, 0, j)),
                  pl.BlockSpec((None, k, tn), lambda j, i: (layer, 0, nj + j))],
        out_specs=pl.BlockSpec((tm, tn), lambda j, i: (i, j)),
        out_shape=jax.ShapeDtypeStruct((m, D_FF), bf16),
        scratch_shapes=[pltpu.VMEM((k, tn), bf16), pltpu.VMEM((k, tn), bf16)],
        compiler_params=_params(("parallel", "arbitrary")),
        name="ffn_up",
    )(x, w_gu, w_gu)


def _gla_kernel(q_ref, k_ref, v_ref, r_ref, a_ref, aup_ref, ab_ref, ng_ref, nb_ref, s0_ref,
                o_ref, s_ref, *, chunk, sub, l_real, l_pad):
    @pl.when(pl.program_id(1) == 0)
    def _():
        s_ref[...] = s0_ref[...]

    for ci in range(sub):
        rows = slice(ci * chunk, (ci + 1) * chunk)
        refs = [x.at[rows, :] for x in (q_ref, k_ref, v_ref, r_ref, a_ref, o_ref)]
        _gla_chunk(*refs[:5], aup_ref, ab_ref, ng_ref, nb_ref, refs[5], s_ref,
                   (pl.program_id(1) * sub + ci) * chunk, chunk, l_real, l_pad)


def _gla_chunk(q_ref, k_ref, v_ref, r_ref, a_ref, aup_ref, ab_ref, ng_ref, nb_ref, o_ref, s_ref,
               t0, chunk, l_real, l_pad):
    la = _dot(a_ref[...].astype(bf16), aup_ref[...]) + ab_ref[...]
    la = (jnp.minimum(la, 0.0) - jnp.log(1.0 + jnp.exp(-jnp.abs(la)))) * (1.0 / GLA_TAU)
    row = lax.broadcasted_iota(jnp.int32, (chunk, GLA_QK), 0)
    kin = k_ref[...]
    if l_real < l_pad:
        real = (t0 + row) < l_real
        la = jnp.where(real, la, 0.0)
        kin = jnp.where(real, kin, 0.0)
    b = la
    d = 1
    while d < chunk:
        b = b + jnp.where(row >= d, pltpu.roll(b, d, 0), 0.0)
        d *= 2
    b_last = b[chunk - 1:chunk, :]
    q_dec = q_ref[...] * (GLA_DK ** -0.5) * jnp.exp(b)
    k_inv = kin * jnp.exp(-b)
    k_dec = kin * jnp.exp(b_last - b)
    tt = lax.broadcasted_iota(jnp.int32, (chunk, chunk), 0)
    ss = lax.broadcasted_iota(jnp.int32, (chunk, chunk), 1)
    causal = tt >= ss
    heads = range(GLA_HEADS)
    kss = [slice(h * GLA_DK, (h + 1) * GLA_DK) for h in heads]
    vss = [slice(h * GLA_DV, (h + 1) * GLA_DV) for h in heads]
    qhs = [q_dec[:, ks].astype(bf16) for ks in kss]
    atts = [jnp.where(causal, _dot_nt(qh, k_inv[:, ks].astype(bf16)), 0.0).astype(bf16) for qh, ks in zip(qhs, kss)]
    vhs = [v_ref[:, vs].astype(bf16) for vs in vss]
    s_olds = [s_ref[0, h] for h in heads]
    os_ = [_dot(att, vh) + _dot(qh, s_old.astype(bf16)) for att, vh, qh, s_old in zip(atts, vhs, qhs, s_olds)]
    decay_cols = [jnp.exp(jnp.sum(la[:, ks].T, axis=1, keepdims=True)) for ks in kss]
    for h, ks, vh, s_old, decay_col in zip(heads, kss, vhs, s_olds, decay_cols):
        s_ref[0, h] = decay_col * s_old + _dot(k_dec[:, ks].T.astype(bf16), vh)
    for vs, o in zip(vss, os_):
        mu = jnp.mean(o, axis=-1, keepdims=True)
        dd = o - mu
        var = jnp.mean(dd * dd, axis=-1, keepdims=True)
        nrm = dd * lax.rsqrt(var + 1e-5) * ng_ref[:, vs] + nb_ref[:, vs]
        gr = r_ref[:, vs]
        o_ref[:, vs] = (nrm * (gr * _sigmoid(gr))).astype(bf16)


def gla(u, a_up, a_b, norm_g, norm_b, s0, layer, batch, l_pad, l_real, chunk):
    m = u.shape[0]
    sub = next(s for s in (4, 2, 1) if l_pad % (s * chunk) == 0)
    rows = sub * chunk
    nc = l_pad // rows
    row = lambda b, c: b * nc + c
    return pl.pallas_call(
        functools.partial(_gla_kernel, chunk=chunk, sub=sub, l_real=l_real, l_pad=l_pad),
        grid=(batch, nc),
        in_specs=[pl.BlockSpec((rows, GLA_QK), lambda b, c: (row(b, c), C_GQ // GLA_QK)),
                  pl.BlockSpec((rows, GLA_QK), lambda b, c: (row(b, c), C_GK // GLA_QK)),
                  pl.BlockSpec((rows, GLA_V), lambda b, c: (row(b, c), C_GV // GLA_V)),
                  pl.BlockSpec((rows, GLA_V), lambda b, c: (row(b, c), C_GR // GLA_V)),
                  pl.BlockSpec((rows, LANES), lambda b, c: (row(b, c), C_GA // LANES)),
                  pl.BlockSpec((LANES, GLA_QK), lambda b, c: (0, 0)),
                  pl.BlockSpec((1, GLA_QK), lambda b, c: (0, 0)),
                  pl.BlockSpec((1, GLA_V), lambda b, c: (0, 0)),
                  pl.BlockSpec((1, GLA_V), lambda b, c: (0, 0)),
                  pl.BlockSpec((None, 1, GLA_HEADS, GLA_DK, GLA_DV), lambda b, c: (layer, b, 0, 0, 0))],
        out_specs=[pl.BlockSpec((rows, GLA_V), lambda b, c: (row(b, c), 0)),
                   pl.BlockSpec((1, GLA_HEADS, GLA_DK, GLA_DV), lambda b, c: (b, 0, 0, 0))],
        out_shape=[jax.ShapeDtypeStruct((m, GLA_V), bf16),
                   jax.ShapeDtypeStruct((batch, GLA_HEADS, GLA_DK, GLA_DV), f32)],
        compiler_params=_params(("parallel", "arbitrary")),
        name="gla",
    )(u, u, u, u, u, a_up, a_b, norm_g, norm_b, s0)


def _alibi_slopes():
    return 2.0 ** (-8.0 * jnp.arange(1, SWA_Q_HEADS + 1, dtype=f32) / SWA_Q_HEADS)


def _swa_bias(t_pos, key_pos, key_ok):
    dist = (t_pos[:, None] - key_pos[None, :]).astype(f32)
    valid = (dist >= 0) & (dist <= WINDOW) & key_ok[None, :]
    slopes = _alibi_slopes().reshape(SWA_KV_HEADS, SWA_GROUP, 1, 1)
    bias = jnp.where(valid[None, None], -slopes * dist[None, None], NEG)
    return bias.reshape(SWA_KV_HEADS, SWA_GROUP * t_pos.shape[0], key_pos.shape[0])


def _swa_groups(q_all, keys, vals, biases, sink_ref, rows):
    kvs = range(SWA_KV_HEADS)
    d = SWA_HEAD_DIM
    pairs = SWA_GROUP // 2
    heads = [range(kv * SWA_GROUP, (kv + 1) * SWA_GROUP) for kv in kvs]
    low = lax.broadcasted_iota(jnp.int32, (rows, LANES), 1) < d

    def both_halves(x, kv):
        swapped = pltpu.roll(x, d, 1)
        first = lax.broadcasted_iota(jnp.int32, x.shape, 1) < d
        return jnp.where(first, x, swapped) if kv == 0 else jnp.where(first, swapped, x)

    def stacked_queries(kv):
        parts = []
        for p in range(kv * pairs, (kv + 1) * pairs):
            q_pair = q_all[:, p * LANES:(p + 1) * LANES]
            parts += [jnp.where(low, q_pair, 0.0), jnp.where(low, 0.0, q_pair)]
        return jnp.concatenate(parts, axis=0).astype(bf16)

    qs = [stacked_queries(kv) for kv in kvs]
    k2s = [jnp.concatenate([both_halves(kb, kv) for kb in keys], axis=0).astype(bf16) for kv in kvs]
    v2s = [jnp.concatenate([both_halves(vb, kv) for vb in vals], axis=0).astype(bf16) for kv in kvs]
    ss = [_dot_nt(q, k2) * (d ** -0.5) + bias for q, k2, bias in zip(qs, k2s, biases)]
    sinks = [jnp.concatenate([jnp.full((rows, 1), sink_ref[h], f32) for h in hh], axis=0) for hh in heads]
    ms = [jnp.maximum(jnp.max(s, axis=-1, keepdims=True), sink) for s, sink in zip(ss, sinks)]
    ps = [jnp.exp(s - m) for s, m in zip(ss, ms)]
    dens = [jnp.sum(p, axis=-1, keepdims=True) + jnp.exp(sink - m) for p, sink, m in zip(ps, sinks, ms)]
    os_ = [_dot(p.astype(bf16), v2) / den for p, v2, den in zip(ps, v2s, dens)]
    return [jnp.where(low, o[(2 * p) * rows:(2 * p + 1) * rows, :], o[(2 * p + 1) * rows:(2 * p + 2) * rows, :])
            for o in os_ for p in range(pairs)]


def _swa_prompt_kernel(sink_ref, bias_ref, q_ref, kc_ref, kp_ref, vc_ref, vp_ref, o_ref):
    keys = (kp_ref[...], kc_ref[...])
    vals = (vp_ref[...], vc_ref[...])
    biases = [bias_ref[0, kv] for kv in range(SWA_KV_HEADS)]
    outs = _swa_groups(q_ref[...], keys, vals, biases, sink_ref, WINDOW)
    o_ref[...] = jnp.concatenate(outs, axis=-1).astype(bf16)


def swa_prompt(u, sinks, batch, seq):
    m = u.shape[0]
    nb = seq // WINDOW
    cur = lambda b, i: b * nb + i
    prev = lambda b, i: b * nb + jnp.maximum(i - 1, 0)
    kcol, vcol = C_SK // LANES, C_SV // LANES
    t_pos = jnp.arange(WINDOW)
    key_pos = jnp.arange(2 * WINDOW) - WINDOW
    bias = jnp.stack([_swa_bias(t_pos, key_pos, key_pos >= 0), _swa_bias(t_pos, key_pos, key_pos >= -WINDOW)])
    return pl.pallas_call(
        _swa_prompt_kernel,
        grid=(batch, nb),
        in_specs=[pl.BlockSpec(memory_space=pltpu.SMEM),
                  pl.BlockSpec((1,) + bias.shape[1:], lambda b, i: (jnp.minimum(i, 1), 0, 0, 0)),
                  pl.BlockSpec((WINDOW, BRANCH_WIDTH), lambda b, i: (cur(b, i), C_SQ // BRANCH_WIDTH)),
                  pl.BlockSpec((WINDOW, LANES), lambda b, i: (cur(b, i), kcol)),
                  pl.BlockSpec((WINDOW, LANES), lambda b, i: (prev(b, i), kcol)),
                  pl.BlockSpec((WINDOW, LANES), lambda b, i: (cur(b, i), vcol)),
                  pl.BlockSpec((WINDOW, LANES), lambda b, i: (prev(b, i), vcol))],
        out_specs=pl.BlockSpec((WINDOW, BRANCH_WIDTH), lambda b, i: (cur(b, i), 0)),
        out_shape=jax.ShapeDtypeStruct((m, BRANCH_WIDTH), bf16),
        compiler_params=_params(("parallel", "parallel")),
        name="swa_prompt",
    )(sinks, bias, u, u, u, u, u)


def _swa_sample_kernel(sink_ref, bias_ref, q_ref, kn_ref, vn_ref, kb_ref, vb_ref, o_ref, *, l_pad, reqs):
    for r in range(reqs):
        rs = slice(r * l_pad, (r + 1) * l_pad)
        bs = slice(r * WINDOW, (r + 1) * WINDOW)
        keys = (kb_ref[bs, :], kn_ref[rs, :])
        vals = (vb_ref[bs, :], vn_ref[rs, :])
        biases = [bias_ref[kv] for kv in range(SWA_KV_HEADS)]
        outs = _swa_groups(q_ref[rs, :], keys, vals, biases, sink_ref, l_pad)
        o_ref[rs, :] = jnp.concatenate(outs, axis=-1).astype(bf16)


def swa_sample(u, kbuf, vbuf, layer, sinks, batch, l_pad, l_real, reqs):
    m = u.shape[0]
    kcol, vcol = C_SK // LANES, C_SV // LANES
    key_pos = jnp.concatenate([jnp.arange(WINDOW) - WINDOW, jnp.arange(l_pad)])
    key_ok = jnp.concatenate([jnp.ones((WINDOW,), bool), jnp.arange(l_pad) < l_real])
    bias = _swa_bias(jnp.arange(l_pad), key_pos, key_ok)
    return pl.pallas_call(
        functools.partial(_swa_sample_kernel, l_pad=l_pad, reqs=reqs),
        grid=(batch // reqs,),
        in_specs=[pl.BlockSpec(memory_space=pltpu.SMEM),
                  pl.BlockSpec(bias.shape, lambda b: (0, 0, 0)),
                  pl.BlockSpec((reqs * l_pad, BRANCH_WIDTH), lambda b: (b, C_SQ // BRANCH_WIDTH)),
                  pl.BlockSpec((reqs * l_pad, LANES), lambda b: (b, kcol)),
                  pl.BlockSpec((reqs * l_pad, LANES), lambda b: (b, vcol)),
                  pl.BlockSpec((None, reqs * WINDOW, LANES), lambda b: (layer, b, 0)),
                  pl.BlockSpec((None, reqs * WINDOW, LANES), lambda b: (layer, b, 0))],
        out_specs=pl.BlockSpec((reqs * l_pad, BRANCH_WIDTH), lambda b: (b, 0)),
        out_shape=jax.ShapeDtypeStruct((m, BRANCH_WIDTH), bf16),
        compiler_params=_params(("parallel",)),
        name="swa_sample",
    )(sinks, bias, u, u, u, kbuf, vbuf)


def _mem_kernel(q_ref, k_ref, v_ref, o_ref):
    for h in range(MEM_HEADS):
        hs = slice(h * MEM_HEAD_DIM, (h + 1) * MEM_HEAD_DIM)
        s = _dot_nt(q_ref[:, hs].astype(bf16), k_ref[:, hs].astype(bf16)) * (MEM_HEAD_DIM ** -0.5)
        p = jnp.exp(s - jnp.max(s, axis=-1, keepdims=True))
        o = _dot(p.astype(bf16), v_ref[:, hs].astype(bf16)) / jnp.sum(p, axis=-1, keepdims=True)
        o_ref[:, hs] = o.astype(bf16)


def mem_attention(u, kv, cache, layer, batch, l_pad, tl):
    m = u.shape[0]
    nl = l_pad // tl
    width = MEM_HEADS * MEM_HEAD_DIM
    if cache is None:
        kv_args = (kv, kv)
        kv_specs = [pl.BlockSpec((MEM_TOKENS, width), lambda b, i: (b, 0)),
                    pl.BlockSpec((MEM_TOKENS, width), lambda b, i: (b, 1))]
    else:
        kv_args = cache
        kv_specs = [pl.BlockSpec((MEM_TOKENS, width), lambda b, i: (layer * batch + b, 0))] * 2
    return pl.pallas_call(
        _mem_kernel,
        grid=(batch, nl),
        in_specs=[pl.BlockSpec((tl, width), lambda b, i: (b * nl + i, C_MQ // width))] + kv_specs,
        out_specs=pl.BlockSpec((tl, width), lambda b, i: (b * nl + i, 0)),
        out_shape=jax.ShapeDtypeStruct((m, width), bf16),
        compiler_params=_params(("parallel", "parallel")),
        name="mem_attention",
    )(u, *kv_args)


def _seg64_sum(x, ones_ref):
    hi = x.astype(bf16)
    lo = (x - hi.astype(f32)).astype(bf16)
    cols = []
    for j in range(x.shape[1] // LANES):
        js = slice(j * LANES, (j + 1) * LANES)
        cols.append(_dot(hi[:, js], ones_ref[...]) + _dot(lo[:, js], ones_ref[...]))
    return jnp.concatenate(cols, axis=-1)


def _rwkv_features(xs, prevs, w_refs, ones_ref):
    mur_ref, muk_ref, muv_ref, mulo_ref, w0_ref, a0_ref, wl_ref, kk_ref, ka_ref, rk_ref = w_refs

    def token_shift(x, prev, mu_ref):
        row = lax.broadcasted_iota(jnp.int32, x.shape, 0)
        shifted = jnp.where(row == 0, prev, pltpu.roll(x, 1, 0))
        return x + (shifted - x) * mu_ref[...]

    r, k0, v, lo = (token_shift(x, pv, mu) for x, pv, mu in zip(xs, prevs, (mur_ref, muk_ref, muv_ref, mulo_ref)))
    col = lax.broadcasted_iota(jnp.int32, lo.shape, 1)
    act = jnp.where(col < 64, jnp.tanh(lo), jnp.where(col < 128, lo, _sigmoid(lo)))
    proj = _dot(act.astype(bf16), wl_ref[...])
    log_w = -_softplus(-(w0_ref[...] + proj[:, :BRANCH_WIDTH])) - 0.5
    log_decay = -jnp.exp(log_w)
    a = _sigmoid(a0_ref[...] + proj[:, BRANCH_WIDTH:2 * BRANCH_WIDTH])
    g = proj[:, 2 * BRANCH_WIDTH:]
    kk = k0 * kk_ref[...]
    k = k0 * (1.0 + (a - 1.0) * ka_ref[...])
    tb = kk.shape[0]
    head_sums = _seg64_sum(jnp.concatenate([kk * kk, r * k * rk_ref[...]], axis=0), ones_ref)
    kk = kk / jnp.maximum(jnp.sqrt(head_sums[:tb]), 1e-12)
    bonus = head_sums[tb:] * v
    return r, k, v, kk, a, log_decay, g, bonus


def _rwkv_output(y, bonus, g, lg_ref, lb_ref, ones_ref):
    mu = _seg64_sum(y, ones_ref) * (1.0 / RWKV_HEAD)
    d = y - mu
    var = _seg64_sum(d * d, ones_ref) * (1.0 / RWKV_HEAD)
    yn = d * lax.rsqrt(var + 64e-5) * lg_ref[...] + lb_ref[...]
    return ((yn + bonus) * g).astype(bf16)


def _rwkv_seq_kernel(*refs, tb, steps, reqs):
    x_refs, p_refs, w_refs = refs[0:4], refs[4:8], refs[8:18]
    lg_ref, lb_ref, ones_ref, s0_ref, o_ref, s_ref = refs[18:24]
    w3, kk3, kka3, k3, r3, v3, y3, bonus_s, g_s = refs[24:]
    s_ref[...] = s0_ref[...]
    y3[...] = jnp.zeros_like(y3)

    for q in range(reqs):
        rows = slice(q * tb, (q + 1) * tb)
        xs = [x[rows, :] for x in x_refs]
        prevs = [p[q] for p in p_refs]
        r, k, v, kk, a, log_decay, g, bonus = _rwkv_features(xs, prevs, w_refs, ones_ref)
        decay = jnp.exp(log_decay)
        kka = kk * a
        g_s[rows, :] = g
        bonus_s[rows, :] = bonus
        for h in range(RWKV_HEADS):
            hs = slice(h * RWKV_HEAD, (h + 1) * RWKV_HEAD)
            j = q * RWKV_HEADS + h
            w3[j] = decay[:, hs]
            kk3[j] = kk[:, hs]
            kka3[j] = kka[:, hs]
            k3[j] = k[:, hs]
            r3[j] = r[:, hs]
            v3[j] = v[:, hs]

    eye = (lax.broadcasted_iota(jnp.int32, (RWKV_HEAD, RWKV_HEAD), 0)
           == lax.broadcasted_iota(jnp.int32, (RWKV_HEAD, RWKV_HEAD), 1)).astype(f32)

    def step(t, carry):
        ts = pl.ds(t, 1)
        chains = [(q, h, q * RWKV_HEADS + h) for q in range(reqs) for h in range(RWKV_HEADS)]
        s_olds = [s_ref[q, h] for q, h, _ in chains]
        sas = [jnp.sum(s * kk3[j, ts, :], axis=1, keepdims=True) for s, (_, _, j) in zip(s_olds, chains)]
        v_cols = [jnp.sum(eye * v3[j, ts, :], axis=1, keepdims=True) for _, _, j in chains]
        s_news = [s * w3[j, ts, :] - sa * kka3[j, ts, :] + vc * k3[j, ts, :]
                  for s, sa, vc, (_, _, j) in zip(s_olds, sas, v_cols, chains)]
        for s_new, (q, h, _) in zip(s_news, chains):
            s_ref[q, h] = s_new
        y_cols = [jnp.sum(s * r3[j, ts, :], axis=1, keepdims=True) for s, (_, _, j) in zip(s_news, chains)]
        for y_col, (_, _, j) in zip(y_cols, chains):
            y3[j, ts, :] = jnp.sum(eye * y_col, axis=0, keepdims=True)
        return carry

    lax.fori_loop(0, steps, step, 0)

    for q in range(reqs):
        rows = slice(q * tb, (q + 1) * tb)
        y = jnp.concatenate([y3[q * RWKV_HEADS + h] for h in range(RWKV_HEADS)], axis=-1)
        o_ref[rows, :] = _rwkv_output(y, bonus_s[rows, :], g_s[rows, :], lg_ref, lb_ref, ones_ref)


def _unit_lower_inverse_minus_identity(ns):
    size = ns[0].shape[0]
    t = lax.broadcasted_iota(jnp.int32, (size, size), 0)
    s = lax.broadcasted_iota(jnp.int32, (size, size), 1)
    first = ((t >> 1) == (s >> 1)) & (t > s)
    es = [-jnp.where(first, n, 0.0) for n in ns]
    blk, shift = 4, 2
    while blk <= RWKV_HEAD:
        half = blk // 2
        sel = ((t >> shift) == (s >> shift)) & ((t & (blk - 1)) >= half) & ((s & (blk - 1)) < half)
        cs = [jnp.where(sel, n, 0.0) for n in ns]
        zs = [c + _dot(c.astype(bf16), e.astype(bf16)) for c, e in zip(cs, es)]
        es = [e - z - _dot(e.astype(bf16), z.astype(bf16)) for e, z in zip(es, zs)]
        blk, shift = blk * 2, shift + 1
    return es


def _rwkv_chunk_kernel(*refs, tb, nb):
    x_refs, p_refs, w_refs = refs[0:4], refs[4:8], refs[8:18]
    lg_ref, lb_ref, ones_ref, s0_ref, o_ref, s_ref = refs[18:24]
    carries = refs[24:28]
    sp = refs[28]
    i = pl.program_id(1)
    n_pairs = RWKV_HEADS // 2
    hd = RWKV_HEAD

    @pl.when(i == 0)
    def _():
        zero = jnp.zeros((hd, hd), f32)
        for q in range(nb):
            for p in range(n_pairs):
                top = jnp.concatenate([s0_ref[q, 2 * p], zero], axis=1)
                bot = jnp.concatenate([zero, s0_ref[q, 2 * p + 1]], axis=1)
                sp[q * n_pairs + p] = jnp.concatenate([top, bot], axis=0)
            for carry, p_ref in zip(carries, p_refs):
                carry[q] = p_ref[q]

    def sequence_features(q):
        xs = [x[q] for x in x_refs]
        prevs = [carry[q] for carry in carries]
        for carry, x in zip(carries, xs):
            carry[q] = x[tb - 1:tb, :]
        r, k, v, kk, a, lw, g, bonus = _rwkv_features(xs, prevs, w_refs, ones_ref)
        beta = kk * a
        row = lax.broadcasted_iota(jnp.int32, lw.shape, 0)
        cum = lw
        d = 1
        while d < tb:
            cum = cum + jnp.where(row >= d, pltpu.roll(cum, d, 0), 0.0)
            d *= 2
        c_last = cum[tb - 1:tb, :]
        e_neg = jnp.exp(-cum)
        e_hat = jnp.exp(c_last - cum)
        return {'a_t': kk * jnp.exp(cum - lw), 'r_t': r * jnp.exp(cum), 'k_t': k * e_neg, 'b_t': beta * e_neg,
                'k_h': k * e_hat, 'b_h': beta * e_hat, 'gamma': jnp.exp(c_last), 'v': v, 'g': g, 'bonus': bonus}

    seqs = [sequence_features(q) for q in range(nb)]

    lane = lax.broadcasted_iota(jnp.int32, (tb, LANES), 1)
    head0 = lane < hd
    split = lambda x: (jnp.where(head0, x, 0.0), jnp.where(head0, 0.0, x))
    tt = lax.broadcasted_iota(jnp.int32, (LANES, LANES), 0)
    ss = lax.broadcasted_iota(jnp.int32, (LANES, LANES), 1)
    strict = tt > ss
    incl = tt >= ss
    same_head = (tt >= hd) == (ss >= hd)
    fold = lambda m: m[:tb, :] + m[tb:, :]

    units = [(q, p) for q in range(nb) for p in range(n_pairs)]
    piece = lambda name: [seqs[q][name][:, p * LANES:(p + 1) * LANES] for q, p in units]
    a_ts, r_ts, k_ts, b_ts, k_hs, b_hs, vs, gammas = (piece(n) for n in
                                                      ('a_t', 'r_t', 'k_t', 'b_t', 'k_h', 'b_h', 'v', 'gamma'))
    stack = lambda *xs: jnp.concatenate(xs, axis=0)
    scs = [_dot_nt(stack(*split(a_t), *split(r_t)).astype(bf16), stack(*split(k_t), *split(b_t)).astype(bf16))
           for a_t, r_t, k_t, b_t in zip(a_ts, r_ts, k_ts, b_ts)]
    es = _unit_lower_inverse_minus_identity([jnp.where(strict, sc[:2 * tb, 2 * tb:], 0.0) for sc in scs])
    eye = jnp.where(tt == ss, 1.0, 0.0)
    t_cats = [fold(eye + e).astype(bf16) for e in es]
    ak_cats = [fold(jnp.where(strict, sc[:2 * tb, :2 * tb], 0.0)).astype(bf16) for sc in scs]
    r_cats = [jnp.concatenate([fold(jnp.where(incl, sc[2 * tb:, :2 * tb], 0.0)),
                               -fold(jnp.where(incl, sc[2 * tb:, 2 * tb:], 0.0))], axis=1).astype(bf16)
              for sc in scs]
    s_olds = [sp[q * n_pairs + p] for q, p in units]
    grs = [_dot_nt(stack(a_t, r_t).astype(bf16), s_old.astype(bf16))
           for a_t, r_t, s_old in zip(a_ts, r_ts, s_olds)]
    v_sts = [stack(*split(v)).astype(bf16) for v in vs]
    u_rhss = [gr[:tb] + _dot(ak, v_st) for gr, ak, v_st in zip(grs, ak_cats, v_sts)]
    us = [_dot(t_cat, stack(*split(u_rhs)).astype(bf16)) for t_cat, u_rhs in zip(t_cats, u_rhss)]
    ys = [gr[tb:] + _dot(r_cat, stack(v_st, stack(*split(u)).astype(bf16)))
          for gr, r_cat, v_st, u in zip(grs, r_cats, v_sts, us)]
    for (q, p), s_old, u, v, k_h, b_h, gamma in zip(units, s_olds, us, vs, k_hs, b_hs, gammas):
        vu_t = stack(v, -u).T.astype(bf16)
        kb = stack(k_h, b_h).astype(bf16)
        sp[q * n_pairs + p] = s_old * gamma + jnp.where(same_head, _dot(vu_t, kb), 0.0)

    for q in range(nb):
        y = jnp.concatenate(ys[q * n_pairs:(q + 1) * n_pairs], axis=-1)
        o_ref[q] = _rwkv_output(y, seqs[q]['bonus'], seqs[q]['g'], lg_ref, lb_ref, ones_ref)

    @pl.when(i == pl.num_programs(1) - 1)
    def _():
        for q in range(nb):
            for p in range(n_pairs):
                full = sp[q * n_pairs + p]
                s_ref[q, 2 * p] = full[:hd, :hd]
                s_ref[q, 2 * p + 1] = full[hd:, hd:]


def rwkv(u, prev, p, s0, layer, ones2, batch, l_pad, l_real, chunked, reqs=1):
    m = u.shape[0]
    bw = BRANCH_WIDTH
    full = lambda w: pl.BlockSpec((1, w), lambda b, i: (0, 0))
    if chunked:
        tb, nb, nblk = RWKV_HEAD, 2, l_pad // RWKV_HEAD
        assert l_real == l_pad and l_pad % tb == 0 and batch % nb == 0
        body = functools.partial(_rwkv_chunk_kernel, tb=tb, nb=nb)
        scratch = [pltpu.VMEM((nb, 1, bw), f32), pltpu.VMEM((nb, 1, bw), f32), pltpu.VMEM((nb, 1, bw), f32),
                   pltpu.VMEM((nb, 1, RWKV_LORA), f32), pltpu.VMEM((nb * RWKV_HEADS // 2, LANES, LANES), f32)]
        u = u.reshape(batch, l_pad, u.shape[1])
        x_spec = lambda w, c0: pl.BlockSpec((nb, tb, w), lambda b, i: (b, i, c0 // w))
        o_spec = pl.BlockSpec((nb, tb, bw), lambda b, i: (b, i, 0))
        o_shape = jax.ShapeDtypeStruct((batch, l_pad, bw), bf16)
    else:
        tb, nb, nblk = l_pad, reqs, 1
        assert batch % reqs == 0
        body = functools.partial(_rwkv_seq_kernel, tb=tb, steps=l_real, reqs=reqs)
        head_buf = pltpu.VMEM((reqs * RWKV_HEADS, tb, RWKV_HEAD), f32)
        scratch = [head_buf] * 7 + [pltpu.VMEM((reqs * tb, bw), f32), pltpu.VMEM((reqs * tb, bw), f32)]
        x_spec = lambda w, c0: pl.BlockSpec((nb * tb, w), lambda b, i: (b, c0 // w))
        o_spec = pl.BlockSpec((nb * tb, bw), lambda b, i: (b, 0))
        o_shape = jax.ShapeDtypeStruct((m, bw), bf16)
    prev_spec = lambda w: pl.BlockSpec((None, nb, 1, w), lambda b, i: (layer, b, 0, 0))
    out, state = pl.pallas_call(
        body,
        grid=(batch // nb, nblk),
        in_specs=[x_spec(bw, C_R), x_spec(bw, C_K), x_spec(bw, C_V), x_spec(RWKV_LORA, C_LORA),
                  prev_spec(bw), prev_spec(bw), prev_spec(bw), prev_spec(RWKV_LORA),
                  full(bw), full(bw), full(bw), full(RWKV_LORA),
                  full(bw), full(bw),
                  pl.BlockSpec((RWKV_LORA, 3 * bw), lambda b, i: (0, 0)),
                  full(bw), full(bw), full(bw), full(bw), full(bw),
                  pl.BlockSpec((LANES, LANES), lambda b, i: (0, 0)),
                  pl.BlockSpec((None, nb, RWKV_HEADS, RWKV_HEAD, RWKV_HEAD), lambda b, i: (layer, b, 0, 0, 0))],
        out_specs=[o_spec,
                   pl.BlockSpec((nb, RWKV_HEADS, RWKV_HEAD, RWKV_HEAD), lambda b, i: (b, 0, 0, 0))],
        out_shape=[o_shape, jax.ShapeDtypeStruct((batch, RWKV_HEADS, RWKV_HEAD, RWKV_HEAD), f32)],
        scratch_shapes=scratch,
        compiler_params=_params(("parallel", "arbitrary")),
        name="rwkv7_chunked" if chunked else "rwkv7_seq",
    )(u, u, u, u, *prev, p['mu_r'], p['mu_k'], p['mu_v'], p['mu_lo'], p['w0'], p['a0'], p['w_lora'],
      p['k_k'], p['k_a'], p['r_k'], p['ln_g'], p['ln_b'], ones2, s0)
    return out.reshape(m, bw), state


IN_TILE = 1024
IN_TAIL_TILE = C_LORA // IN_TILE


def _in_proj_weights(w_in):
    wt = jnp.swapaxes(w_in, 1, 2).astype(bf16)
    seg = lambda o, n: wt[:, o:o + n]
    tail = jnp.concatenate([seg(_O_RU + 3072, RWKV_LORA), seg(_O_SK, 128), seg(_O_SV, 128),
                            seg(_O_GA, GLA_GATE_RANK),
                            jnp.zeros((wt.shape[0], N_PACK - C_GA - GLA_GATE_RANK, wt.shape[2]), bf16)], axis=1)
    return wt, tail


def _in_tile_source(j):
    return jnp.where(j < 2, _O_GV + IN_TILE * j,
           jnp.where(j == 2, _O_SQ,
           jnp.where(j == 3, _O_MQ,
           jnp.where(j < 7, _O_RU + IN_TILE * (j - 4),
           jnp.where(j < 15, _O_GPRE + IN_TILE * (j - 7), _O_GQ)))))


def _in_proj_kernel(x_ref, w_ref, tail_ref, o_ref):
    j = pl.program_id(1)

    @pl.when(j != IN_TAIL_TILE)
    def _():
        o_ref[...] = _dot_nt(x_ref[...], w_ref[0])

    @pl.when(j == IN_TAIL_TILE)
    def _():
        o_ref[...] = _dot_nt(x_ref[...], tail_ref[...])


def in_projection(x, wt, tail, layer, tm):
    m, k = x.shape
    return pl.pallas_call(
        _in_proj_kernel,
        grid=(m // tm, N_PACK // IN_TILE),
        in_specs=[pl.BlockSpec((tm, k), lambda i, j: (i, 0)),
                  pl.BlockSpec((pl.Element(1), pl.Element(IN_TILE), pl.Element(k)),
                               lambda i, j: (layer, pl.multiple_of(_in_tile_source(j), 16), 0)),
                  pl.BlockSpec((None, IN_TILE, k), lambda i, j: (layer, 0, 0))],
        out_specs=pl.BlockSpec((tm, IN_TILE), lambda i, j: (i, j)),
        out_shape=jax.ShapeDtypeStruct((m, N_PACK), f32),
        compiler_params=_params(("parallel", "parallel")),
        name="in_proj",
    )(x, wt, tail)


def _layer_params(l, w):
    row = lambda x: x[l].reshape(1, -1)
    mu = w['rwkv_mu'][l]
    z = lambda r, c: jnp.zeros((r, c), f32)
    w_lora = jnp.concatenate([
        jnp.concatenate([w['rwkv_w2'][l], z(64, 1024), z(64, 1024)], axis=1),
        jnp.concatenate([z(64, 1024), w['rwkv_a2'][l], z(64, 1024)], axis=1),
        jnp.concatenate([z(128, 1024), z(128, 1024), w['rwkv_g2'][l]], axis=1)], axis=0).astype(bf16)
    a_up = jnp.concatenate([w['gla_a_up'][l], z(LANES - GLA_GATE_RANK, GLA_QK)], axis=0).astype(bf16)
    return {
        'a_up': a_up, 'a_b': row(w['gla_a_b']),
        'gla_g': row(w['gla_norm_g']), 'gla_b': row(w['gla_norm_b']),
        'sinks': w['swa_sinks'][l],
        'rwkv': {'mu_r': mu[:1024].reshape(1, -1), 'mu_k': mu[1024:2048].reshape(1, -1),
                 'mu_v': mu[2048:3072].reshape(1, -1), 'mu_lo': mu[3072:].reshape(1, -1),
                 'w0': row(w['rwkv_w0']), 'a0': row(w['rwkv_a0']), 'w_lora': w_lora,
                 'k_k': row(w['rwkv_k_k']), 'k_a': row(w['rwkv_k_a']), 'r_k': row(w['rwkv_r_k']),
                 'ln_g': row(w['rwkv_ln_g']), 'ln_b': row(w['rwkv_ln_b'])},
        'ln1_g': row(w['ln1_g']), 'ln1_b': row(w['ln1_b']),
        'ln2_g': row(w['ln2_g']), 'ln2_b': row(w['ln2_b']),
    }


def _tiles(m):
    if m >= 1024:
        return {'proj': 2048, 'merge': 512, 'ffn': 2048, 'ln_out': 512, 'ln_down': 256}
    return {'proj': m, 'merge': m, 'ffn': m, 'ln_out': m, 'ln_down': m}


def _trunk_layer(h_f, h_b, l, p, big, ones2, mem_kv, mem_cache, gla_s0, rwkv_s0, rwkv_prev, swa_buf,
                 batch, l_pad, l_real):
    m = h_f.shape[0]
    tl = _tiles(m)
    u = in_projection(h_b, big['w_in_t'], big['w_in_tail'], l, tl['proj'])
    o_a, gla_s = gla(u, p['a_up'], p['a_b'], p['gla_g'], p['gla_b'], gla_s0[0], gla_s0[1], batch, l_pad, l_real,
                     min(GLA_CHUNK, l_pad))
    if swa_buf is None:
        o_b = swa_prompt(u, p['sinks'], batch, l_pad)
    else:
        o_b = swa_sample(u, swa_buf[0], swa_buf[1], l, p['sinks'], batch, l_pad, l_real, 4)
    chunked = l_real == l_pad and l_pad % RWKV_HEAD == 0
    o_c, rwkv_s = rwkv(u, rwkv_prev[0], p['rwkv'], rwkv_s0[0], rwkv_s0[1], ones2, batch, l_pad, l_real,
                       chunked, reqs=1 if chunked else 4)
    o_m = mem_attention(u, mem_kv, mem_cache, l, batch, l_pad, min(512, l_pad))
    merged = gated_merge((o_a, o_b, o_c, o_m), u, big['gate_b'], big['w_branch'], l, tl['merge'], 512)
    x_f, x_b = matmul_residual_ln(merged, big['w_out'], l, h_f, p['ln1_g'], p['ln1_b'], tl['ln_out'])
    act = ffn_up(x_b, big['w_gu'], l, tl['ffn'], 512)
    y_f, y_b = matmul_residual_ln(act, big['w_down'], l, x_f, p['ln2_g'], p['ln2_b'], tl['ln_down'])
    return y_f, y_b, u, gla_s, rwkv_s


def _split_ru(x):
    return (x[..., :1024], x[..., 1024:2048], x[..., 2048:3072], x[..., 3072:])


def kernel(x_prompt, x_sample, mem_prompt, cache_swa_k, cache_swa_v, cache_mem_k, cache_mem_v, state_gla, state_rwkv, state_rwkv_shift, w_in, gate_b, gla_a_up, gla_a_b, gla_norm_g, gla_norm_b, swa_sinks, rwkv_mu, rwkv_w0, rwkv_w2, rwkv_a0, rwkv_a2, rwkv_g2, rwkv_k_k, rwkv_k_a, rwkv_r_k, rwkv_ln_g, rwkv_ln_b, w_mem_kv, w_branch, w_out, ln1_g, ln1_b, w_gu, w_down, ln2_g, ln2_b):
    weights = {'gla_a_up': gla_a_up, 'gla_a_b': gla_a_b,
               'gla_norm_g': gla_norm_g, 'gla_norm_b': gla_norm_b, 'swa_sinks': swa_sinks,
               'rwkv_mu': rwkv_mu, 'rwkv_w0': rwkv_w0, 'rwkv_w2': rwkv_w2, 'rwkv_a0': rwkv_a0,
               'rwkv_a2': rwkv_a2, 'rwkv_g2': rwkv_g2, 'rwkv_k_k': rwkv_k_k, 'rwkv_k_a': rwkv_k_a,
               'rwkv_r_k': rwkv_r_k, 'rwkv_ln_g': rwkv_ln_g, 'rwkv_ln_b': rwkv_ln_b,
               'ln1_g': ln1_g, 'ln1_b': ln1_b, 'ln2_g': ln2_g, 'ln2_b': ln2_b}
    w_in_t, w_in_tail = _in_proj_weights(w_in)
    big = {'w_in_t': w_in_t, 'w_in_tail': w_in_tail, 'gate_b': gate_b, 'w_branch': w_branch, 'w_gu': w_gu,
           'w_out': w_out.astype(bf16), 'w_down': w_down.astype(bf16)}
    bp, lp, _ = x_prompt.shape
    bs, ls, _ = x_sample.shape
    ls_pad = -(-ls // SUBLANES) * SUBLANES
    mp, ms = bp * lp, bs * ls_pad

    hp_f = x_prompt.reshape(mp, D_MODEL)
    hs_f = jnp.pad(x_sample, ((0, 0), (0, ls_pad - ls), (0, 0))).reshape(ms, D_MODEL)
    hp_b, hs_b = hp_f.astype(bf16), hs_f.astype(bf16)
    mem_b = mem_prompt.reshape(bp * MEM_TOKENS, D_MODEL).astype(bf16)
    half = LANES // 2
    blk = jnp.ones((half, half), f32)
    zero = jnp.zeros((half, half), f32)
    ones2 = jnp.concatenate([jnp.concatenate([blk, zero], 1), jnp.concatenate([zero, blk], 1)], 0).astype(bf16)

    gla0_p = jnp.zeros((1, bp, GLA_HEADS, GLA_DK, GLA_DV), f32)
    rwkv0_p = jnp.zeros((1, bp, RWKV_HEADS, RWKV_HEAD, RWKV_HEAD), f32)
    prev0_p = _split_ru(jnp.zeros((1, bp, 1, RWKV_COLS), f32))
    prev_s = _split_ru(state_rwkv_shift)
    kvw = SWA_KV_HEADS * SWA_HEAD_DIM
    kbuf = cache_swa_k.reshape(DEPTH, bs * WINDOW, kvw)
    vbuf = cache_swa_v.reshape(DEPTH, bs * WINDOW, kvw)
    mem_k2d = cache_mem_k.reshape(DEPTH * bs * MEM_TOKENS, MEM_HEADS * MEM_HEAD_DIM)
    mem_v2d = cache_mem_v.reshape(DEPTH * bs * MEM_TOKENS, MEM_HEADS * MEM_HEAD_DIM)

    outs = {k: [] for k in ('p_swk', 'p_swv', 'p_mk', 'p_mv', 'p_gla', 'p_rw', 'p_rs',
                            's_swk', 's_swv', 's_gla', 's_rw', 's_rs')}
    for l in range(DEPTH):
        p = _layer_params(l, weights)
        kv = matmul(mem_b, w_mem_kv, l, bp * MEM_TOKENS, 512, "mem_kv")
        hp_f, hp_b, u, gs, rs = _trunk_layer(hp_f, hp_b, l, p, big, ones2, kv, None, (gla0_p, 0), (rwkv0_p, 0),
                                             (prev0_p, 0), None, bp, lp, lp)
        u3 = u.reshape(bp, lp, N_PACK)
        outs['p_swk'].append(u3[:, lp - WINDOW:, C_SK:C_SK + kvw].reshape(bp, WINDOW, SWA_KV_HEADS, SWA_HEAD_DIM))
        outs['p_swv'].append(u3[:, lp - WINDOW:, C_SV:C_SV + kvw].reshape(bp, WINDOW, SWA_KV_HEADS, SWA_HEAD_DIM))
        outs['p_mk'].append(kv[:, :1024].reshape(bp, MEM_TOKENS, MEM_HEADS, MEM_HEAD_DIM))
        outs['p_mv'].append(kv[:, 1024:].reshape(bp, MEM_TOKENS, MEM_HEADS, MEM_HEAD_DIM))
        outs['p_gla'].append(gs)
        outs['p_rw'].append(rs)
        outs['p_rs'].append(jnp.concatenate([u3[:, lp - 1:, C_R:C_R + 3072],
                                             u3[:, lp - 1:, C_LORA:C_LORA + RWKV_LORA]], axis=-1))
        hs_f, hs_b, u, gs, rs = _trunk_layer(hs_f, hs_b, l, p, big, ones2, None, (mem_k2d, mem_v2d),
                                             (state_gla, l), (state_rwkv, l), (prev_s, l), (kbuf, vbuf),
                                             bs, ls_pad, ls)
        u3 = u.reshape(bs, ls_pad, N_PACK)
        k_new = u3[:, :ls, C_SK:C_SK + kvw].reshape(bs, ls, SWA_KV_HEADS, SWA_HEAD_DIM)
        v_new = u3[:, :ls, C_SV:C_SV + kvw].reshape(bs, ls, SWA_KV_HEADS, SWA_HEAD_DIM)
        outs['s_swk'].append(jnp.concatenate([cache_swa_k[l][:, ls:], k_new], axis=1))
        outs['s_swv'].append(jnp.concatenate([cache_swa_v[l][:, ls:], v_new], axis=1))
        outs['s_gla'].append(gs)
        outs['s_rw'].append(rs)
        outs['s_rs'].append(jnp.concatenate([u3[:, ls - 1:ls, C_R:C_R + 3072],
                                             u3[:, ls - 1:ls, C_LORA:C_LORA + RWKV_LORA]], axis=-1))

    st = {k: jnp.stack(v) for k, v in outs.items()}
    y_prompt = hp_f.reshape(bp, lp, D_MODEL)
    y_sample = hs_f.reshape(bs, ls_pad, D_MODEL)[:, :ls]
    return (y_prompt, y_sample,
            st['p_swk'], st['p_swv'], st['p_mk'], st['p_mv'], st['p_gla'], st['p_rw'], st['p_rs'],
            st['s_swk'], st['s_swv'], st['s_gla'], st['s_rw'], st['s_rs'])
```
